```python
import math
import jax, jax.numpy as jnp
from jax import lax
import numpy as np

D_MODEL = 2048
BATCH = 4
SEQ = 4096
DEPTH = 4

GRID_W = 64
CTX_LEN = 256
MIX_WIDTH = D_MODEL
ATTN_WIDTH = MIX_WIDTH // 2
RWKV_WIDTH = MIX_WIDTH - ATTN_WIDTH
HEAD_DIM = 128
N_Q_HEADS = ATTN_WIDTH // HEAD_DIM
N_KV_HEADS = 2
GQA_GROUP = N_Q_HEADS // N_KV_HEADS
RWKV_HEAD = 64
N_RWKV_HEADS = RWKV_WIDTH // RWKV_HEAD
LORA_DECAY = 96
LORA_ICLR = 96
LORA_GATE = 256
D_FF = -(-8 * D_MODEL // (3 * 256)) * 256
Q_BLOCK = 128
ROPE_THETA = 10000.0
ROPE_FREQS = HEAD_DIM // 4
ATTN_SCALE = HEAD_DIM ** -0.5
NORM_EPS = 1e-6
GN_EPS = 64e-5
L2_EPS = 1e-12

Q_COLS = N_Q_HEADS * HEAD_DIM
KV_COLS = N_KV_HEADS * HEAD_DIM
ATTN_COLS = Q_COLS + 2 * KV_COLS
SHIFT_COLS = 3 * RWKV_WIDTH + LORA_GATE + 2 * LORA_DECAY + 2 * LORA_ICLR
PROJ_COLS = ATTN_COLS + SHIFT_COLS
ATTN_SPLITS = (Q_COLS, Q_COLS + KV_COLS)
RWKV_SPLITS = (RWKV_WIDTH, 2 * RWKV_WIDTH, 3 * RWKV_WIDTH,
               3 * RWKV_WIDTH + LORA_GATE, 3 * RWKV_WIDTH + LORA_GATE + 2 * LORA_DECAY)

kernel_name = "hymba_style_gqa_rwkv7_flow_backbone"


def rms_norm(x, g):
    xf = x.astype(jnp.float32)
    y = xf * lax.rsqrt(jnp.mean(xf * xf, axis=-1, keepdims=True) + NORM_EPS)
    return (y * g.astype(jnp.float32)).astype(x.dtype)


def split_heads(t, n_heads, dim):
    return t.reshape(t.shape[:-1] + (n_heads, dim))


def axial_rope_tables(n_tokens):
    rows = n_tokens // GRID_W
    row = jnp.broadcast_to(jnp.arange(rows)[:, None], (rows, GRID_W)).reshape(-1)
    col = jnp.broadcast_to(jnp.arange(GRID_W)[None, :], (rows, GRID_W)).reshape(-1)
    inv = ROPE_THETA ** (-jnp.arange(ROPE_FREQS, dtype=jnp.float32) / ROPE_FREQS)
    ang = jnp.concatenate([row[:, None].astype(jnp.float32) * inv,
                           col[:, None].astype(jnp.float32) * inv], axis=-1)
    return jnp.cos(ang), jnp.sin(ang)


def apply_rope(x, cos, sin):
    x1, x2 = jnp.split(x, 2, axis=-1)
    cos = cos[None, :, None, :]
    sin = sin[None, :, None, :]
    return jnp.concatenate([x1 * cos - x2 * sin, x2 * cos + x1 * sin], axis=-1).astype(x.dtype)


def latent_attention(q, k_all, v_all):
    b, s = q.shape[0], q.shape[1]
    nb = s // Q_BLOCK
    qb = q.reshape(b, nb, Q_BLOCK, N_KV_HEADS, GQA_GROUP, HEAD_DIM).swapaxes(0, 1)

    def one_block(qblk):
        sc = jnp.einsum('bqhgd,bshd->bhgqs', qblk, k_all).astype(jnp.float32) * ATTN_SCALE
        p = jax.nn.softmax(sc, axis=-1).astype(v_all.dtype)
        return jnp.einsum('bhgqs,bshd->bqhgd', p, v_all)

    o = lax.map(one_block, qb)
    return o.swapaxes(0, 1).reshape(b, s, ATTN_WIDTH)


def context_attention(q, k, v):
    sc = jnp.einsum('bqhgd,bshd->bhgqs',
                    q.reshape(q.shape[0], q.shape[1], N_KV_HEADS, GQA_GROUP, HEAD_DIM), k
                    ).astype(jnp.float32) * ATTN_SCALE
    p = jax.nn.softmax(sc, axis=-1).astype(v.dtype)
    o = jnp.einsum('bhgqs,bshd->bqhgd', p, v)
    return o.reshape(q.shape[0], q.shape[1], ATTN_WIDTH)


def centred_shift(f, mu_prev, mu_next):
    prev = jnp.pad(f[:, :-1], ((0, 0), (1, 0), (0, 0)))
    nxt = jnp.pad(f[:, 1:], ((0, 0), (0, 1), (0, 0)))
    return f + mu_prev * (prev - f) + mu_next * (nxt - f)


def rwkv_prepare(feat, p):
    feat = centred_shift(feat, p['mu_prev'], p['mu_next'])
    r, k, v, glr, wlr, alr = jnp.split(feat, RWKV_SPLITS, axis=-1)
    b, t = feat.shape[0], feat.shape[1]
    g = jax.nn.sigmoid(glr) @ p['g2']
    kk = split_heads((k * p['k_k']).astype(jnp.float32), N_RWKV_HEADS, RWKV_HEAD)
    kk = kk * lax.rsqrt(jnp.sum(kk * kk, axis=-1, keepdims=True) + L2_EPS)
    wlr = jnp.tanh(wlr.reshape(b, t, 2, LORA_DECAY))
    alr = alr.reshape(b, t, 2, LORA_ICLR)
    w_log = -jax.nn.softplus(-(p['w0'][:, None, None, :]
                               + jnp.einsum('btdr,drc->dbtc', wlr, p['w2']))) - 0.5
    decay = jnp.exp(-jnp.exp(w_log.astype(jnp.float32)))
    a = jax.nn.sigmoid(p['a0'][:, None, None, :]
                       + jnp.einsum('btdr,drc->dbtc', alr, p['a2']))
    k_dir = k[None] * (1.0 + (a - 1.0) * p['k_a'])
    hd = lambda z: split_heads(z, N_RWKV_HEADS, RWKV_HEAD)
    return (hd(r), hd(v), kk, g, hd(decay), hd(k_dir), hd(a))


def wkv_scan(s0, r, decay, k, v, kk, a, reverse):
    f32 = jnp.float32
    xs = tuple(jnp.moveaxis(z.astype(f32), 1, 0) for z in (r, decay, k, v, kk, kk * a))

    def step(S, inp):
        r_t, w_t, k_t, v_t, kk_t, b_t = inp
        sa = jnp.einsum('bhvk,bhk->bhv', S, kk_t)
        S = (S * w_t[:, :, None, :] - sa[..., None] * b_t[:, :, None, :]
             + v_t[..., None] * k_t[:, :, None, :])
        return S, jnp.einsum('bhvk,bhk->bhv', S, r_t)

    s_fin, ys = lax.scan(step, s0, xs, reverse=reverse)
    return s_fin, jnp.moveaxis(ys, 0, 1)


def rwkv_bidir(streams, s_init):
    r, v, kk, g, decay, k_dir, a = streams
    s_f, y_f = wkv_scan(s_init[0], r, decay[0], k_dir[0], v, kk, a[0], reverse=False)
    s_b, y_b = wkv_scan(s_init[1], r, decay[1], k_dir[1], v, kk, a[1], reverse=True)
    return jnp.stack([s_f, s_b]), y_f + y_b


def rwkv_output(y, streams, p, dtype):
    r, v, kk, g, decay, k_dir, a = streams
    mean = jnp.mean(y, axis=-1, keepdims=True)
    var = jnp.mean(jnp.square(y - mean), axis=-1, keepdims=True)
    yn = ((y - mean) * lax.rsqrt(var + GN_EPS)).reshape(y.shape[0], y.shape[1], RWKV_WIDTH)
    yn = yn * p['ln_x_w'] + p['ln_x_b']
    bonus = jnp.sum(jnp.sum(r[None] * k_dir * p['r_k'], axis=-1, keepdims=True) * v[None], axis=0)
    bonus = bonus.reshape(yn.shape)
    return ((yn + bonus) * g).astype(dtype)


def qk_heads(f, n_tok, p):
    q, k, v = jnp.split(f[..., :ATTN_COLS], ATTN_SPLITS, axis=-1)
    q = rms_norm(split_heads(q, N_Q_HEADS, HEAD_DIM), p['q_gain'])
    k = rms_norm(split_heads(k, N_KV_HEADS, HEAD_DIM), p['k_gain'])
    v = split_heads(v, N_KV_HEADS, HEAD_DIM)
    return q, k, v


def token_mixer(h, hc, rope_cos, rope_sin, p, need_ctx_out):
    f = h @ p['w_in']
    fc = hc @ p['w_in']
    qx, kx, vx = qk_heads(f, h.shape[1], p)
    qx = apply_rope(qx, rope_cos, rope_sin)
    kx = apply_rope(kx, rope_cos, rope_sin)
    qc, kc, vc = qk_heads(fc, hc.shape[1], p)
    attn_x = latent_attention(qx, jnp.concatenate([kx, kc], axis=1),
                              jnp.concatenate([vx, vc], axis=1))
    st_c = rwkv_prepare(fc[..., ATTN_COLS:], p)
    st_x = rwkv_prepare(f[..., ATTN_COLS:], p)
    s0 = jnp.zeros((2, hc.shape[0], N_RWKV_HEADS, RWKV_HEAD, RWKV_HEAD), jnp.float32)
    s_ctx, y_c = rwkv_bidir(st_c, s0)
    _, y_x = rwkv_bidir(st_x, s_ctx)
    rwkv_x = rwkv_output(y_x, st_x, p, h.dtype)
    out_x = jnp.concatenate([attn_x, rwkv_x], axis=-1) @ p['w_out']
    if not need_ctx_out:
        return out_x, None
    attn_c = context_attention(qc, kc, vc)
    rwkv_c = rwkv_output(y_c, st_c, p, hc.dtype)
    out_c = jnp.concatenate([attn_c, rwkv_c], axis=-1) @ p['w_out']
    return out_x, out_c


def swiglu(h, w1, w3, w2):
    return (jax.nn.silu(h @ w1) * (h @ w3)) @ w2


def setup_inputs(seed: int = 0) -> dict:
    key = jax.random.key(seed)
    ks = jax.random.split(key, 32)
    f32 = jnp.float32
    nrm = lambda k, shape, s: jax.random.normal(k, shape, f32) * s
    unif = lambda k, shape, lo, hi: jax.random.uniform(k, shape, f32, lo, hi)
    return {
        'x': nrm(ks[0], (BATCH, SEQ, D_MODEL), 1.0),
        'c': nrm(ks[1], (BATCH, D_MODEL), 1.0),
        'ctx': nrm(ks[2], (BATCH, CTX_LEN, D_MODEL), 1.0),
        'c_ctx': nrm(ks[3], (D_MODEL,), 1.0),
        'norm1_g': 1.0 + nrm(ks[4], (DEPTH, D_MODEL), 0.02),
        'norm2_g': 1.0 + nrm(ks[5], (DEPTH, D_MODEL), 0.02),
        'ada_w': nrm(ks[6], (DEPTH, D_MODEL, 6 * D_MODEL), 0.5 * D_MODEL ** -0.5),
        'ada_b': nrm(ks[7], (DEPTH, 6 * D_MODEL), 0.02),
        'w_in': nrm(ks[8], (DEPTH, D_MODEL, PROJ_COLS), D_MODEL ** -0.5),
        'q_gain': 1.0 + nrm(ks[9], (DEPTH, HEAD_DIM), 0.02),
        'k_gain': 1.0 + nrm(ks[10], (DEPTH, HEAD_DIM), 0.02),
        'mu_prev': unif(ks[11], (DEPTH, SHIFT_COLS), 0.0, 0.5),
        'mu_next': unif(ks[12], (DEPTH, SHIFT_COLS), 0.0, 0.5),
        'w0': unif(ks[13], (DEPTH, 2, RWKV_WIDTH), -6.0, -1.0),
        'w2': nrm(ks[14], (DEPTH, 2, LORA_DECAY, RWKV_WIDTH), 0.5 * LORA_DECAY ** -0.5),
        'a0': nrm(ks[15], (DEPTH, 2, RWKV_WIDTH), 0.1),
        'a2': nrm(ks[16], (DEPTH, 2, LORA_ICLR, RWKV_WIDTH), 0.5 * LORA_ICLR ** -0.5),
        'g2': nrm(ks[17], (DEPTH, LORA_GATE, RWKV_WIDTH), LORA_GATE ** -0.5),
        'k_k': 0.85 + nrm(ks[18], (DEPTH, RWKV_WIDTH), 0.02),
        'k_a': 1.0 + nrm(ks[19], (DEPTH, RWKV_WIDTH), 0.02),
        'r_k': nrm(ks[20], (DEPTH, N_RWKV_HEADS, RWKV_HEAD), 0.1),
        'ln_x_w': 1.0 + nrm(ks[21], (DEPTH, RWKV_WIDTH), 0.02),
        'ln_x_b': nrm(ks[22], (DEPTH, RWKV_WIDTH), 0.02),
        'w_out': nrm(ks[23], (DEPTH, MIX_WIDTH, D_MODEL), MIX_WIDTH ** -0.5),
        'ffn_w1': nrm(ks[24], (DEPTH, D_MODEL, D_FF), D_MODEL ** -0.5),
        'ffn_w3': nrm(ks[25], (DEPTH, D_MODEL, D_FF), D_MODEL ** -0.5),
        'ffn_w2': nrm(ks[26], (DEPTH, D_FF, D_MODEL), D_FF ** -0.5),
        'final_g': 1.0 + nrm(ks[27], (D_MODEL,), 0.02),
    }


def reference(x, c, ctx, c_ctx, norm1_g, norm2_g, ada_w, ada_b, w_in, q_gain, k_gain,
              mu_prev, mu_next, w0, w2, a0, a2, g2, k_k, k_a, r_k, ln_x_w, ln_x_b,
              w_out, ffn_w1, ffn_w3, ffn_w2, final_g):
    rope_cos, rope_sin = axial_rope_tables(x.shape[1])
    sc_lat = jax.nn.silu(c)
    sc_ctx = jax.nn.silu(c_ctx)
    for i in range(DEPTH):
        last = i == DEPTH - 1
        p = {'w_in': w_in[i], 'q_gain': q_gain[i], 'k_gain': k_gain[i],
             'mu_prev': mu_prev[i], 'mu_next': mu_next[i], 'w0': w0[i], 'w2': w2[i],
             'a0': a0[i], 'a2': a2[i], 'g2': g2[i], 'k_k': k_k[i], 'k_a': k_a[i],
             'r_k': r_k[i], 'ln_x_w': ln_x_w[i], 'ln_x_b': ln_x_b[i], 'w_out': w_out[i]}
        mod = (sc_lat @ ada_w[i] + ada_b[i])[:, None, :]
        mod_c = sc_ctx @ ada_w[i] + ada_b[i]
        sh1, scl1, gt1, sh2, scl2, gt2 = jnp.split(mod, 6, axis=-1)
        ch1, ccl1, cgt1, ch2, ccl2, cgt2 = jnp.split(mod_c, 6, axis=-1)
        h = rms_norm(x, norm1_g[i]) * (1.0 + scl1) + sh1
        hc = rms_norm(ctx, norm1_g[i]) * (1.0 + ccl1) + ch1
        out_x, out_c = token_mixer(h, hc, rope_cos, rope_sin, p, not last)
        x = x + gt1 * out_x
        h = rms_norm(x, norm2_g[i]) * (1.0 + scl2) + sh2
        x = x + gt2 * swiglu(h, ffn_w1[i], ffn_w3[i], ffn_w2[i])
        if not last:
            ctx = ctx + cgt1 * out_c
            hc = rms_norm(ctx, norm2_g[i]) * (1.0 + ccl2) + ch2
            ctx = ctx + cgt2 * swiglu(hc, ffn_w1[i], ffn_w3[i], ffn_w2[i])
    return rms_norm(x, final_g)
```

```python
import functools

import jax
import jax.numpy as jnp
from jax import lax
from jax.experimental import pallas as pl
from jax.experimental.pallas import tpu as pltpu

F32 = jnp.float32
BF16 = jnp.bfloat16

D_MODEL = 2048
HEAD_DIM = 128
N_Q_HEADS = 8
N_KV_HEADS = 2
GQA_GROUP = N_Q_HEADS // N_KV_HEADS
Q_COLS = N_Q_HEADS * HEAD_DIM
KV_COLS = N_KV_HEADS * HEAD_DIM
ATTN_COLS = Q_COLS + 2 * KV_COLS
RWKV_WIDTH = 1024
RWKV_HEAD = 64
LORA_GATE = 256
LORA_DECAY = 96
LORA_ICLR = 96
LORA_COLS = LORA_GATE + 2 * LORA_DECAY + 2 * LORA_ICLR
LORA_MIX = 2 * LORA_DECAY + 2 * LORA_ICLR
RKV_COLS = 3 * RWKV_WIDTH
D_FF = 5632
GRID_W = 64
ROPE_THETA = 10000.0
ROPE_FREQS = HEAD_DIM // 4
ATTN_SCALE = HEAD_DIM ** -0.5
NORM_EPS = 1e-6
GN_EPS = 64e-5
L2_EPS = 1e-12
DECAY_SCALE = 0.6065306597126334

LANES = 128
SUBLANES = 8
VMEM_LIMIT = 56 * 1024 * 1024

SCAN_CHUNK = 64
SCAN_HEADS = 4
INV_BLOCK = 16

NN = ((1,), (0,))
NT = ((1,), (1,))
TN = ((0,), (0,))


def _dot(a, b, dims=NN):
    return lax.dot_general(a, b, (dims, ((), ())), preferred_element_type=F32)


def _split(x):
    hi = x.astype(BF16)
    lo = (x - hi.astype(F32)).astype(BF16)
    return hi, lo


def _mm3(a, b, dims=NN):
    ah, al = a
    bh, bl = b
    return _dot(ah, bh, dims) + (_dot(ah, bl, dims) + _dot(al, bh, dims))


def _mm3f(a, b, dims=NN):
    return _mm3(_split(a), _split(b), dims)


def _mm_exact_rhs(a, b_exact, dims=NN):
    ah, al = _split(a)
    return _dot(ah, b_exact, dims) + _dot(al, b_exact, dims)


def _sigmoid(z):
    return 1.0 / (1.0 + jnp.exp(-z))


def _norm_mod(x, g, scale, shift):
    ms = jnp.mean(x * x, axis=-1, keepdims=True)
    y = x * lax.rsqrt(ms + NORM_EPS) * g
    return y * (1.0 + scale) + shift


def _head_ones(width):
    r = lax.broadcasted_iota(jnp.int32, (width, width), 0) >> 6
    c = lax.broadcasted_iota(jnp.int32, (width, width), 1) >> 6
    return jnp.where(r == c, 1.0, 0.0).astype(BF16)


def _head_sum(x, ones):
    w = ones.shape[0]
    parts = [_mm_exact_rhs(x[:, c:c + w], ones) for c in range(0, x.shape[1], w)]
    return jnp.concatenate(parts, axis=1)


def _params(sem):
    return pltpu.CompilerParams(dimension_semantics=sem, vmem_limit_bytes=VMEM_LIMIT)


def _mod_kernel(c_ref, w_ref, b_ref, o_ref):
    c = c_ref[...]
    s = c * _sigmoid(c)
    o_ref[...] = _mm3f(s, w_ref[...]) + b_ref[...]


def _modulation(cond, ada_w, ada_b):
    depth, d, n = ada_w.shape
    tn = 512
    rows = cond.shape[0]
    return pl.pallas_call(
        _mod_kernel,
        grid=(depth, n // tn),
        in_specs=[
            pl.BlockSpec((rows, d), lambda l, j: (0, 0)),
            pl.BlockSpec((None, d, tn), lambda l, j: (l, 0, j)),
            pl.BlockSpec((None, 1, tn), lambda l, j: (l, 0, j)),
        ],
        out_specs=pl.BlockSpec((None, rows, tn), lambda l, j: (l, 0, j)),
        out_shape=jax.ShapeDtypeStruct((depth, rows, n), F32),
        compiler_params=_params(("parallel", "parallel")),
        name="modulation",
    )(cond, ada_w, ada_b.reshape(depth, 1, n))


class _Rows:
    def __init__(self, batch, seq, ctx_len, tm):
        self.batch, self.seq, self.ctx_len, self.tm = batch, seq, ctx_len, tm
        self.n_lat = batch * seq
        self.n_ctx = batch * ctx_len
        self.n_all = self.n_lat + self.n_ctx
        assert seq % tm == 0 and self.n_ctx % tm == 0
        self.blocks_per_batch = seq // tm

    def mod_row(self, i):
        return jnp.minimum(i // self.blocks_per_batch, self.batch)


def _mod_spec(rows, which, ngrid):
    d = D_MODEL
    if ngrid == 1:
        return pl.BlockSpec((None, 1, d), lambda i: (rows.mod_row(i), 0, which))
    return pl.BlockSpec((None, 1, d), lambda i, j: (rows.mod_row(i), 0, which))


def _proj_kernel(x_ref, g_ref, sc_ref, sh_ref, w_ref, o_ref, h_ref):
    @pl.when(pl.program_id(1) == 0)
    def _():
        h_ref[...] = _norm_mod(x_ref[...], g_ref[...], sc_ref[...], sh_ref[...]).astype(BF16)

    o_ref[...] = _dot(h_ref[...], w_ref[...]).astype(o_ref.dtype)


def _project(x, g, mod, w, rows, tn):
    n = w.shape[1]
    tm, d = rows.tm, D_MODEL
    return pl.pallas_call(
        _proj_kernel,
        grid=(rows.n_all // tm, n // tn),
        in_specs=[
            pl.BlockSpec((tm, d), lambda i, j: (i, 0)),
            pl.BlockSpec((1, d), lambda i, j: (0, 0)),
            _mod_spec(rows, 1, 2),
            _mod_spec(rows, 0, 2),
            pl.BlockSpec((d, tn), lambda i, j: (0, j)),
        ],
        out_specs=pl.BlockSpec((tm, tn), lambda i, j: (i, j)),
        out_shape=jax.ShapeDtypeStruct((rows.n_all, n), F32),
        scratch_shapes=[pltpu.VMEM((tm, d), BF16)],
        compiler_params=_params(("parallel", "arbitrary")),
        name="rwkv_proj",
    )(x, g, mod, mod, w)


def _attn_proj_kernel(x_ref, g_ref, sc_ref, sh_ref, w_ref, qg_ref, kg_ref, cos_ref, sin_ref,
                      q_ref, k_ref, v_ref):
    h = _norm_mod(x_ref[...], g_ref[...], sc_ref[...], sh_ref[...]).astype(BF16)
    f = _dot(h, w_ref[...])
    cos = cos_ref[...]
    sin = sin_ref[...]

    def head(xh, gain, scale):
        ms = jnp.mean(xh * xh, axis=-1, keepdims=True)
        y = xh * lax.rsqrt(ms + NORM_EPS) * gain
        y = y * cos + pltpu.roll(y, HEAD_DIM // 2, 1) * sin
        return (y * scale).astype(BF16)

    qg = qg_ref[...]
    kg = kg_ref[...]
    for hq in range(N_Q_HEADS):
        c = hq * HEAD_DIM
        q_ref[:, c:c + HEAD_DIM] = head(f[:, c:c + HEAD_DIM], qg, ATTN_SCALE)
    for hk in range(N_KV_HEADS):
        c = hk * HEAD_DIM
        k_ref[:, c:c + HEAD_DIM] = head(f[:, Q_COLS + c:Q_COLS + c + HEAD_DIM], kg, 1.0)
    v_ref[...] = f[:, Q_COLS + KV_COLS:ATTN_COLS].astype(BF16)


def _attn_project(x, g, mod, w, q_gain, k_gain, cos2, sin2, rows):
    tm, d = rows.tm, D_MODEL
    bpb = rows.blocks_per_batch

    def rope_idx(i):
        return (jnp.where(i < rows.n_lat // tm, i % bpb, bpb), 0)

    return pl.pallas_call(
        _attn_proj_kernel,
        grid=(rows.n_all // tm,),
        in_specs=[
            pl.BlockSpec((tm, d), lambda i: (i, 0)),
            pl.BlockSpec((1, d), lambda i: (0, 0)),
            _mod_spec(rows, 1, 1),
            _mod_spec(rows, 0, 1),
            pl.BlockSpec((d, ATTN_COLS), lambda i: (0, 0)),
            pl.BlockSpec((1, HEAD_DIM), lambda i: (0, 0)),
            pl.BlockSpec((1, HEAD_DIM), lambda i: (0, 0)),
            pl.BlockSpec((tm, HEAD_DIM), rope_idx),
            pl.BlockSpec((tm, HEAD_DIM), rope_idx),
        ],
        out_specs=[
            pl.BlockSpec((tm, Q_COLS), lambda i: (i, 0)),
            pl.BlockSpec((tm, KV_COLS), lambda i: (i, 0)),
            pl.BlockSpec((tm, KV_COLS), lambda i: (i, 0)),
        ],
        out_shape=[
            jax.ShapeDtypeStruct((rows.n_all, Q_COLS), BF16),
            jax.ShapeDtypeStruct((rows.n_all, KV_COLS), BF16),
            jax.ShapeDtypeStruct((rows.n_all, KV_COLS), BF16),
        ],
        compiler_params=_params(("parallel",)),
        name="attn_proj",
    )(x, g, mod, mod, w, q_gain, k_gain, cos2, sin2)


def _attn_kernel(q_ref, *refs, n_kv, kv_chunk):
    k_refs = refs[:n_kv]
    v_refs = refs[n_kv:2 * n_kv]
    o_ref = refs[2 * n_kv]
    tq = q_ref.shape[0]
    for h in range(GQA_GROUP):
        c = h * HEAD_DIM
        q = q_ref[:, c:c + HEAD_DIM]
        m = jnp.full((tq, 1), -jnp.inf, F32)
        l = jnp.zeros((tq, 1), F32)
        acc = jnp.zeros((tq, HEAD_DIM), F32)
        for k_ref, v_ref in zip(k_refs, v_refs):
            n = k_ref.shape[0]
            ck = min(kv_chunk, n)
            for c0 in range(0, n, ck):
                s = _dot(q, k_ref[c0:c0 + ck, :], NT)
                m_new = jnp.maximum(m, jnp.max(s, axis=-1, keepdims=True))
                alpha = jnp.exp(m - m_new)
                p = jnp.exp(s - m_new)
                l = alpha * l + jnp.sum(p, axis=-1, keepdims=True)
                acc = alpha * acc + _dot(p.astype(BF16), v_ref[c0:c0 + ck, :])
                m = m_new
        o_ref[:, c:c + HEAD_DIM] = (acc / l).astype(o_ref.dtype)


def _attention(q, k, v, rows, latent):
    b, s, cl = rows.batch, rows.seq, rows.ctx_len
    gw = GQA_GROUP * HEAD_DIM
    ctx_blk0 = rows.n_lat // cl
    if latent:
        tq = min(256, s)
        nq = s // tq
        q_spec = pl.BlockSpec((tq, gw), lambda bi, hi, qi: (bi * nq + qi, hi))
        kv_specs = [pl.BlockSpec((s, HEAD_DIM), lambda bi, hi, qi: (bi, hi)),
                    pl.BlockSpec((cl, HEAD_DIM), lambda bi, hi, qi: (ctx_blk0 + bi, hi))]
        n_rows = rows.n_lat
        name = "attention_latent"
    else:
        tq = cl
        nq = 1
        q_spec = pl.BlockSpec((tq, gw), lambda bi, hi, qi: (ctx_blk0 + bi, hi))
        kv_specs = [pl.BlockSpec((cl, HEAD_DIM), lambda bi, hi, qi: (ctx_blk0 + bi, hi))]
        n_rows = rows.n_ctx
        name = "attention_context"
    n_kv = len(kv_specs)
    return pl.pallas_call(
        functools.partial(_attn_kernel, n_kv=n_kv, kv_chunk=1024),
        grid=(b, N_KV_HEADS, nq),
        in_specs=[q_spec] + kv_specs + kv_specs,
        out_specs=pl.BlockSpec((tq, gw), lambda bi, hi, qi: (bi * nq + qi, hi)),
        out_shape=jax.ShapeDtypeStruct((n_rows, Q_COLS), BF16),
        compiler_params=_params(("parallel", "parallel", "arbitrary")),
        name=name,
    )(q, *([k] * n_kv), *([v] * n_kv))


def _rwkv_prep_kernel(f_ref, l_ref, fp_ref, lp_ref, fn_ref, ln_ref,
                      mupf_ref, munf_ref, mupl_ref, munl_ref,
                      g2_ref, wmix_ref, w0_ref, a0_ref, kk_scale_ref, ka_ref, rk_ref,
                      r_o, kk_o, v_o, g_o, bonus_o, lw_o, kd_o, bd_o,
                      *, n_lat_blocks, bps_lat, bps_ctx):
    i = pl.program_id(0)
    tm = f_ref.shape[0]
    is_lat = i < n_lat_blocks
    j = jnp.where(is_lat, i, i - n_lat_blocks)
    bps = jnp.where(is_lat, bps_lat, bps_ctx)
    pos = j % bps
    first = pos == 0
    last = pos == bps - 1
    row = lax.broadcasted_iota(jnp.int32, (tm, 1), 0)

    def shifted(x, prev_row, next_row, mu_prev, mu_next):
        prev_row = jnp.where(first, 0.0, prev_row)
        next_row = jnp.where(last, 0.0, next_row)
        prev = jnp.where(row == 0, prev_row, pltpu.roll(x, 1, 0))
        nxt = jnp.where(row == tm - 1, next_row, pltpu.roll(x, tm - 1, 0))
        return x + mu_prev * (prev - x) + mu_next * (nxt - x)

    w = RWKV_WIDTH

    def panel(p):
        c = slice(p * w, (p + 1) * w)
        return shifted(f_ref[:, c], fp_ref[SUBLANES - 1:SUBLANES, c], fn_ref[0:1, c],
                       mupf_ref[:, c], munf_ref[:, c])

    r = panel(0)
    k = panel(1)
    v = panel(2)
    lora = shifted(l_ref[...], lp_ref[SUBLANES - 1:SUBLANES, :], ln_ref[0:1, :],
                   mupl_ref[...], munl_ref[...])

    gate_in = _sigmoid(lora[:, :LORA_GATE])
    mix = lora[:, LORA_GATE:]
    lane = lax.broadcasted_iota(jnp.int32, (1, LORA_MIX), 1)
    mix_in = jnp.where(lane < 2 * LORA_DECAY, jnp.tanh(mix), mix)
    g = _mm3f(gate_in, g2_ref[...])
    wa = _mm3f(mix_in, wmix_ref[...])

    ones = _head_ones(2 * LANES)
    kk = k * kk_scale_ref[...]
    kk = kk * lax.rsqrt(_head_sum(kk * kk, ones) + L2_EPS)

    r_o[...] = r
    kk_o[...] = kk
    v_o[...] = v
    g_o[...] = g
    rk = rk_ref[...]
    ka = ka_ref[...]
    bonus = jnp.zeros_like(r)
    for d in range(2):
        z = w0_ref[d:d + 1, :] + wa[:, d * w:(d + 1) * w]
        lw_o[d] = -DECAY_SCALE * _sigmoid(z)
        a = _sigmoid(a0_ref[d:d + 1, :] + wa[:, (2 + d) * w:(3 + d) * w])
        kd = k * (1.0 + (a - 1.0) * ka)
        kd_o[d] = kd
        bd_o[d] = kk * a
        bonus = bonus + _head_sum(r * kd * rk, ones) * v
    bonus_o[...] = bonus


def _rwkv_prepare(f_rkv, f_lora, p, rows):
    tm = min(256, rows.ctx_len)
    n_all = rows.n_all
    nblk = n_all // tm
    sub_per_blk = tm // SUBLANES
    n_sub = n_all // SUBLANES
    w = RWKV_WIDTH

    def prev_idx(i):
        return (jnp.maximum(i * sub_per_blk - 1, 0), 0)

    def next_idx(i):
        return (jnp.minimum((i + 1) * sub_per_blk, n_sub - 1), 0)

    full = lambda shape: pl.BlockSpec(shape, lambda i: (0,) * len(shape))
    row_spec = lambda width: pl.BlockSpec((tm, width), lambda i: (i, 0))
    dir_spec = pl.BlockSpec((2, tm, w), lambda i: (0, i, 0))
    kern = functools.partial(_rwkv_prep_kernel, n_lat_blocks=rows.n_lat // tm,
                             bps_lat=rows.seq // tm, bps_ctx=rows.ctx_len // tm)
    sds = jax.ShapeDtypeStruct
    return pl.pallas_call(
        kern,
        grid=(nblk,),
        in_specs=[
            row_spec(RKV_COLS), row_spec(LORA_COLS),
            pl.BlockSpec((SUBLANES, RKV_COLS), prev_idx), pl.BlockSpec((SUBLANES, LORA_COLS), prev_idx),
            pl.BlockSpec((SUBLANES, RKV_COLS), next_idx), pl.BlockSpec((SUBLANES, LORA_COLS), next_idx),
            full((1, RKV_COLS)), full((1, RKV_COLS)), full((1, LORA_COLS)), full((1, LORA_COLS)),
            full((LORA_GATE, w)), full((LORA_MIX, 4 * w)), full((2, w)), full((2, w)),
            full((1, w)), full((1, w)), full((1, w)),
        ],
        out_specs=[row_spec(w)] * 5 + [dir_spec] * 3,
        out_shape=[sds((n_all, w), F32)] * 5 + [sds((2, n_all, w), F32)] * 3,
        compiler_params=_params(("parallel",)),
        name="rwkv_prepare",
    )(f_rkv, f_lora, f_rkv, f_lora, f_rkv, f_lora,
      p['mupf'], p['munf'], p['mupl'], p['munl'],
      p['g2'], p['wmix'], p['w0'], p['a0'], p['k_k'], p['k_a'], p['r_k'])


def _scan_kernel(r_ref, kk_ref, v_ref, lw_ref, kd_ref, bd_ref, y_ref, h_ref, *, chunk, heads):
    c = chunk
    wd = heads * RWKV_HEAD
    rr = c * heads
    log2c = c.bit_length() - 1
    d = pl.program_id(0)
    step = pl.program_id(3)
    sgn = 1 - 2 * d

    @pl.when(step == 0)
    def _():
        h_ref[...] = jnp.zeros_like(h_ref)

    iota = lambda shape, ax: lax.broadcasted_iota(jnp.int32, shape, ax)
    rel_c = (iota((c, c), 0) - iota((c, c), 1)) * sgn
    l_incl = jnp.where(rel_c >= 0, 1.0, 0.0).astype(BF16)
    ri = iota((rr, rr), 0)
    ci = iota((rr, rr), 1)
    rel = ((ri & (c - 1)) - (ci & (c - 1))) * sgn
    rel = jnp.where((ri >> log2c) == (ci >> log2c), rel, -1)
    m_strict = rel > 0
    m_incl = rel >= 0
    log2b = INV_BLOCK.bit_length() - 1
    same_blk = (ri >> log2b) == (ci >> log2b)
    eye = jnp.where(ri == ci, 1.0, 0.0)
    st_mask = jnp.where((iota((rr, wd), 0) >> log2c) == (iota((rr, wd), 1) >> 6), 1.0, 0.0)
    st_mask_b = st_mask.astype(BF16)
    eye_w = iota((wd, wd), 0) == iota((wd, wd), 1)

    def tile(z):
        return jnp.concatenate([z] * heads, axis=0)

    def stack(z):
        return tile(z) * (st_mask_b if z.dtype == BF16 else st_mask)

    def pair(fn, zs):
        return fn(zs[0]), fn(zs[1])

    def unstack(z):
        out = z[0:c]
        for hh in range(1, heads):
            out = out + z[hh * c:(hh + 1) * c]
        return out

    r = r_ref[...]
    kk = kk_ref[...]
    v = v_ref[...]
    lw = lw_ref[...]
    k = kd_ref[...]
    b = bd_ref[...]

    g = _mm_exact_rhs_left(l_incl, lw)
    gx = g - lw
    gt = jnp.sum(lw, axis=0, keepdims=True)
    e_x = jnp.exp(gx)
    e_g = jnp.exp(g)
    e_n = jnp.exp(-g)
    e_c = jnp.exp(gt - g)
    kt = _split(kk * e_x)
    rt_f = r * e_g
    rt = _split(rt_f)
    bh = _split(b * e_n)
    kh = _split(k * e_n)
    kb = _split(k * e_c)
    bb = _split(b * e_c)
    vs = _split(v)

    cat0 = lambda xs: jnp.concatenate(xs, axis=0)
    cat1 = lambda xs: jnp.concatenate(xs, axis=1)
    kt_st = pair(stack, kt)
    rt_st = pair(stack, rt)
    v_st = pair(stack, vs)
    x_op = (cat0([kt_st[0], rt_st[0]]), cat0([kt_st[1], rt_st[1]]))
    y_op = (cat0([tile(bh[0]), tile(kh[0])]), cat0([tile(bh[1]), tile(kh[1])]))
    gm = _mm3(x_op, y_op, NT)
    a_m = jnp.where(m_strict, gm[:rr, :rr], 0.0)
    b_m = jnp.where(m_strict, gm[:rr, rr:], 0.0)
    e_m = jnp.where(m_incl, gm[rr:, :rr], 0.0)
    c_m = jnp.where(m_incl, gm[rr:, rr:], 0.0)

    a_d = jnp.where(same_blk, a_m, 0.0)
    a_o = _split(a_m - a_d)
    x1 = _split(-a_d)
    x2 = _split(_mm3(x1, x1))
    x4 = _split(_mm3(x2, x2))
    x8 = _split(_mm3(x4, x4))
    dm = eye - a_d
    dm = dm + _mm3(_split(dm), x2)
    dm = dm + _mm3(_split(dm), x4)
    dm = dm + _mm3(_split(dm), x8)
    dm_s = _split(dm)
    n1 = _mm3(dm_s, a_o)
    n1_s = _split(n1)
    n2 = _split(_mm3(n1_s, n1_s))
    t1 = eye - n1
    t1 = t1 + _mm3(_split(t1), n2)
    t_m = _split(_mm3(_split(t1), dm_s))

    bc = _split(cat0([b_m, c_m]))
    bcv = _mm3(bc, v_st)
    bv = _split(bcv[:rr])
    cv = bcv[rr:]
    wu = _mm3(t_m, (cat1([kt_st[0], bv[0]]), cat1([kt_st[1], bv[1]])))
    wu_s = _split(wu)
    ew = _mm3(_split(e_m), wu_s)
    rhat = rt_f - unstack(ew[:, :wd])
    oloc = unstack(cv - ew[:, wd:])
    p1 = _mm3(pair(stack, bb), wu_s, TN)
    p2 = _mm3(pair(stack, kb), v_st, TN)
    m_bd = jnp.where(eye_w, jnp.exp(gt), 0.0) - p1[:, :wd]
    n_bd = p2 - p1[:, wd:]

    res = _mm3(_split(cat0([m_bd, rhat])), _split(h_ref[...]))
    h_ref[...] = res[:wd] + n_bd
    y_ref[...] = res[wd:] + oloc


def _mm_exact_rhs_left(a_exact, b):
    bh = b.astype(BF16)
    r1 = b - bh.astype(F32)
    bm = r1.astype(BF16)
    bl = (r1 - bm.astype(F32)).astype(BF16)
    return _dot(a_exact, bh) + (_dot(a_exact, bm) + _dot(a_exact, bl))


def _rwkv_scan(r, kk, v, lw, kd, bd, rows):
    c, heads = SCAN_CHUNK, SCAN_HEADS
    wd = heads * RWKV_HEAD
    nq = RWKV_WIDTH // wd
    nc_ctx = rows.ctx_len // c
    nc_lat = rows.seq // c
    lat_blocks = rows.n_lat // c

    def chunk_idx(d, bi, st):
        in_ctx = st < nc_ctx
        t_ctx = jnp.where(d == 0, st, nc_ctx - 1 - st)
        sl = st - nc_ctx
        t_lat = jnp.where(d == 0, sl, nc_lat - 1 - sl)
        return jnp.where(in_ctx, lat_blocks + bi * nc_ctx + t_ctx, bi * nc_lat + t_lat)

    shared = pl.BlockSpec((c, wd), lambda d, bi, q, st: (chunk_idx(d, bi, st), q))
    per_dir = pl.BlockSpec((None, c, wd), lambda d, bi, q, st: (d, chunk_idx(d, bi, st), q))
    return pl.pallas_call(
        functools.partial(_scan_kernel, chunk=c, heads=heads),
        grid=(2, rows.batch, nq, nc_ctx + nc_lat),
        in_specs=[shared, shared, shared, per_dir, per_dir, per_dir],
        out_specs=per_dir,
        out_shape=jax.ShapeDtypeStruct((2, rows.n_all, RWKV_WIDTH), F32),
        scratch_shapes=[pltpu.VMEM((wd, wd), F32)],
        compiler_params=_params(("parallel", "parallel", "parallel", "arbitrary")),
        name="rwkv_scan",
    )(r, kk, v, lw, kd, bd)


def _out_kernel(attn_ref, y_ref, bonus_ref, g_ref, lnw_ref, lnb_ref, x_ref, gate_ref,
                wa_ref, wr_ref, o_ref, rw_ref):
    @pl.when(pl.program_id(1) == 0)
    def _():
        ones = _head_ones(2 * LANES)
        y = y_ref[0] + y_ref[1]
        mean = _head_sum(y, ones) * (1.0 / RWKV_HEAD)
        yc = y - mean
        var = _head_sum(yc * yc, ones) * (1.0 / RWKV_HEAD)
        yn = yc * lax.rsqrt(var + GN_EPS) * lnw_ref[...] + lnb_ref[...]
        rw_ref[...] = ((yn + bonus_ref[...]) * g_ref[...]).astype(BF16)

    out = _dot(attn_ref[...], wa_ref[...]) + _dot(rw_ref[...], wr_ref[...])
    o_ref[...] = x_ref[...] + gate_ref[...] * out


def _mix_output(attn, y, bonus, g, ln_w, ln_b, x, mod, w_attn, w_rwkv, rows, n_rows):
    tm, d = rows.tm, D_MODEL
    tn = 1024
    w = RWKV_WIDTH
    return pl.pallas_call(
        _out_kernel,
        grid=(n_rows // tm, d // tn),
        in_specs=[
            pl.BlockSpec((tm, Q_COLS), lambda i, j: (i, 0)),
            pl.BlockSpec((2, tm, w), lambda i, j: (0, i, 0)),
            pl.BlockSpec((tm, w), lambda i, j: (i, 0)),
            pl.BlockSpec((tm, w), lambda i, j: (i, 0)),
            pl.BlockSpec((1, w), lambda i, j: (0, 0)),
            pl.BlockSpec((1, w), lambda i, j: (0, 0)),
            pl.BlockSpec((tm, tn), lambda i, j: (i, j)),
            pl.BlockSpec((None, 1, tn), lambda i, j: (rows.mod_row(i), 0, 2 * (d // tn) + j)),
            pl.BlockSpec((Q_COLS, tn), lambda i, j: (0, j)),
            pl.BlockSpec((w, tn), lambda i, j: (0, j)),
        ],
        out_specs=pl.BlockSpec((tm, tn), lambda i, j: (i, j)),
        out_shape=jax.ShapeDtypeStruct((n_rows, d), F32),
        scratch_shapes=[pltpu.VMEM((tm, w), BF16)],
        compiler_params=_params(("parallel", "arbitrary")),
        name="mix_output",
    )(attn, y, bonus, g, ln_w, ln_b, x, mod, w_attn, w_rwkv)


def _ffn_kernel(x_ref, g_ref, sc_ref, sh_ref, gate_ref, w1_ref, w3_ref, w2_ref, fg_ref,
                o_ref, h_ref, acc_ref, *, final):
    j = pl.program_id(1)

    @pl.when(j == 0)
    def _():
        h_ref[...] = _norm_mod(x_ref[...], g_ref[...], sc_ref[...], sh_ref[...]).astype(BF16)
        acc_ref[...] = jnp.zeros_like(acc_ref)

    h = h_ref[...]
    u = _dot(h, w1_ref[...])
    a = (u * _sigmoid(u)) * _dot(h, w3_ref[...])
    acc_ref[...] += _dot(a.astype(BF16), w2_ref[...])

    @pl.when(j == pl.num_programs(1) - 1)
    def _():
        y = x_ref[...] + gate_ref[...] * acc_ref[...]
        if final:
            ms = jnp.mean(y * y, axis=-1, keepdims=True)
            y = y * lax.rsqrt(ms + NORM_EPS) * fg_ref[...]
        o_ref[...] = y


def _ffn(x, g, mod, w1, w3, w2, final_g, rows, n_rows, final):
    tm, d = rows.tm, D_MODEL
    tf = 512
    return pl.pallas_call(
        functools.partial(_ffn_kernel, final=final),
        grid=(n_rows // tm, D_FF // tf),
        in_specs=[
            pl.BlockSpec((tm, d), lambda i, j: (i, 0)),
            pl.BlockSpec((1, d), lambda i, j: (0, 0)),
            _mod_spec(rows, 4, 2),
            _mod_spec(rows, 3, 2),
            _mod_spec(rows, 5, 2),
            pl.BlockSpec((d, tf), lambda i, j: (0, j)),
            pl.BlockSpec((d, tf), lambda i, j: (0, j)),
            pl.BlockSpec((tf, d), lambda i, j: (j, 0)),
            pl.BlockSpec((1, d), lambda i, j: (0, 0)),
        ],
        out_specs=pl.BlockSpec((tm, d), lambda i, j: (i, 0)),
        out_shape=jax.ShapeDtypeStruct((n_rows, d), F32),
        scratch_shapes=[pltpu.VMEM((tm, d), BF16), pltpu.VMEM((tm, d), F32)],
        compiler_params=_params(("parallel", "arbitrary")),
        name="ffn_final" if final else "ffn",
    )(x, g, mod, mod, mod, w1, w3, w2, final_g)


def _rope_tables(seq, tm):
    n_rows = seq // GRID_W
    row = jnp.broadcast_to(jnp.arange(n_rows)[:, None], (n_rows, GRID_W)).reshape(-1)
    col = jnp.broadcast_to(jnp.arange(GRID_W)[None, :], (n_rows, GRID_W)).reshape(-1)
    inv = ROPE_THETA ** (-jnp.arange(ROPE_FREQS, dtype=F32) / ROPE_FREQS)
    ang = jnp.concatenate([row[:, None].astype(F32) * inv, col[:, None].astype(F32) * inv], axis=-1)
    cos, sin = jnp.cos(ang), jnp.sin(ang)
    cos2 = jnp.concatenate([cos, cos], axis=-1)
    sin2 = jnp.concatenate([-sin, sin], axis=-1)
    cos2 = jnp.concatenate([cos2, jnp.ones((tm, HEAD_DIM), F32)], axis=0)
    sin2 = jnp.concatenate([sin2, jnp.zeros((tm, HEAD_DIM), F32)], axis=0)
    return cos2, sin2


def _mix_weights(w2, a2):
    w = RWKV_WIDTH
    out = jnp.zeros((LORA_MIX, 4 * w), F32)
    out = out.at[0:LORA_DECAY, 0:w].set(w2[0])
    out = out.at[LORA_DECAY:2 * LORA_DECAY, w:2 * w].set(w2[1])
    o = 2 * LORA_DECAY
    out = out.at[o:o + LORA_ICLR, 2 * w:3 * w].set(a2[0])
    out = out.at[o + LORA_ICLR:o + 2 * LORA_ICLR, 3 * w:4 * w].set(a2[1])
    return out


def kernel(x, c, ctx, c_ctx, norm1_g, norm2_g, ada_w, ada_b, w_in, q_gain, k_gain, mu_prev, mu_next,
           w0, w2, a0, a2, g2, k_k, k_a, r_k, ln_x_w, ln_x_b, w_out, ffn_w1, ffn_w3, ffn_w2, final_g):
    batch, seq, d = x.shape
    ctx_len = ctx.shape[1]
    depth = w_in.shape[0]
    rows = _Rows(batch, seq, ctx_len, min(512, batch * ctx_len))
    w = RWKV_WIDTH

    cond = jnp.concatenate([c, c_ctx[None, :], jnp.zeros((SUBLANES - batch - 1, d), F32)], axis=0)
    mod_all = _modulation(cond, ada_w, ada_b)
    mod_all = mod_all.reshape(depth, SUBLANES, 1, 6 * d)
    cos2, sin2 = _rope_tables(seq, rows.tm)

    w_in_b = w_in.astype(BF16)
    w_out_b = w_out.astype(BF16)
    w1_b = ffn_w1.astype(BF16)
    w3_b = ffn_w3.astype(BF16)
    w2_b = ffn_w2.astype(BF16)

    tok = jnp.concatenate([x.reshape(batch * seq, d), ctx.reshape(batch * ctx_len, d)], axis=0)
    row1 = lambda a: a.reshape(1, -1)
    for i in range(depth):
        last = i == depth - 1
        mod = mod_all[i]
        g1 = row1(norm1_g[i])
        q, k, v = _attn_project(tok, g1, mod, w_in_b[i, :, :ATTN_COLS], row1(q_gain[i]), row1(k_gain[i]),
                                cos2, sin2, rows)
        f_rkv = _project(tok, g1, mod, w_in_b[i, :, ATTN_COLS:ATTN_COLS + RKV_COLS], rows, 1024)
        f_lora = _project(tok, g1, mod, w_in_b[i, :, ATTN_COLS + RKV_COLS:], rows, LORA_COLS)
        attn = _attention(q, k, v, rows, latent=True)
        if not last:
            attn = jnp.concatenate([attn, _attention(q, k, v, rows, latent=False)], axis=0)
        o_rkv = ATTN_COLS
        prep = {
            'mupf': row1(mu_prev[i, :RKV_COLS]), 'munf': row1(mu_next[i, :RKV_COLS]),
            'mupl': row1(mu_prev[i, RKV_COLS:]), 'munl': row1(mu_next[i, RKV_COLS:]),
            'g2': g2[i], 'wmix': _mix_weights(w2[i], a2[i]), 'w0': w0[i], 'a0': a0[i],
            'k_k': row1(k_k[i]), 'k_a': row1(k_a[i]), 'r_k': row1(r_k[i]),
        }
        r_s, kk_s, v_s, g_s, bonus, lw, kd, bd = _rwkv_prepare(f_rkv, f_lora, prep, rows)
        y = _rwkv_scan(r_s, kk_s, v_s, lw, kd, bd, rows)
        n_rows = rows.n_lat if last else rows.n_all
        tok = _mix_output(attn, y, bonus, g_s, row1(ln_x_w[i]), row1(ln_x_b[i]), tok, mod,
                          w_out_b[i, :Q_COLS], w_out_b[i, Q_COLS:], rows, n_rows)
        tok = _ffn(tok, row1(norm2_g[i]), mod, w1_b[i], w3_b[i], w2_b[i], row1(final_g),
                   rows, n_rows, final=last)
    return tok.reshape(batch, seq, d)
```

```python
import functools

import jax
import jax.numpy as jnp
from jax import lax
from jax.experimental import pallas as pl
from jax.experimental.pallas import tpu as pltpu

F32 = jnp.float32
BF16 = jnp.bfloat16

D_MODEL = 2048
HEAD_DIM = 128
N_Q_HEADS = 8
N_KV_HEADS = 2
GQA_GROUP = N_Q_HEADS // N_KV_HEADS
Q_COLS = N_Q_HEADS * HEAD_DIM
KV_COLS = N_KV_HEADS * HEAD_DIM
ATTN_COLS = Q_COLS + 2 * KV_COLS
RWKV_WIDTH = 1024
RWKV_HEAD = 64
LORA_GATE = 256
LORA_DECAY = 96
LORA_ICLR = 96
LORA_COLS = LORA_GATE + 2 * LORA_DECAY + 2 * LORA_ICLR
LORA_MIX = 2 * LORA_DECAY + 2 * LORA_ICLR
RKV_COLS = 3 * RWKV_WIDTH
D_FF = 5632
GRID_W = 64
ROPE_THETA = 10000.0
ROPE_FREQS = HEAD_DIM // 4
ATTN_SCALE = HEAD_DIM ** -0.5
NORM_EPS = 1e-6
GN_EPS = 64e-5
L2_EPS = 1e-12
DECAY_SCALE = 0.6065306597126334

LANES = 128
SUBLANES = 8
VMEM_LIMIT = 56 * 1024 * 1024

SCAN_CHUNK = 64
SCAN_HEADS = 4
INV_BLOCK = 16
SCAN_GROUPS = 4

NN = ((1,), (0,))
NT = ((1,), (1,))
TN = ((0,), (0,))


def _dot(a, b, dims=NN):
    return lax.dot_general(a, b, (dims, ((), ())), preferred_element_type=F32)


def _split(x):
    hi = x.astype(BF16)
    lo = (x - hi.astype(F32)).astype(BF16)
    return hi, lo


def _mm3(a, b, dims=NN):
    ah, al = a
    bh, bl = b
    return _dot(ah, bh, dims) + (_dot(ah, bl, dims) + _dot(al, bh, dims))


def _mm3f(a, b, dims=NN):
    return _mm3(_split(a), _split(b), dims)


def _mm_exact_rhs(a, b_exact, dims=NN):
    ah, al = _split(a)
    return _dot(ah, b_exact, dims) + _dot(al, b_exact, dims)


def _sigmoid(z):
    return 1.0 / (1.0 + jnp.exp(-z))


def _norm_mod(x, g, scale, shift):
    ms = jnp.mean(x * x, axis=-1, keepdims=True)
    y = x * lax.rsqrt(ms + NORM_EPS) * g
    return y * (1.0 + scale) + shift


def _head_ones(width):
    r = lax.broadcasted_iota(jnp.int32, (width, width), 0) >> 6
    c = lax.broadcasted_iota(jnp.int32, (width, width), 1) >> 6
    return jnp.where(r == c, 1.0, 0.0).astype(BF16)


def _head_sum(x, ones):
    w = ones.shape[0]
    parts = [_mm_exact_rhs(x[:, c:c + w], ones) for c in range(0, x.shape[1], w)]
    return jnp.concatenate(parts, axis=1)


def _params(sem):
    return pltpu.CompilerParams(dimension_semantics=sem, vmem_limit_bytes=VMEM_LIMIT)


def _mod_kernel(c_ref, w_ref, b_ref, o_ref):
    c = c_ref[...]
    s = c * _sigmoid(c)
    o_ref[...] = _mm3f(s, w_ref[...]) + b_ref[...]


def _modulation(cond, ada_w, ada_b):
    depth, d, n = ada_w.shape
    tn = 512
    rows = cond.shape[0]
    return pl.pallas_call(
        _mod_kernel,
        grid=(depth, n // tn),
        in_specs=[
            pl.BlockSpec((rows, d), lambda l, j: (0, 0)),
            pl.BlockSpec((None, d, tn), lambda l, j: (l, 0, j)),
            pl.BlockSpec((None, 1, tn), lambda l, j: (l, 0, j)),
        ],
        out_specs=pl.BlockSpec((None, rows, tn), lambda l, j: (l, 0, j)),
        out_shape=jax.ShapeDtypeStruct((depth, rows, n), F32),
        compiler_params=_params(("parallel", "parallel")),
        name="modulation",
    )(cond, ada_w, ada_b.reshape(depth, 1, n))


class _Rows:
    def __init__(self, batch, seq, ctx_len, tm):
        self.batch, self.seq, self.ctx_len, self.tm = batch, seq, ctx_len, tm
        self.n_lat = batch * seq
        self.n_ctx = batch * ctx_len
        self.n_all = self.n_lat + self.n_ctx
        assert seq % tm == 0 and self.n_ctx % tm == 0
        self.blocks_per_batch = seq // tm

    def mod_row(self, i):
        return jnp.minimum(i // self.blocks_per_batch, self.batch)


def _mod_spec(rows, which, ngrid):
    d = D_MODEL
    if ngrid == 1:
        return pl.BlockSpec((None, 1, d), lambda i: (rows.mod_row(i), 0, which))
    return pl.BlockSpec((None, 1, d), lambda i, j: (rows.mod_row(i), 0, which))


def _proj_kernel(x_ref, g_ref, sc_ref, sh_ref, w_ref, o_ref, h_ref):
    @pl.when(pl.program_id(1) == 0)
    def _():
        h_ref[...] = _norm_mod(x_ref[...], g_ref[...], sc_ref[...], sh_ref[...]).astype(BF16)

    o_ref[...] = _dot(h_ref[...], w_ref[...]).astype(o_ref.dtype)


def _project(x, g, mod, w, rows, tn):
    n = w.shape[1]
    tm, d = rows.tm, D_MODEL
    return pl.pallas_call(
        _proj_kernel,
        grid=(rows.n_all // tm, n // tn),
        in_specs=[
            pl.BlockSpec((tm, d), lambda i, j: (i, 0)),
            pl.BlockSpec((1, d), lambda i, j: (0, 0)),
            _mod_spec(rows, 1, 2),
            _mod_spec(rows, 0, 2),
            pl.BlockSpec((d, tn), lambda i, j: (0, j)),
        ],
        out_specs=pl.BlockSpec((tm, tn), lambda i, j: (i, j)),
        out_shape=jax.ShapeDtypeStruct((rows.n_all, n), F32),
        scratch_shapes=[pltpu.VMEM((tm, d), BF16)],
        compiler_params=_params(("parallel", "arbitrary")),
        name="rwkv_proj",
    )(x, g, mod, mod, w)


def _attn_proj_kernel(x_ref, g_ref, sc_ref, sh_ref, w_ref, qg_ref, kg_ref, cos_ref, sin_ref,
                      q_ref, k_ref, v_ref):
    h = _norm_mod(x_ref[...], g_ref[...], sc_ref[...], sh_ref[...]).astype(BF16)
    f = _dot(h, w_ref[...])
    cos = cos_ref[...]
    sin = sin_ref[...]

    def head(xh, gain, scale):
        ms = jnp.mean(xh * xh, axis=-1, keepdims=True)
        y = xh * lax.rsqrt(ms + NORM_EPS) * gain
        y = y * cos + pltpu.roll(y, HEAD_DIM // 2, 1) * sin
        return (y * scale).astype(BF16)

    qg = qg_ref[...]
    kg = kg_ref[...]
    for hq in range(N_Q_HEADS):
        c = hq * HEAD_DIM
        q_ref[:, c:c + HEAD_DIM] = head(f[:, c:c + HEAD_DIM], qg, ATTN_SCALE)
    for hk in range(N_KV_HEADS):
        c = hk * HEAD_DIM
        k_ref[:, c:c + HEAD_DIM] = head(f[:, Q_COLS + c:Q_COLS + c + HEAD_DIM], kg, 1.0)
    v_ref[...] = f[:, Q_COLS + KV_COLS:ATTN_COLS].astype(BF16)


def _attn_project(x, g, mod, w, q_gain, k_gain, cos2, sin2, rows):
    tm, d = rows.tm, D_MODEL
    bpb = rows.blocks_per_batch

    def rope_idx(i):
        return (jnp.where(i < rows.n_lat // tm, i % bpb, bpb), 0)

    return pl.pallas_call(
        _attn_proj_kernel,
        grid=(rows.n_all // tm,),
        in_specs=[
            pl.BlockSpec((tm, d), lambda i: (i, 0)),
            pl.BlockSpec((1, d), lambda i: (0, 0)),
            _mod_spec(rows, 1, 1),
            _mod_spec(rows, 0, 1),
            pl.BlockSpec((d, ATTN_COLS), lambda i: (0, 0)),
            pl.BlockSpec((1, HEAD_DIM), lambda i: (0, 0)),
            pl.BlockSpec((1, HEAD_DIM), lambda i: (0, 0)),
            pl.BlockSpec((tm, HEAD_DIM), rope_idx),
            pl.BlockSpec((tm, HEAD_DIM), rope_idx),
        ],
        out_specs=[
            pl.BlockSpec((tm, Q_COLS), lambda i: (i, 0)),
            pl.BlockSpec((tm, KV_COLS), lambda i: (i, 0)),
            pl.BlockSpec((tm, KV_COLS), lambda i: (i, 0)),
        ],
        out_shape=[
            jax.ShapeDtypeStruct((rows.n_all, Q_COLS), BF16),
            jax.ShapeDtypeStruct((rows.n_all, KV_COLS), BF16),
            jax.ShapeDtypeStruct((rows.n_all, KV_COLS), BF16),
        ],
        compiler_params=_params(("parallel",)),
        name="attn_proj",
    )(x, g, mod, mod, w, q_gain, k_gain, cos2, sin2)


def _attn_kernel(q_ref, *refs, n_kv, kv_chunk):
    k_refs = refs[:n_kv]
    v_refs = refs[n_kv:2 * n_kv]
    o_ref = refs[2 * n_kv]
    tq = q_ref.shape[0]
    for h in range(GQA_GROUP):
        c = h * HEAD_DIM
        q = q_ref[:, c:c + HEAD_DIM]
        m = jnp.full((tq, 1), -jnp.inf, F32)
        l = jnp.zeros((tq, 1), F32)
        acc = jnp.zeros((tq, HEAD_DIM), F32)
        for k_ref, v_ref in zip(k_refs, v_refs):
            n = k_ref.shape[0]
            ck = min(kv_chunk, n)
            for c0 in range(0, n, ck):
                s = _dot(q, k_ref[c0:c0 + ck, :], NT)
                m_new = jnp.maximum(m, jnp.max(s, axis=-1, keepdims=True))
                alpha = jnp.exp(m - m_new)
                p = jnp.exp(s - m_new)
                l = alpha * l + jnp.sum(p, axis=-1, keepdims=True)
                acc = alpha * acc + _dot(p.astype(BF16), v_ref[c0:c0 + ck, :])
                m = m_new
        o_ref[:, c:c + HEAD_DIM] = (acc / l).astype(o_ref.dtype)


def _attention(q, k, v, rows, latent):
    b, s, cl = rows.batch, rows.seq, rows.ctx_len
    gw = GQA_GROUP * HEAD_DIM
    ctx_blk0 = rows.n_lat // cl
    if latent:
        tq = min(256, s)
        nq = s // tq
        q_spec = pl.BlockSpec((tq, gw), lambda bi, hi, qi: (bi * nq + qi, hi))
        kv_specs = [pl.BlockSpec((s, HEAD_DIM), lambda bi, hi, qi: (bi, hi)),
                    pl.BlockSpec((cl, HEAD_DIM), lambda bi, hi, qi: (ctx_blk0 + bi, hi))]
        n_rows = rows.n_lat
        name = "attention_latent"
    else:
        tq = cl
        nq = 1
        q_spec = pl.BlockSpec((tq, gw), lambda bi, hi, qi: (ctx_blk0 + bi, hi))
        kv_specs = [pl.BlockSpec((cl, HEAD_DIM), lambda bi, hi, qi: (ctx_blk0 + bi, hi))]
        n_rows = rows.n_ctx
        name = "attention_context"
    n_kv = len(kv_specs)
    return pl.pallas_call(
        functools.partial(_attn_kernel, n_kv=n_kv, kv_chunk=1024),
        grid=(b, N_KV_HEADS, nq),
        in_specs=[q_spec] + kv_specs + kv_specs,
        out_specs=pl.BlockSpec((tq, gw), lambda bi, hi, qi: (bi * nq + qi, hi)),
        out_shape=jax.ShapeDtypeStruct((n_rows, Q_COLS), BF16),
        compiler_params=_params(("parallel", "parallel", "arbitrary")),
        name=name,
    )(q, *([k] * n_kv), *([v] * n_kv))


def _rwkv_prep_kernel(f_ref, l_ref, fp_ref, lp_ref, fn_ref, ln_ref,
                      mupf_ref, munf_ref, mupl_ref, munl_ref,
                      g2_ref, wmix_ref, w0_ref, a0_ref, kk_scale_ref, ka_ref, rk_ref,
                      r_o, kk_o, v_o, g_o, bonus_o, lw_o, kd_o, bd_o,
                      *, n_lat_blocks, bps_lat, bps_ctx):
    i = pl.program_id(0)
    tm = f_ref.shape[0]
    is_lat = i < n_lat_blocks
    j = jnp.where(is_lat, i, i - n_lat_blocks)
    bps = jnp.where(is_lat, bps_lat, bps_ctx)
    pos = j % bps
    first = pos == 0
    last = pos == bps - 1
    row = lax.broadcasted_iota(jnp.int32, (tm, 1), 0)

    def shifted(x, prev_row, next_row, mu_prev, mu_next):
        prev_row = jnp.where(first, 0.0, prev_row)
        next_row = jnp.where(last, 0.0, next_row)
        prev = jnp.where(row == 0, prev_row, pltpu.roll(x, 1, 0))
        nxt = jnp.where(row == tm - 1, next_row, pltpu.roll(x, tm - 1, 0))
        return x + mu_prev * (prev - x) + mu_next * (nxt - x)

    w = RWKV_WIDTH

    def panel(p):
        c = slice(p * w, (p + 1) * w)
        return shifted(f_ref[:, c], fp_ref[SUBLANES - 1:SUBLANES, c], fn_ref[0:1, c],
                       mupf_ref[:, c], munf_ref[:, c])

    r = panel(0)
    k = panel(1)
    v = panel(2)
    lora = shifted(l_ref[...], lp_ref[SUBLANES - 1:SUBLANES, :], ln_ref[0:1, :],
                   mupl_ref[...], munl_ref[...])

    gate_in = _sigmoid(lora[:, :LORA_GATE])
    mix = lora[:, LORA_GATE:]
    lane = lax.broadcasted_iota(jnp.int32, (1, LORA_MIX), 1)
    mix_in = jnp.where(lane < 2 * LORA_DECAY, jnp.tanh(mix), mix)
    g = _mm3f(gate_in, g2_ref[...])
    wa = _mm3f(mix_in, wmix_ref[...])

    ones = _head_ones(2 * LANES)
    kk = k * kk_scale_ref[...]
    kk = kk * lax.rsqrt(_head_sum(kk * kk, ones) + L2_EPS)

    r_o[...] = r
    kk_o[...] = kk
    v_o[...] = v
    g_o[...] = g
    rk = rk_ref[...]
    ka = ka_ref[...]
    bonus = jnp.zeros_like(r)
    for d in range(2):
        z = w0_ref[d:d + 1, :] + wa[:, d * w:(d + 1) * w]
        lw_o[d] = -DECAY_SCALE * _sigmoid(z)
        a = _sigmoid(a0_ref[d:d + 1, :] + wa[:, (2 + d) * w:(3 + d) * w])
        kd = k * (1.0 + (a - 1.0) * ka)
        kd_o[d] = kd
        bd_o[d] = kk * a
        bonus = bonus + _head_sum(r * kd * rk, ones) * v
    bonus_o[...] = bonus


def _rwkv_prepare(f_rkv, f_lora, p, rows):
    tm = min(256, rows.ctx_len)
    n_all = rows.n_all
    nblk = n_all // tm
    sub_per_blk = tm // SUBLANES
    n_sub = n_all // SUBLANES
    w = RWKV_WIDTH

    def prev_idx(i):
        return (jnp.maximum(i * sub_per_blk - 1, 0), 0)

    def next_idx(i):
        return (jnp.minimum((i + 1) * sub_per_blk, n_sub - 1), 0)

    full = lambda shape: pl.BlockSpec(shape, lambda i: (0,) * len(shape))
    row_spec = lambda width: pl.BlockSpec((tm, width), lambda i: (i, 0))
    dir_spec = pl.BlockSpec((2, tm, w), lambda i: (0, i, 0))
    kern = functools.partial(_rwkv_prep_kernel, n_lat_blocks=rows.n_lat // tm,
                             bps_lat=rows.seq // tm, bps_ctx=rows.ctx_len // tm)
    sds = jax.ShapeDtypeStruct
    return pl.pallas_call(
        kern,
        grid=(nblk,),
        in_specs=[
            row_spec(RKV_COLS), row_spec(LORA_COLS),
            pl.BlockSpec((SUBLANES, RKV_COLS), prev_idx), pl.BlockSpec((SUBLANES, LORA_COLS), prev_idx),
            pl.BlockSpec((SUBLANES, RKV_COLS), next_idx), pl.BlockSpec((SUBLANES, LORA_COLS), next_idx),
            full((1, RKV_COLS)), full((1, RKV_COLS)), full((1, LORA_COLS)), full((1, LORA_COLS)),
            full((LORA_GATE, w)), full((LORA_MIX, 4 * w)), full((2, w)), full((2, w)),
            full((1, w)), full((1, w)), full((1, w)),
        ],
        out_specs=[row_spec(w)] * 5 + [dir_spec] * 3,
        out_shape=[sds((n_all, w), F32)] * 5 + [sds((2, n_all, w), F32)] * 3,
        compiler_params=_params(("parallel",)),
        name="rwkv_prepare",
    )(f_rkv, f_lora, f_rkv, f_lora, f_rkv, f_lora,
      p['mupf'], p['munf'], p['mupl'], p['munl'],
      p['g2'], p['wmix'], p['w0'], p['a0'], p['k_k'], p['k_a'], p['r_k'])


def _scan_kernel(r_ref, kk_ref, v_ref, lw_ref, kd_ref, bd_ref, y_ref, h_ref,
                 lincl_ref, msd_ref, mso_ref, mincl_ref, eye_ref, hm_ref, hmb_ref,
                 *, chunk, heads, groups):
    c = chunk
    wd = heads * RWKV_HEAD
    step = pl.program_id(3)

    @pl.when(step == 0)
    def _():
        h_ref[...] = jnp.zeros_like(h_ref)
        sgn = 1 - 2 * pl.program_id(0)
        iota = lambda shape, ax: lax.broadcasted_iota(jnp.int32, shape, ax)
        rel_c = (iota((c, c), 0) - iota((c, c), 1)) * sgn
        lincl_ref[...] = jnp.where(rel_c >= 0, 1.0, 0.0).astype(BF16)
        ti = iota((c, wd), 0)
        si = iota((c, wd), 1) & (c - 1)
        rel = (ti - si) * sgn
        log2b = INV_BLOCK.bit_length() - 1
        same_blk = (ti >> log2b) == (si >> log2b)
        strict = jnp.where(rel > 0, 1.0, 0.0)
        msd_ref[...] = jnp.where(same_blk, strict, 0.0)
        mso_ref[...] = jnp.where(same_blk, 0.0, strict)
        mincl_ref[...] = jnp.where(rel >= 0, 1.0, 0.0)
        eye_ref[...] = jnp.where(ti == si, 1.0, 0.0)
        hm = jnp.where((iota((wd, wd), 0) >> 6) == (iota((wd, wd), 1) >> 6), 1.0, 0.0)
        hm_ref[...] = hm
        hmb_ref[...] = hm.astype(BF16)

    lanes = [slice(q * wd, (q + 1) * wd) for q in range(groups)]
    ins = [[ref[:, ln] for ln in lanes] for ref in (r_ref, kk_ref, v_ref, lw_ref, kd_ref, bd_ref)]
    h_new, y = _chunk_update(*ins, [h_ref[q] for q in range(groups)],
                             lincl_ref, msd_ref, mso_ref, mincl_ref, eye_ref, hm_ref, hmb_ref, heads)
    for q in range(groups):
        h_ref[q] = h_new[q]
        y_ref[:, lanes[q]] = y[q]


def _chunk_update(r, kk, v, lw, k, b, h, lincl_ref, msd_ref, mso_ref, mincl_ref, eye_ref,
                  hm_ref, hmb_ref, heads):
    c = RWKV_HEAD
    wd = heads * RWKV_HEAD
    hmb = hmb_ref[...]
    each = lambda f, *ls: [f(*a) for a in zip(*ls)]

    def stacked(z):
        zb = z.astype(BF16)
        return jnp.concatenate([zb] * heads, axis=0) * hmb

    def mm(x, y_stacked, dims=NN):
        return _dot(x.astype(BF16), y_stacked, dims)

    def head_blocks(full):
        z = full * hm_ref[...]
        out = z[0:c]
        for hh in range(1, heads):
            out = out + z[hh * c:(hh + 1) * c]
        return out

    cat0 = lambda *xs: jnp.concatenate(xs, axis=0)
    cat1 = lambda *xs: jnp.concatenate(xs, axis=1)
    mul = lambda x, y: x * y
    sub = lambda x, y: x - y
    add = lambda x, y: x + y
    left = lambda z: z[:, :wd]
    right = lambda z: z[:, wd:]
    top = lambda z: z[:c]
    bottom = lambda z: z[c:]
    eye = eye_ref[...]
    lincl = lincl_ref[...]
    g = each(lambda x: _mm_exact_rhs_left(lincl, x), lw)
    gt = each(lambda x: jnp.sum(x, axis=0, keepdims=True), lw)
    e_x = each(lambda gi, lwi: jnp.exp(gi - lwi), g, lw)
    e_g = each(jnp.exp, g)
    e_n = each(lambda gi: jnp.exp(-gi), g)
    e_c = each(lambda gti, gi: jnp.exp(gti - gi), gt, g)
    kt = each(mul, kk, e_x)
    rt = each(mul, r, e_g)
    bh = each(mul, b, e_n)
    kh = each(mul, k, e_n)
    kb = each(mul, k, e_c)
    bb = each(mul, b, e_c)

    gm = each(lambda kti, rti, bhi, khi: mm(cat0(kti, rti), cat0(stacked(bhi), stacked(khi)), NT),
              kt, rt, bh, kh)
    msd = msd_ref[...]
    mso = mso_ref[...]
    m_incl = mincl_ref[...]
    a_d = each(lambda z: z[:c, :wd] * msd, gm)
    a_o = each(lambda z: z[:c, :wd] * mso, gm)
    b_m = each(lambda z: z[:c, wd:] * (msd + mso), gm)
    e_m = each(lambda z: z[c:, :wd] * m_incl, gm)
    c_m = each(lambda z: z[c:, wd:] * m_incl, gm)

    mm_st = lambda x, y: mm(x, stacked(y))
    x1 = each(lambda z: -z, a_d)
    x2 = each(mm_st, x1, x1)
    s2 = each(stacked, x2)
    x4 = each(mm, x2, s2)
    s4 = each(stacked, x4)
    x8 = each(mm, x4, s4)
    dm = each(lambda z: eye - z, a_d)
    dm = each(lambda di, si: di + mm(di, si), dm, s2)
    dm = each(lambda di, si: di + mm(di, si), dm, s4)
    dm = each(lambda di, xi: di + mm_st(di, xi), dm, x8)
    n1 = each(mm_st, dm, a_o)
    n2 = each(mm_st, n1, n1)
    t1 = each(lambda z: eye - z, n1)
    t1 = each(lambda ti, ni: ti + mm_st(ti, ni), t1, n2)
    t_m = each(mm_st, t1, dm)

    bcv = each(lambda bi, ci, vi: mm(cat0(bi, ci), stacked(vi)), b_m, c_m, v)
    bv = each(top, bcv)
    cv = each(bottom, bcv)
    wu = each(lambda ti, kti, bvi: mm(ti, cat1(stacked(kti), stacked(bvi))), t_m, kt, bv)
    wt = each(left, wu)
    ut = each(right, wu)
    ew = each(lambda ei, wi, ui: mm(ei, cat1(stacked(wi), stacked(ui))), e_m, wt, ut)
    rhat = each(lambda ri, z: ri - z[:, :wd], rt, ew)
    oloc = each(lambda ci, z: ci - z[:, wd:], cv, ew)
    full_m = each(lambda bi, wi: _dot(bi.astype(BF16), wi.astype(BF16), TN), bb, wt)
    full_n = each(lambda ki, bi, vi, ui: _dot(cat0(ki, bi).astype(BF16), cat0(vi, -ui).astype(BF16), TN),
                  kb, bb, v, ut)
    m_c = each(lambda gti, fi: eye * jnp.exp(gti) - head_blocks(fi), gt, full_m)
    n_c = each(head_blocks, full_n)

    def advance(mi, ri, hi):
        lhs_hi, lhs_lo = _split(cat0(mi, ri))
        h_hi, h_lo = _split(hi)
        sh_hi = stacked(h_hi)
        return _dot(lhs_hi, sh_hi) + (_dot(lhs_hi, stacked(h_lo)) + _dot(lhs_lo, sh_hi))

    res = each(advance, m_c, rhat, h)
    h_new = each(lambda z, ni: z[:c] + ni, res, n_c)
    y = each(lambda z, oi: z[c:] + oi, res, oloc)
    return h_new, y


def _mm_exact_rhs_left(a_exact, b):
    bh = b.astype(BF16)
    r1 = b - bh.astype(F32)
    bm = r1.astype(BF16)
    bl = (r1 - bm.astype(F32)).astype(BF16)
    return _dot(a_exact, bh) + (_dot(a_exact, bm) + _dot(a_exact, bl))


def _rwkv_scan(r, kk, v, lw, kd, bd, rows):
    c, heads = SCAN_CHUNK, SCAN_HEADS
    wd = heads * RWKV_HEAD
    nq = RWKV_WIDTH // wd
    nc_ctx = rows.ctx_len // c
    nc_lat = rows.seq // c
    lat_blocks = rows.n_lat // c

    def chunk_idx(d, bi, st):
        in_ctx = st < nc_ctx
        t_ctx = jnp.where(d == 0, st, nc_ctx - 1 - st)
        sl = st - nc_ctx
        t_lat = jnp.where(d == 0, sl, nc_lat - 1 - sl)
        return jnp.where(in_ctx, lat_blocks + bi * nc_ctx + t_ctx, bi * nc_lat + t_lat)

    assert c == RWKV_HEAD
    groups = SCAN_GROUPS
    bw = wd * groups
    shared = pl.BlockSpec((c, bw), lambda d, bi, q, st: (chunk_idx(d, bi, st), q))
    per_dir = pl.BlockSpec((None, c, bw), lambda d, bi, q, st: (d, chunk_idx(d, bi, st), q))
    return pl.pallas_call(
        functools.partial(_scan_kernel, chunk=c, heads=heads, groups=groups),
        grid=(2, rows.batch, nq // groups, nc_ctx + nc_lat),
        in_specs=[shared, shared, shared, per_dir, per_dir, per_dir],
        out_specs=per_dir,
        out_shape=jax.ShapeDtypeStruct((2, rows.n_all, RWKV_WIDTH), F32),
        scratch_shapes=[
            pltpu.VMEM((groups, c, wd), F32),
            pltpu.VMEM((c, c), BF16),
            pltpu.VMEM((c, wd), F32),
            pltpu.VMEM((c, wd), F32),
            pltpu.VMEM((c, wd), F32),
            pltpu.VMEM((c, wd), F32),
            pltpu.VMEM((wd, wd), F32),
            pltpu.VMEM((wd, wd), BF16),
        ],
        compiler_params=_params(("parallel", "parallel", "parallel", "arbitrary")),
        name="rwkv_scan",
    )(r, kk, v, lw, kd, bd)


def _out_kernel(attn_ref, y_ref, bonus_ref, g_ref, lnw_ref, lnb_ref, x_ref, gate_ref,
                wa_ref, wr_ref, o_ref, rw_ref):
    @pl.when(pl.program_id(1) == 0)
    def _():
        ones = _head_ones(2 * LANES)
        y = y_ref[0] + y_ref[1]
        mean = _head_sum(y, ones) * (1.0 / RWKV_HEAD)
        yc = y - mean
        var = _head_sum(yc * yc, ones) * (1.0 / RWKV_HEAD)
        yn = yc * lax.rsqrt(var + GN_EPS) * lnw_ref[...] + lnb_ref[...]
        rw_ref[...] = ((yn + bonus_ref[...]) * g_ref[...]).astype(BF16)

    out = _dot(attn_ref[...], wa_ref[...]) + _dot(rw_ref[...], wr_ref[...])
    o_ref[...] = x_ref[...] + gate_ref[...] * out


def _mix_output(attn, y, bonus, g, ln_w, ln_b, x, mod, w_attn, w_rwkv, rows, n_rows):
    tm, d = rows.tm, D_MODEL
    tn = 1024
    w = RWKV_WIDTH
    return pl.pallas_call(
        _out_kernel,
        grid=(n_rows // tm, d // tn),
        in_specs=[
            pl.BlockSpec((tm, Q_COLS), lambda i, j: (i, 0)),
            pl.BlockSpec((2, tm, w), lambda i, j: (0, i, 0)),
            pl.BlockSpec((tm, w), lambda i, j: (i, 0)),
            pl.BlockSpec((tm, w), lambda i, j: (i, 0)),
            pl.BlockSpec((1, w), lambda i, j: (0, 0)),
            pl.BlockSpec((1, w), lambda i, j: (0, 0)),
            pl.BlockSpec((tm, tn), lambda i, j: (i, j)),
            pl.BlockSpec((None, 1, tn), lambda i, j: (rows.mod_row(i), 0, 2 * (d // tn) + j)),
            pl.BlockSpec((Q_COLS, tn), lambda i, j: (0, j)),
            pl.BlockSpec((w, tn), lambda i, j: (0, j)),
        ],
        out_specs=pl.BlockSpec((tm, tn), lambda i, j: (i, j)),
        out_shape=jax.ShapeDtypeStruct((n_rows, d), F32),
        scratch_shapes=[pltpu.VMEM((tm, w), BF16)],
        compiler_params=_params(("parallel", "arbitrary")),
        name="mix_output",
    )(attn, y, bonus, g, ln_w, ln_b, x, mod, w_attn, w_rwkv)


def _ffn_kernel(x_ref, g_ref, sc_ref, sh_ref, gate_ref, w1_ref, w3_ref, w2_ref, fg_ref,
                o_ref, h_ref, acc_ref, *, final):
    j = pl.program_id(1)

    @pl.when(j == 0)
    def _():
        h_ref[...] = _norm_mod(x_ref[...], g_ref[...], sc_ref[...], sh_ref[...]).astype(BF16)
        acc_ref[...] = jnp.zeros_like(acc_ref)

    h = h_ref[...]
    u = _dot(h, w1_ref[...])
    a = (u * _sigmoid(u)) * _dot(h, w3_ref[...])
    acc_ref[...] += _dot(a.astype(BF16), w2_ref[...])

    @pl.when(j == pl.num_programs(1) - 1)
    def _():
        y = x_ref[...] + gate_ref[...] * acc_ref[...]
        if final:
            ms = jnp.mean(y * y, axis=-1, keepdims=True)
            y = y * lax.rsqrt(ms + NORM_EPS) * fg_ref[...]
        o_ref[...] = y


def _ffn(x, g, mod, w1, w3, w2, final_g, rows, n_rows, final):
    tm, d = rows.tm, D_MODEL
    tf = 512
    return pl.pallas_call(
        functools.partial(_ffn_kernel, final=final),
        grid=(n_rows // tm, D_FF // tf),
        in_specs=[
            pl.BlockSpec((tm, d), lambda i, j: (i, 0)),
            pl.BlockSpec((1, d), lambda i, j: (0, 0)),
            _mod_spec(rows, 4, 2),
            _mod_spec(rows, 3, 2),
            _mod_spec(rows, 5, 2),
            pl.BlockSpec((d, tf), lambda i, j: (0, j)),
            pl.BlockSpec((d, tf), lambda i, j: (0, j)),
            pl.BlockSpec((tf, d), lambda i, j: (j, 0)),
            pl.BlockSpec((1, d), lambda i, j: (0, 0)),
        ],
        out_specs=pl.BlockSpec((tm, d), lambda i, j: (i, 0)),
        out_shape=jax.ShapeDtypeStruct((n_rows, d), F32),
        scratch_shapes=[pltpu.VMEM((tm, d), BF16), pltpu.VMEM((tm, d), F32)],
        compiler_params=_params(("parallel", "arbitrary")),
        name="ffn_final" if final else "ffn",
    )(x, g, mod, mod, mod, w1, w3, w2, final_g)


def _rope_tables(seq, tm):
    n_rows = seq // GRID_W
    row = jnp.broadcast_to(jnp.arange(n_rows)[:, None], (n_rows, GRID_W)).reshape(-1)
    col = jnp.broadcast_to(jnp.arange(GRID_W)[None, :], (n_rows, GRID_W)).reshape(-1)
    inv = ROPE_THETA ** (-jnp.arange(ROPE_FREQS, dtype=F32) / ROPE_FREQS)
    ang = jnp.concatenate([row[:, None].astype(F32) * inv, col[:, None].astype(F32) * inv], axis=-1)
    cos, sin = jnp.cos(ang), jnp.sin(ang)
    cos2 = jnp.concatenate([cos, cos], axis=-1)
    sin2 = jnp.concatenate([-sin, sin], axis=-1)
    cos2 = jnp.concatenate([cos2, jnp.ones((tm, HEAD_DIM), F32)], axis=0)
    sin2 = jnp.concatenate([sin2, jnp.zeros((tm, HEAD_DIM), F32)], axis=0)
    return cos2, sin2


def _mix_weights(w2, a2):
    w = RWKV_WIDTH
    out = jnp.zeros((LORA_MIX, 4 * w), F32)
    out = out.at[0:LORA_DECAY, 0:w].set(w2[0])
    out = out.at[LORA_DECAY:2 * LORA_DECAY, w:2 * w].set(w2[1])
    o = 2 * LORA_DECAY
    out = out.at[o:o + LORA_ICLR, 2 * w:3 * w].set(a2[0])
    out = out.at[o + LORA_ICLR:o + 2 * LORA_ICLR, 3 * w:4 * w].set(a2[1])
    return out


def kernel(x, c, ctx, c_ctx, norm1_g, norm2_g, ada_w, ada_b, w_in, q_gain, k_gain, mu_prev, mu_next,
           w0, w2, a0, a2, g2, k_k, k_a, r_k, ln_x_w, ln_x_b, w_out, ffn_w1, ffn_w3, ffn_w2, final_g):
    batch, seq, d = x.shape
    ctx_len = ctx.shape[1]
    depth = w_in.shape[0]
    rows = _Rows(batch, seq, ctx_len, min(512, batch * ctx_len))
    w = RWKV_WIDTH

    cond = jnp.concatenate([c, c_ctx[None, :], jnp.zeros((SUBLANES - batch - 1, d), F32)], axis=0)
    mod_all = _modulation(cond, ada_w, ada_b)
    mod_all = mod_all.reshape(depth, SUBLANES, 1, 6 * d)
    cos2, sin2 = _rope_tables(seq, rows.tm)

    w_in_b = w_in.astype(BF16)
    w_out_b = w_out.astype(BF16)
    w1_b = ffn_w1.astype(BF16)
    w3_b = ffn_w3.astype(BF16)
    w2_b = ffn_w2.astype(BF16)

    tok = jnp.concatenate([x.reshape(batch * seq, d), ctx.reshape(batch * ctx_len, d)], axis=0)
    row1 = lambda a: a.reshape(1, -1)
    for i in range(depth):
        last = i == depth - 1
        mod = mod_all[i]
        g1 = row1(norm1_g[i])
        q, k, v = _attn_project(tok, g1, mod, w_in_b[i, :, :ATTN_COLS], row1(q_gain[i]), row1(k_gain[i]),
                                cos2, sin2, rows)
        f_rkv = _project(tok, g1, mod, w_in_b[i, :, ATTN_COLS:ATTN_COLS + RKV_COLS], rows, 1024)
        f_lora = _project(tok, g1, mod, w_in_b[i, :, ATTN_COLS + RKV_COLS:], rows, LORA_COLS)
        attn = _attention(q, k, v, rows, latent=True)
        if not last:
            attn = jnp.concatenate([attn, _attention(q, k, v, rows, latent=False)], axis=0)
        o_rkv = ATTN_COLS
        prep = {
            'mupf': row1(mu_prev[i, :RKV_COLS]), 'munf': row1(mu_next[i, :RKV_COLS]),
            'mupl': row1(mu_prev[i, RKV_COLS:]), 'munl': row1(mu_next[i, RKV_COLS:]),
            'g2': g2[i], 'wmix': _mix_weights(w2[i], a2[i]), 'w0': w0[i], 'a0': a0[i],
            'k_k': row1(k_k[i]), 'k_a': row1(k_a[i]), 'r_k': row1(r_k[i]),
        }
        r_s, kk_s, v_s, g_s, bonus, lw, kd, bd = _rwkv_prepare(f_rkv, f_lora, prep, rows)
        y = _rwkv_scan(r_s, kk_s, v_s, lw, kd, bd, rows)
        n_rows = rows.n_lat if last else rows.n_all
        tok = _mix_output(attn, y, bonus, g_s, row1(ln_x_w[i]), row1(ln_x_b[i]), tok, mod,
                          w_out_b[i, :Q_COLS], w_out_b[i, Q_COLS:], rows, n_rows)
        tok = _ffn(tok, row1(norm2_g[i]), mod, w1_b[i], w3_b[i], w2_b[i], row1(final_g),
                   rows, n_rows, final=last)
    return tok.reshape(batch, seq, d)
```

```python
import functools

import jax
import jax.numpy as jnp
from jax import lax
from jax.experimental import pallas as pl
from jax.experimental.pallas import tpu as pltpu

F32 = jnp.float32
BF16 = jnp.bfloat16

D_MODEL = 2048
HEAD_DIM = 128
N_Q_HEADS = 8
N_KV_HEADS = 2
GQA_GROUP = N_Q_HEADS // N_KV_HEADS
Q_COLS = N_Q_HEADS * HEAD_DIM
KV_COLS = N_KV_HEADS * HEAD_DIM
ATTN_COLS = Q_COLS + 2 * KV_COLS
RWKV_WIDTH = 1024
RWKV_HEAD = 64
LORA_GATE = 256
LORA_DECAY = 96
LORA_ICLR = 96
LORA_COLS = LORA_GATE + 2 * LORA_DECAY + 2 * LORA_ICLR
LORA_MIX = 2 * LORA_DECAY + 2 * LORA_ICLR
RKV_COLS = 3 * RWKV_WIDTH
D_FF = 5632
GRID_W = 64
ROPE_THETA = 10000.0
ROPE_FREQS = HEAD_DIM // 4
ATTN_SCALE = HEAD_DIM ** -0.5
LOG2_E = 1.4426950408889634
NORM_EPS = 1e-6
GN_EPS = 64e-5
L2_EPS = 1e-12
DECAY_SCALE = 0.6065306597126334

LANES = 128
SUBLANES = 8
VMEM_LIMIT = 56 * 1024 * 1024

ATTN_KV_CHUNK = 1024
SCAN_CHUNK = 64
SCAN_HEADS = 4
INV_BLOCK = 16
SCAN_GROUPS = 4
SCAN_SUBCHUNKS = 2

NN = ((1,), (0,))
NT = ((1,), (1,))
TN = ((0,), (0,))


def _dot(a, b, dims=NN):
    return lax.dot_general(a, b, (dims, ((), ())), preferred_element_type=F32)


def _split(x):
    hi = x.astype(BF16)
    lo = (x - hi.astype(F32)).astype(BF16)
    return hi, lo


def _mm3(a, b, dims=NN):
    ah, al = a
    bh, bl = b
    return _dot(ah, bh, dims) + (_dot(ah, bl, dims) + _dot(al, bh, dims))


def _mm3f(a, b, dims=NN):
    return _mm3(_split(a), _split(b), dims)


def _mm_exact_rhs(a, b_exact, dims=NN):
    ah, al = _split(a)
    return _dot(ah, b_exact, dims) + _dot(al, b_exact, dims)


def _sigmoid(z):
    return 1.0 / (1.0 + jnp.exp(-z))


def _norm_mod(x, g, scale, shift):
    ms = jnp.mean(x * x, axis=-1, keepdims=True)
    y = x * lax.rsqrt(ms + NORM_EPS) * g
    return y * (1.0 + scale) + shift


def _head_ones(width):
    r = lax.broadcasted_iota(jnp.int32, (width, width), 0) >> 6
    c = lax.broadcasted_iota(jnp.int32, (width, width), 1) >> 6
    return jnp.where(r == c, 1.0, 0.0).astype(BF16)


def _head_sum(x, ones):
    w = ones.shape[0]
    parts = [_mm_exact_rhs(x[:, c:c + w], ones) for c in range(0, x.shape[1], w)]
    return jnp.concatenate(parts, axis=1)


def _params(sem):
    return pltpu.CompilerParams(dimension_semantics=sem, vmem_limit_bytes=VMEM_LIMIT)


def _mod_kernel(c_ref, w_ref, b_ref, o_ref):
    c = c_ref[...]
    s = c * _sigmoid(c)
    o_ref[...] = _mm3f(s, w_ref[...]) + b_ref[...]


def _modulation(cond, ada_w, ada_b):
    depth, d, n = ada_w.shape
    tn = 512
    rows = cond.shape[0]
    return pl.pallas_call(
        _mod_kernel,
        grid=(depth, n // tn),
        in_specs=[
            pl.BlockSpec((rows, d), lambda l, j: (0, 0)),
            pl.BlockSpec((None, d, tn), lambda l, j: (l, 0, j)),
            pl.BlockSpec((None, 1, tn), lambda l, j: (l, 0, j)),
        ],
        out_specs=pl.BlockSpec((None, rows, tn), lambda l, j: (l, 0, j)),
        out_shape=jax.ShapeDtypeStruct((depth, rows, n), F32),
        compiler_params=_params(("parallel", "parallel")),
        name="modulation",
    )(cond, ada_w, ada_b.reshape(depth, 1, n))


class _Rows:
    def __init__(self, batch, seq, ctx_len, tm):
        self.batch, self.seq, self.ctx_len, self.tm = batch, seq, ctx_len, tm
        self.n_lat = batch * seq
        self.n_ctx = batch * ctx_len
        self.n_all = self.n_lat + self.n_ctx
        assert seq % tm == 0 and self.n_ctx % tm == 0
        self.blocks_per_batch = seq // tm

    def mod_row(self, i):
        return jnp.minimum(i // self.blocks_per_batch, self.batch)


def _mod_spec(rows, which, ngrid):
    d = D_MODEL
    if ngrid == 1:
        return pl.BlockSpec((None, 1, d), lambda i: (rows.mod_row(i), 0, which))
    return pl.BlockSpec((None, 1, d), lambda i, j: (rows.mod_row(i), 0, which))


def _proj_kernel(x_ref, g_ref, sc_ref, sh_ref, w_ref, o_ref, h_ref):
    @pl.when(pl.program_id(1) == 0)
    def _():
        h_ref[...] = _norm_mod(x_ref[...], g_ref[...], sc_ref[...], sh_ref[...]).astype(BF16)

    o_ref[...] = _dot(h_ref[...], w_ref[...]).astype(o_ref.dtype)


def _project(x, g, mod, w, rows, tn):
    n = w.shape[1]
    tm, d = rows.tm, D_MODEL
    return pl.pallas_call(
        _proj_kernel,
        grid=(rows.n_all // tm, n // tn),
        in_specs=[
            pl.BlockSpec((tm, d), lambda i, j: (i, 0)),
            pl.BlockSpec((1, d), lambda i, j: (0, 0)),
            _mod_spec(rows, 1, 2),
            _mod_spec(rows, 0, 2),
            pl.BlockSpec((d, tn), lambda i, j: (0, j)),
        ],
        out_specs=pl.BlockSpec((tm, tn), lambda i, j: (i, j)),
        out_shape=jax.ShapeDtypeStruct((rows.n_all, n), F32),
        scratch_shapes=[pltpu.VMEM((tm, d), BF16)],
        compiler_params=_params(("parallel", "arbitrary")),
        name="rwkv_proj",
    )(x, g, mod, mod, w)


def _attn_proj_kernel(x_ref, g_ref, sc_ref, sh_ref, w_ref, qg_ref, kg_ref, cos_ref, sin_ref,
                      q_ref, k_ref, vt_ref):
    h = _norm_mod(x_ref[...], g_ref[...], sc_ref[...], sh_ref[...]).astype(BF16)
    f = _dot(h, w_ref[...])
    cos = cos_ref[...]
    sin = sin_ref[...]

    def head(xh, gain, scale):
        ms = jnp.mean(xh * xh, axis=-1, keepdims=True)
        y = xh * lax.rsqrt(ms + NORM_EPS) * gain
        y = y * cos + pltpu.roll(y, HEAD_DIM // 2, 1) * sin
        return (y * scale).astype(BF16)

    qg = qg_ref[...]
    kg = kg_ref[...]
    for hq in range(N_Q_HEADS):
        c = hq * HEAD_DIM
        q_ref[:, c:c + HEAD_DIM] = head(f[:, c:c + HEAD_DIM], qg, ATTN_SCALE * LOG2_E)
    for hk in range(N_KV_HEADS):
        c = hk * HEAD_DIM
        k_ref[:, c:c + HEAD_DIM] = head(f[:, Q_COLS + c:Q_COLS + c + HEAD_DIM], kg, 1.0)
    vt_ref[...] = f[:, Q_COLS + KV_COLS:ATTN_COLS].T.astype(BF16)


def _attn_project(x, g, mod, w, q_gain, k_gain, cos2, sin2, rows):
    tm, d = rows.tm, D_MODEL
    bpb = rows.blocks_per_batch

    def rope_idx(i):
        return (jnp.where(i < rows.n_lat // tm, i % bpb, bpb), 0)

    return pl.pallas_call(
        _attn_proj_kernel,
        grid=(rows.n_all // tm,),
        in_specs=[
            pl.BlockSpec((tm, d), lambda i: (i, 0)),
            pl.BlockSpec((1, d), lambda i: (0, 0)),
            _mod_spec(rows, 1, 1),
            _mod_spec(rows, 0, 1),
            pl.BlockSpec((d, ATTN_COLS), lambda i: (0, 0)),
            pl.BlockSpec((1, HEAD_DIM), lambda i: (0, 0)),
            pl.BlockSpec((1, HEAD_DIM), lambda i: (0, 0)),
            pl.BlockSpec((tm, HEAD_DIM), rope_idx),
            pl.BlockSpec((tm, HEAD_DIM), rope_idx),
        ],
        out_specs=[
            pl.BlockSpec((tm, Q_COLS), lambda i: (i, 0)),
            pl.BlockSpec((tm, KV_COLS), lambda i: (i, 0)),
            pl.BlockSpec((KV_COLS, tm), lambda i: (0, i)),
        ],
        out_shape=[
            jax.ShapeDtypeStruct((rows.n_all, Q_COLS), BF16),
            jax.ShapeDtypeStruct((rows.n_all, KV_COLS), BF16),
            jax.ShapeDtypeStruct((KV_COLS, rows.n_all), BF16),
        ],
        compiler_params=_params(("parallel",)),
        name="attn_proj",
    )(x, g, mod, mod, w, q_gain, k_gain, cos2, sin2)


def _attn_kernel(q_ref, *refs, n_kv, kv_chunk):
    k_refs = refs[:n_kv]
    vt_refs = refs[n_kv:2 * n_kv]
    o_ref = refs[2 * n_kv]
    tq = q_ref.shape[0]
    heads = range(GQA_GROUP)
    each = lambda f, *ls: [f(*a) for a in zip(*ls)]
    q = [q_ref[:, h * HEAD_DIM:(h + 1) * HEAD_DIM] for h in heads]
    m = [jnp.full((1, tq), -jnp.inf, F32) for _ in heads]
    l = [jnp.zeros((1, tq), F32) for _ in heads]
    acc = [jnp.zeros((HEAD_DIM, tq), F32) for _ in heads]
    for k_ref, vt_ref in zip(k_refs, vt_refs):
        n = k_ref.shape[0]
        ck = min(kv_chunk, n)
        for c0 in range(0, n, ck):
            kc = k_ref[c0:c0 + ck, :]
            vtc = vt_ref[:, c0:c0 + ck]
            st = each(lambda qh: _dot(kc, qh, NT), q)
            m_new = each(lambda mh, sh: jnp.maximum(mh, jnp.max(sh, axis=0, keepdims=True)), m, st)
            alpha = each(lambda mh, mn: jnp.exp2(mh - mn), m, m_new)
            p = each(lambda sh, mn: jnp.exp2(sh - mn), st, m_new)
            l = each(lambda ah, lh, ph: ah * lh + jnp.sum(ph, axis=0, keepdims=True), alpha, l, p)
            acc = each(lambda ah, ch, ph: ah * ch + _dot(vtc, ph.astype(BF16)), alpha, acc, p)
            m = m_new
    for h in heads:
        o_ref[:, h * HEAD_DIM:(h + 1) * HEAD_DIM] = (acc[h] / l[h]).T.astype(o_ref.dtype)


def _attention(q, k, v, rows, latent):
    b, s, cl = rows.batch, rows.seq, rows.ctx_len
    gw = GQA_GROUP * HEAD_DIM
    ctx_blk0 = rows.n_lat // cl
    if latent:
        tq = min(256, s)
        nq = s // tq
        q_spec = pl.BlockSpec((tq, gw), lambda bi, hi, qi: (bi * nq + qi, hi))
        k_specs = [pl.BlockSpec((s, HEAD_DIM), lambda bi, hi, qi: (bi, hi)),
                   pl.BlockSpec((cl, HEAD_DIM), lambda bi, hi, qi: (ctx_blk0 + bi, hi))]
        vt_specs = [pl.BlockSpec((HEAD_DIM, s), lambda bi, hi, qi: (hi, bi)),
                    pl.BlockSpec((HEAD_DIM, cl), lambda bi, hi, qi: (hi, ctx_blk0 + bi))]
        n_rows = rows.n_lat
        name = "attention_latent"
    else:
        tq = cl
        nq = 1
        q_spec = pl.BlockSpec((tq, gw), lambda bi, hi, qi: (ctx_blk0 + bi, hi))
        k_specs = [pl.BlockSpec((cl, HEAD_DIM), lambda bi, hi, qi: (ctx_blk0 + bi, hi))]
        vt_specs = [pl.BlockSpec((HEAD_DIM, cl), lambda bi, hi, qi: (hi, ctx_blk0 + bi))]
        n_rows = rows.n_ctx
        name = "attention_context"
    n_kv = len(k_specs)
    return pl.pallas_call(
        functools.partial(_attn_kernel, n_kv=n_kv, kv_chunk=ATTN_KV_CHUNK),
        grid=(b, N_KV_HEADS, nq),
        in_specs=[q_spec] + k_specs + vt_specs,
        out_specs=pl.BlockSpec((tq, gw), lambda bi, hi, qi: (bi * nq + qi, hi)),
        out_shape=jax.ShapeDtypeStruct((n_rows, Q_COLS), BF16),
        compiler_params=_params(("parallel", "parallel", "arbitrary")),
        name=name,
    )(q, *([k] * n_kv), *([v] * n_kv))


def _rwkv_prep_kernel(f_ref, l_ref, fp_ref, lp_ref, fn_ref, ln_ref,
                      mupf_ref, munf_ref, mupl_ref, munl_ref,
                      g2_ref, wmix_ref, w0_ref, a0_ref, kk_scale_ref, ka_ref, rk_ref,
                      r_o, kk_o, v_o, g_o, bonus_o, lw_o, kd_o, bd_o,
                      *, n_lat_blocks, bps_lat, bps_ctx):
    i = pl.program_id(0)
    tm = f_ref.shape[0]
    is_lat = i < n_lat_blocks
    j = jnp.where(is_lat, i, i - n_lat_blocks)
    bps = jnp.where(is_lat, bps_lat, bps_ctx)
    pos = j % bps
    first = pos == 0
    last = pos == bps - 1
    row = lax.broadcasted_iota(jnp.int32, (tm, 1), 0)

    def shifted(x, prev_row, next_row, mu_prev, mu_next):
        prev_row = jnp.where(first, 0.0, prev_row)
        next_row = jnp.where(last, 0.0, next_row)
        prev = jnp.where(row == 0, prev_row, pltpu.roll(x, 1, 0))
        nxt = jnp.where(row == tm - 1, next_row, pltpu.roll(x, tm - 1, 0))
        return x + mu_prev * (prev - x) + mu_next * (nxt - x)

    w = RWKV_WIDTH

    def panel(p):
        c = slice(p * w, (p + 1) * w)
        return shifted(f_ref[:, c], fp_ref[SUBLANES - 1:SUBLANES, c], fn_ref[0:1, c],
                       mupf_ref[:, c], munf_ref[:, c])

    r = panel(0)
    k = panel(1)
    v = panel(2)
    lora = shifted(l_ref[...], lp_ref[SUBLANES - 1:SUBLANES, :], ln_ref[0:1, :],
                   mupl_ref[...], munl_ref[...])

    gate_in = _sigmoid(lora[:, :LORA_GATE])
    mix = lora[:, LORA_GATE:]
    lane = lax.broadcasted_iota(jnp.int32, (1, LORA_MIX), 1)
    mix_in = jnp.where(lane < 2 * LORA_DECAY, jnp.tanh(mix), mix)
    g = _dot(gate_in.astype(BF16), g2_ref[...])
    wa = _dot(mix_in.astype(BF16), wmix_ref[...])

    ones = _head_ones(2 * LANES)
    kk = k * kk_scale_ref[...]
    kk = kk * lax.rsqrt(_head_sum(kk * kk, ones) + L2_EPS)

    r_o[...] = r
    kk_o[...] = kk
    v_o[...] = v
    g_o[...] = g
    rk = rk_ref[...]
    ka = ka_ref[...]
    bonus = jnp.zeros_like(r)
    for d in range(2):
        z = w0_ref[d:d + 1, :] + wa[:, d * w:(d + 1) * w]
        lw_o[d] = -DECAY_SCALE * _sigmoid(z)
        a = _sigmoid(a0_ref[d:d + 1, :] + wa[:, (2 + d) * w:(3 + d) * w])
        kd = k * (1.0 + (a - 1.0) * ka)
        kd_o[d] = kd
        bd_o[d] = kk * a
        bonus = bonus + _head_sum(r * kd * rk, ones) * v
    bonus_o[...] = bonus


def _rwkv_prepare(f_rkv, f_lora, p, rows):
    tm = min(256, rows.ctx_len)
    n_all = rows.n_all
    nblk = n_all // tm
    sub_per_blk = tm // SUBLANES
    n_sub = n_all // SUBLANES
    w = RWKV_WIDTH

    def prev_idx(i):
        return (jnp.maximum(i * sub_per_blk - 1, 0), 0)

    def next_idx(i):
        return (jnp.minimum((i + 1) * sub_per_blk, n_sub - 1), 0)

    full = lambda shape: pl.BlockSpec(shape, lambda i: (0,) * len(shape))
    row_spec = lambda width: pl.BlockSpec((tm, width), lambda i: (i, 0))
    dir_spec = pl.BlockSpec((2, tm, w), lambda i: (0, i, 0))
    kern = functools.partial(_rwkv_prep_kernel, n_lat_blocks=rows.n_lat // tm,
                             bps_lat=rows.seq // tm, bps_ctx=rows.ctx_len // tm)
    sds = jax.ShapeDtypeStruct
    return pl.pallas_call(
        kern,
        grid=(nblk,),
        in_specs=[
            row_spec(RKV_COLS), row_spec(LORA_COLS),
            pl.BlockSpec((SUBLANES, RKV_COLS), prev_idx), pl.BlockSpec((SUBLANES, LORA_COLS), prev_idx),
            pl.BlockSpec((SUBLANES, RKV_COLS), next_idx), pl.BlockSpec((SUBLANES, LORA_COLS), next_idx),
            full((1, RKV_COLS)), full((1, RKV_COLS)), full((1, LORA_COLS)), full((1, LORA_COLS)),
            full((LORA_GATE, w)), full((LORA_MIX, 4 * w)), full((2, w)), full((2, w)),
            full((1, w)), full((1, w)), full((1, w)),
        ],
        out_specs=[row_spec(w)] * 5 + [dir_spec] * 3,
        out_shape=[sds((n_all, w), F32)] * 5 + [sds((2, n_all, w), F32)] * 3,
        compiler_params=_params(("parallel",)),
        name="rwkv_prepare",
    )(f_rkv, f_lora, f_rkv, f_lora, f_rkv, f_lora,
      p['mupf'], p['munf'], p['mupl'], p['munl'],
      p['g2'], p['wmix'], p['w0'], p['a0'], p['k_k'], p['k_a'], p['r_k'])


def _scan_kernel(r_ref, kk_ref, v_ref, lw_ref, kd_ref, bd_ref, y_ref, h_ref,
                 lincl_ref, msd_ref, mso_ref, mincl_ref, eye_ref, hm_ref, hmb_ref,
                 *, chunk, heads, groups):
    c = chunk
    wd = heads * RWKV_HEAD
    step = pl.program_id(3)

    @pl.when(step == 0)
    def _():
        h_ref[...] = jnp.zeros_like(h_ref)
        sgn = 1 - 2 * pl.program_id(0)
        iota = lambda shape, ax: lax.broadcasted_iota(jnp.int32, shape, ax)
        rel_c = (iota((c, c), 0) - iota((c, c), 1)) * sgn
        lincl_ref[...] = jnp.where(rel_c >= 0, 1.0, 0.0).astype(BF16)
        ti = iota((c, wd), 0)
        si = iota((c, wd), 1) & (c - 1)
        rel = (ti - si) * sgn
        log2b = INV_BLOCK.bit_length() - 1
        same_blk = (ti >> log2b) == (si >> log2b)
        strict = jnp.where(rel > 0, 1.0, 0.0)
        msd_ref[...] = jnp.where(same_blk, strict, 0.0)
        mso_ref[...] = jnp.where(same_blk, 0.0, strict)
        mincl_ref[...] = jnp.where(rel >= 0, 1.0, 0.0)
        eye_ref[...] = jnp.where(ti == si, 1.0, 0.0)
        hm = jnp.where((iota((wd, wd), 0) >> 6) == (iota((wd, wd), 1) >> 6), 1.0, 0.0)
        hm_ref[...] = hm
        hmb_ref[...] = hm.astype(BF16)

    n_sub = r_ref.shape[0] // c
    d = pl.program_id(0)
    rows = [pl.ds(pl.multiple_of(jnp.where(d == 0, i, n_sub - 1 - i) * c, c), c) for i in range(n_sub)]
    lanes = [slice(q * wd, (q + 1) * wd) for q in range(groups)]
    ins = [[ref[rw, ln] for rw in rows for ln in lanes]
           for ref in (r_ref, kk_ref, v_ref, lw_ref, kd_ref, bd_ref)]
    m_c, n_c, rhat, oloc = _chunk_operators(*ins, lincl_ref, msd_ref, mso_ref, mincl_ref, eye_ref,
                                            hm_ref, hmb_ref, heads)
    h = [h_ref[q] for q in range(groups)]
    for i, rw in enumerate(rows):
        part = slice(i * groups, (i + 1) * groups)
        h, y = _advance(m_c[part], n_c[part], rhat[part], oloc[part], h, hmb_ref, heads)
        for q in range(groups):
            y_ref[rw, lanes[q]] = y[q]
    for q in range(groups):
        h_ref[q] = h[q]


def _advance(m_c, n_c, rhat, oloc, h, hmb_ref, heads):
    c = RWKV_HEAD
    hmb = hmb_ref[...]

    def stacked(zb):
        return jnp.concatenate([zb] * heads, axis=0) * hmb

    def step(mi, ri, hi):
        lhs_hi, lhs_lo = _split(jnp.concatenate([mi, ri], axis=0))
        h_hi, h_lo = _split(hi)
        sh_hi = stacked(h_hi)
        return _dot(lhs_hi, sh_hi) + (_dot(lhs_hi, stacked(h_lo)) + _dot(lhs_lo, sh_hi))

    res = [step(*a) for a in zip(m_c, rhat, h)]
    h_new = [z[:c] + ni for z, ni in zip(res, n_c)]
    y = [z[c:] + oi for z, oi in zip(res, oloc)]
    return h_new, y


def _chunk_operators(r, kk, v, lw, k, b, lincl_ref, msd_ref, mso_ref, mincl_ref, eye_ref,
                     hm_ref, hmb_ref, heads):
    c = RWKV_HEAD
    wd = heads * RWKV_HEAD
    hmb = hmb_ref[...]
    each = lambda f, *ls: [f(*a) for a in zip(*ls)]

    def stacked(z):
        zb = z.astype(BF16)
        return jnp.concatenate([zb] * heads, axis=0) * hmb

    def mm(x, y_stacked, dims=NN):
        return _dot(x.astype(BF16), y_stacked, dims)

    def head_blocks(full):
        z = full * hm_ref[...]
        out = z[0:c]
        for hh in range(1, heads):
            out = out + z[hh * c:(hh + 1) * c]
        return out

    cat0 = lambda *xs: jnp.concatenate(xs, axis=0)
    cat1 = lambda *xs: jnp.concatenate(xs, axis=1)
    mul = lambda x, y: x * y
    sub = lambda x, y: x - y
    add = lambda x, y: x + y
    left = lambda z: z[:, :wd]
    right = lambda z: z[:, wd:]
    top = lambda z: z[:c]
    bottom = lambda z: z[c:]
    eye = eye_ref[...]
    lincl = lincl_ref[...]
    g = each(lambda x: _mm_exact_rhs_left(lincl, x), lw)
    gt = each(lambda x: jnp.sum(x, axis=0, keepdims=True), lw)
    e_x = each(lambda gi, lwi: jnp.exp(gi - lwi), g, lw)
    e_g = each(jnp.exp, g)
    e_n = each(lambda gi: jnp.exp(-gi), g)
    e_c = each(lambda gti, gi: jnp.exp(gti - gi), gt, g)
    kt = each(mul, kk, e_x)
    rt = each(mul, r, e_g)
    bh = each(mul, b, e_n)
    kh = each(mul, k, e_n)
    kb = each(mul, k, e_c)
    bb = each(mul, b, e_c)

    gm = each(lambda kti, rti, bhi, khi: mm(cat0(kti, rti), cat0(stacked(bhi), stacked(khi)), NT),
              kt, rt, bh, kh)
    msd = msd_ref[...]
    mso = mso_ref[...]
    m_incl = mincl_ref[...]
    a_d = each(lambda z: z[:c, :wd] * msd, gm)
    a_o = each(lambda z: z[:c, :wd] * mso, gm)
    b_m = each(lambda z: z[:c, wd:] * (msd + mso), gm)
    e_m = each(lambda z: z[c:, :wd] * m_incl, gm)
    c_m = each(lambda z: z[c:, wd:] * m_incl, gm)

    mm_st = lambda x, y: mm(x, stacked(y))
    x1 = each(lambda z: -z, a_d)
    x2 = each(mm_st, x1, x1)
    s2 = each(stacked, x2)
    x4 = each(mm, x2, s2)
    s4 = each(stacked, x4)
    x8 = each(mm, x4, s4)
    dm = each(lambda z: eye - z, a_d)
    dm = each(lambda di, si: di + mm(di, si), dm, s2)
    dm = each(lambda di, si: di + mm(di, si), dm, s4)
    dm = each(lambda di, xi: di + mm_st(di, xi), dm, x8)
    n1 = each(mm_st, dm, a_o)
    n2 = each(mm_st, n1, n1)
    t1 = each(lambda z: eye - z, n1)
    t1 = each(lambda ti, ni: ti + mm_st(ti, ni), t1, n2)
    t_m = each(mm_st, t1, dm)

    bcv = each(lambda bi, ci, vi: mm(cat0(bi, ci), stacked(vi)), b_m, c_m, v)
    bv = each(top, bcv)
    cv = each(bottom, bcv)
    wu = each(lambda ti, kti, bvi: mm(ti, cat1(stacked(kti), stacked(bvi))), t_m, kt, bv)
    wt = each(left, wu)
    ut = each(right, wu)
    ew = each(lambda ei, wi, ui: mm(ei, cat1(stacked(wi), stacked(ui))), e_m, wt, ut)
    rhat = each(lambda ri, z: ri - z[:, :wd], rt, ew)
    oloc = each(lambda ci, z: ci - z[:, wd:], cv, ew)
    full_m = each(lambda bi, wi: _dot(bi.astype(BF16), wi.astype(BF16), TN), bb, wt)
    full_n = each(lambda ki, bi, vi, ui: _dot(cat0(ki, bi).astype(BF16), cat0(vi, -ui).astype(BF16), TN),
                  kb, bb, v, ut)
    m_c = each(lambda gti, fi: eye * jnp.exp(gti) - head_blocks(fi), gt, full_m)
    n_c = each(head_blocks, full_n)
    return m_c, n_c, rhat, oloc


def _mm_exact_rhs_left(a_exact, b):
    bh = b.astype(BF16)
    r1 = b - bh.astype(F32)
    bm = r1.astype(BF16)
    bl = (r1 - bm.astype(F32)).astype(BF16)
    return _dot(a_exact, bh) + (_dot(a_exact, bm) + _dot(a_exact, bl))


def _rwkv_scan(r, kk, v, lw, kd, bd, rows):
    c, heads = SCAN_CHUNK, SCAN_HEADS
    wd = heads * RWKV_HEAD
    nq = RWKV_WIDTH // wd
    rb = c * SCAN_SUBCHUNKS
    assert rows.ctx_len % rb == 0 and rows.seq % rb == 0
    nc_ctx = rows.ctx_len // rb
    nc_lat = rows.seq // rb
    lat_blocks = rows.n_lat // rb

    def chunk_idx(d, bi, st):
        in_ctx = st < nc_ctx
        t_ctx = jnp.where(d == 0, st, nc_ctx - 1 - st)
        sl = st - nc_ctx
        t_lat = jnp.where(d == 0, sl, nc_lat - 1 - sl)
        return jnp.where(in_ctx, lat_blocks + bi * nc_ctx + t_ctx, bi * nc_lat + t_lat)

    assert c == RWKV_HEAD
    groups = SCAN_GROUPS
    bw = wd * groups
    shared = pl.BlockSpec((rb, bw), lambda d, bi, q, st: (chunk_idx(d, bi, st), q))
    per_dir = pl.BlockSpec((None, rb, bw), lambda d, bi, q, st: (d, chunk_idx(d, bi, st), q))
    return pl.pallas_call(
        functools.partial(_scan_kernel, chunk=c, heads=heads, groups=groups),
        grid=(2, rows.batch, nq // groups, nc_ctx + nc_lat),
        in_specs=[shared, shared, shared, per_dir, per_dir, per_dir],
        out_specs=per_dir,
        out_shape=jax.ShapeDtypeStruct((2, rows.n_all, RWKV_WIDTH), F32),
        scratch_shapes=[
            pltpu.VMEM((groups, c, wd), F32),
            pltpu.VMEM((c, c), BF16),
            pltpu.VMEM((c, wd), F32),
            pltpu.VMEM((c, wd), F32),
            pltpu.VMEM((c, wd), F32),
            pltpu.VMEM((c, wd), F32),
            pltpu.VMEM((wd, wd), F32),
            pltpu.VMEM((wd, wd), BF16),
        ],
        compiler_params=_params(("parallel", "parallel", "parallel", "arbitrary")),
        name="rwkv_scan",
    )(r, kk, v, lw, kd, bd)


def _out_kernel(attn_ref, y_ref, bonus_ref, g_ref, lnw_ref, lnb_ref, x_ref, gate_ref,
                wa_ref, wr_ref, o_ref, rw_ref):
    @pl.when(pl.program_id(1) == 0)
    def _():
        ones = _head_ones(2 * LANES)
        y = y_ref[0] + y_ref[1]
        mean = _head_sum(y, ones) * (1.0 / RWKV_HEAD)
        yc = y - mean
        var = _head_sum(yc * yc, ones) * (1.0 / RWKV_HEAD)
        yn = yc * lax.rsqrt(var + GN_EPS) * lnw_ref[...] + lnb_ref[...]
        rw_ref[...] = ((yn + bonus_ref[...]) * g_ref[...]).astype(BF16)

    out = _dot(attn_ref[...], wa_ref[...]) + _dot(rw_ref[...], wr_ref[...])
    o_ref[...] = x_ref[...] + gate_ref[...] * out


def _mix_output(attn, y, bonus, g, ln_w, ln_b, x, mod, w_attn, w_rwkv, rows, n_rows):
    tm, d = rows.tm, D_MODEL
    tn = 1024
    w = RWKV_WIDTH
    return pl.pallas_call(
        _out_kernel,
        grid=(n_rows // tm, d // tn),
        in_specs=[
            pl.BlockSpec((tm, Q_COLS), lambda i, j: (i, 0)),
            pl.BlockSpec((2, tm, w), lambda i, j: (0, i, 0)),
            pl.BlockSpec((tm, w), lambda i, j: (i, 0)),
            pl.BlockSpec((tm, w), lambda i, j: (i, 0)),
            pl.BlockSpec((1, w), lambda i, j: (0, 0)),
            pl.BlockSpec((1, w), lambda i, j: (0, 0)),
            pl.BlockSpec((tm, tn), lambda i, j: (i, j)),
            pl.BlockSpec((None, 1, tn), lambda i, j: (rows.mod_row(i), 0, 2 * (d // tn) + j)),
            pl.BlockSpec((Q_COLS, tn), lambda i, j: (0, j)),
            pl.BlockSpec((w, tn), lambda i, j: (0, j)),
        ],
        out_specs=pl.BlockSpec((tm, tn), lambda i, j: (i, j)),
        out_shape=jax.ShapeDtypeStruct((n_rows, d), F32),
        scratch_shapes=[pltpu.VMEM((tm, w), BF16)],
        compiler_params=_params(("parallel", "arbitrary")),
        name="mix_output",
    )(attn, y, bonus, g, ln_w, ln_b, x, mod, w_attn, w_rwkv)


def _ffn_kernel(x_ref, g_ref, sc_ref, sh_ref, gate_ref, w1_ref, w3_ref, w2_ref, fg_ref,
                o_ref, h_ref, acc_ref, *, final):
    j = pl.program_id(1)

    @pl.when(j == 0)
    def _():
        h_ref[...] = _norm_mod(x_ref[...], g_ref[...], sc_ref[...], sh_ref[...]).astype(BF16)
        acc_ref[...] = jnp.zeros_like(acc_ref)

    h = h_ref[...]
    u = _dot(h, w1_ref[...])
    a = (u * _sigmoid(u)) * _dot(h, w3_ref[...])
    acc_ref[...] += _dot(a.astype(BF16), w2_ref[...])

    @pl.when(j == pl.num_programs(1) - 1)
    def _():
        y = x_ref[...] + gate_ref[...] * acc_ref[...]
        if final:
            ms = jnp.mean(y * y, axis=-1, keepdims=True)
            y = y * lax.rsqrt(ms + NORM_EPS) * fg_ref[...]
        o_ref[...] = y


def _ffn(x, g, mod, w1, w3, w2, final_g, rows, n_rows, final):
    tm, d = rows.tm, D_MODEL
    tf = 512
    return pl.pallas_call(
        functools.partial(_ffn_kernel, final=final),
        grid=(n_rows // tm, D_FF // tf),
        in_specs=[
            pl.BlockSpec((tm, d), lambda i, j: (i, 0)),
            pl.BlockSpec((1, d), lambda i, j: (0, 0)),
            _mod_spec(rows, 4, 2),
            _mod_spec(rows, 3, 2),
            _mod_spec(rows, 5, 2),
            pl.BlockSpec((d, tf), lambda i, j: (0, j)),
            pl.BlockSpec((d, tf), lambda i, j: (0, j)),
            pl.BlockSpec((tf, d), lambda i, j: (j, 0)),
            pl.BlockSpec((1, d), lambda i, j: (0, 0)),
        ],
        out_specs=pl.BlockSpec((tm, d), lambda i, j: (i, 0)),
        out_shape=jax.ShapeDtypeStruct((n_rows, d), F32),
        scratch_shapes=[pltpu.VMEM((tm, d), BF16), pltpu.VMEM((tm, d), F32)],
        compiler_params=_params(("parallel", "arbitrary")),
        name="ffn_final" if final else "ffn",
    )(x, g, mod, mod, mod, w1, w3, w2, final_g)


def _rope_tables(seq, tm):
    n_rows = seq // GRID_W
    row = jnp.broadcast_to(jnp.arange(n_rows)[:, None], (n_rows, GRID_W)).reshape(-1)
    col = jnp.broadcast_to(jnp.arange(GRID_W)[None, :], (n_rows, GRID_W)).reshape(-1)
    inv = ROPE_THETA ** (-jnp.arange(ROPE_FREQS, dtype=F32) / ROPE_FREQS)
    ang = jnp.concatenate([row[:, None].astype(F32) * inv, col[:, None].astype(F32) * inv], axis=-1)
    cos, sin = jnp.cos(ang), jnp.sin(ang)
    cos2 = jnp.concatenate([cos, cos], axis=-1)
    sin2 = jnp.concatenate([-sin, sin], axis=-1)
    cos2 = jnp.concatenate([cos2, jnp.ones((tm, HEAD_DIM), F32)], axis=0)
    sin2 = jnp.concatenate([sin2, jnp.zeros((tm, HEAD_DIM), F32)], axis=0)
    return cos2, sin2


def _mix_weights(w2, a2):
    w = RWKV_WIDTH
    out = jnp.zeros((LORA_MIX, 4 * w), F32)
    out = out.at[0:LORA_DECAY, 0:w].set(w2[0])
    out = out.at[LORA_DECAY:2 * LORA_DECAY, w:2 * w].set(w2[1])
    o = 2 * LORA_DECAY
    out = out.at[o:o + LORA_ICLR, 2 * w:3 * w].set(a2[0])
    out = out.at[o + LORA_ICLR:o + 2 * LORA_ICLR, 3 * w:4 * w].set(a2[1])
    return out


def kernel(x, c, ctx, c_ctx, norm1_g, norm2_g, ada_w, ada_b, w_in, q_gain, k_gain, mu_prev, mu_next,
           w0, w2, a0, a2, g2, k_k, k_a, r_k, ln_x_w, ln_x_b, w_out, ffn_w1, ffn_w3, ffn_w2, final_g):
    batch, seq, d = x.shape
    ctx_len = ctx.shape[1]
    depth = w_in.shape[0]
    rows = _Rows(batch, seq, ctx_len, min(512, batch * ctx_len))
    w = RWKV_WIDTH

    cond = jnp.concatenate([c, c_ctx[None, :], jnp.zeros((SUBLANES - batch - 1, d), F32)], axis=0)
    mod_all = _modulation(cond, ada_w, ada_b)
    mod_all = mod_all.reshape(depth, SUBLANES, 1, 6 * d)
    cos2, sin2 = _rope_tables(seq, rows.tm)

    w_in_b = w_in.astype(BF16)
    w_out_b = w_out.astype(BF16)
    w1_b = ffn_w1.astype(BF16)
    w3_b = ffn_w3.astype(BF16)
    w2_b = ffn_w2.astype(BF16)

    tok = jnp.concatenate([x.reshape(batch * seq, d), ctx.reshape(batch * ctx_len, d)], axis=0)
    row1 = lambda a: a.reshape(1, -1)
    for i in range(depth):
        last = i == depth - 1
        mod = mod_all[i]
        g1 = row1(norm1_g[i])
        q, k, v = _attn_project(tok, g1, mod, w_in_b[i, :, :ATTN_COLS], row1(q_gain[i]), row1(k_gain[i]),
                                cos2, sin2, rows)
        f_rkv = _project(tok, g1, mod, w_in_b[i, :, ATTN_COLS:ATTN_COLS + RKV_COLS], rows, 1024)
        f_lora = _project(tok, g1, mod, w_in_b[i, :, ATTN_COLS + RKV_COLS:], rows, LORA_COLS)
        attn = _attention(q, k, v, rows, latent=True)
        if not last:
            attn = jnp.concatenate([attn, _attention(q, k, v, rows, latent=False)], axis=0)
        o_rkv = ATTN_COLS
        prep = {
            'mupf': row1(mu_prev[i, :RKV_COLS]), 'munf': row1(mu_next[i, :RKV_COLS]),
            'mupl': row1(mu_prev[i, RKV_COLS:]), 'munl': row1(mu_next[i, RKV_COLS:]),
            'g2': g2[i].astype(BF16), 'wmix': _mix_weights(w2[i], a2[i]).astype(BF16),
            'w0': w0[i], 'a0': a0[i],
            'k_k': row1(k_k[i]), 'k_a': row1(k_a[i]), 'r_k': row1(r_k[i]),
        }
        r_s, kk_s, v_s, g_s, bonus, lw, kd, bd = _rwkv_prepare(f_rkv, f_lora, prep, rows)
        y = _rwkv_scan(r_s, kk_s, v_s, lw, kd, bd, rows)
        n_rows = rows.n_lat if last else rows.n_all
        tok = _mix_output(attn, y, bonus, g_s, row1(ln_x_w[i]), row1(ln_x_b[i]), tok, mod,
                          w_out_b[i, :Q_COLS], w_out_b[i, Q_COLS:], rows, n_rows)
        tok = _ffn(tok, row1(norm2_g[i]), mod, w1_b[i], w3_b[i], w2_b[i], row1(final_g),
                   rows, n_rows, final=last)
    return tok.reshape(batch, seq, d)
```

```python
import functools

import jax
import jax.numpy as jnp
from jax import lax
from jax.experimental import pallas as pl
from jax.experimental.pallas import tpu as pltpu

F32 = jnp.float32
BF16 = jnp.bfloat16

D_MODEL = 2048
HEAD_DIM = 128
N_Q_HEADS = 8
N_KV_HEADS = 2
GQA_GROUP = N_Q_HEADS // N_KV_HEADS
Q_COLS = N_Q_HEADS * HEAD_DIM
KV_COLS = N_KV_HEADS * HEAD_DIM
ATTN_COLS = Q_COLS + 2 * KV_COLS
RWKV_WIDTH = 1024
RWKV_HEAD = 64
LORA_GATE = 256
LORA_DECAY = 96
LORA_ICLR = 96
LORA_COLS = LORA_GATE + 2 * LORA_DECAY + 2 * LORA_ICLR
LORA_MIX = 2 * LORA_DECAY + 2 * LORA_ICLR
RKV_COLS = 3 * RWKV_WIDTH
D_FF = 5632
GRID_W = 64
ROPE_THETA = 10000.0
ROPE_FREQS = HEAD_DIM // 4
ATTN_SCALE = HEAD_DIM ** -0.5
LOG2_E = 1.4426950408889634
NORM_EPS = 1e-6
GN_EPS = 64e-5
L2_EPS = 1e-12
DECAY_SCALE = 0.6065306597126334

LANES = 128
SUBLANES = 8
VMEM_LIMIT = 56 * 1024 * 1024

ATTN_KV_CHUNK = 1024
SCAN_CHUNK = 64
SCAN_HEADS = 4
INV_BLOCK = 16
SCAN_GROUPS = 4
SCAN_SUBCHUNKS = 2

NN = ((1,), (0,))
NT = ((1,), (1,))
TN = ((0,), (0,))


def _dot(a, b, dims=NN):
    return lax.dot_general(a, b, (dims, ((), ())), preferred_element_type=F32)


def _split(x):
    hi = x.astype(BF16)
    lo = (x - hi.astype(F32)).astype(BF16)
    return hi, lo


def _mm3(a, b, dims=NN):
    ah, al = a
    bh, bl = b
    return _dot(ah, bh, dims) + (_dot(ah, bl, dims) + _dot(al, bh, dims))


def _mm3f(a, b, dims=NN):
    return _mm3(_split(a), _split(b), dims)


def _mm_exact_rhs(a, b_exact, dims=NN):
    ah, al = _split(a)
    return _dot(ah, b_exact, dims) + _dot(al, b_exact, dims)


def _sigmoid(z):
    return 1.0 / (1.0 + jnp.exp(-z))


def _norm_mod(x, g, scale, shift):
    ms = jnp.mean(x * x, axis=-1, keepdims=True)
    y = x * lax.rsqrt(ms + NORM_EPS) * g
    return y * (1.0 + scale) + shift


def _head_ones(width):
    r = lax.broadcasted_iota(jnp.int32, (width, width), 0) >> 6
    c = lax.broadcasted_iota(jnp.int32, (width, width), 1) >> 6
    return jnp.where(r == c, 1.0, 0.0).astype(BF16)


def _head_sum(x, ones):
    w = ones.shape[0]
    xb = x.astype(BF16)
    parts = [_dot(xb[:, c:c + w], ones) for c in range(0, x.shape[1], w)]
    return jnp.concatenate(parts, axis=1)


def _params(sem):
    return pltpu.CompilerParams(dimension_semantics=sem, vmem_limit_bytes=VMEM_LIMIT)


def _mod_kernel(c_ref, w_ref, b_ref, o_ref):
    c = c_ref[...]
    s = c * _sigmoid(c)
    o_ref[...] = _mm3f(s, w_ref[...]) + b_ref[...]


def _modulation(cond, ada_w, ada_b):
    depth, d, n = ada_w.shape
    tn = 512
    rows = cond.shape[0]
    return pl.pallas_call(
        _mod_kernel,
        grid=(depth, n // tn),
        in_specs=[
            pl.BlockSpec((rows, d), lambda l, j: (0, 0)),
            pl.BlockSpec((None, d, tn), lambda l, j: (l, 0, j)),
            pl.BlockSpec((None, 1, tn), lambda l, j: (l, 0, j)),
        ],
        out_specs=pl.BlockSpec((None, rows, tn), lambda l, j: (l, 0, j)),
        out_shape=jax.ShapeDtypeStruct((depth, rows, n), F32),
        compiler_params=_params(("parallel", "parallel")),
        name="modulation",
    )(cond, ada_w, ada_b.reshape(depth, 1, n))


class _Rows:
    def __init__(self, batch, seq, ctx_len, tm):
        self.batch, self.seq, self.ctx_len, self.tm = batch, seq, ctx_len, tm
        self.n_lat = batch * seq
        self.n_ctx = batch * ctx_len
        self.n_all = self.n_lat + self.n_ctx
        assert seq % tm == 0 and self.n_ctx % tm == 0
        self.blocks_per_batch = seq // tm

    def mod_row(self, i, tm=None):
        per_batch = self.blocks_per_batch if tm is None else self.seq // tm
        return jnp.minimum(i // per_batch, self.batch)


def _mod_spec(rows, which, ngrid):
    d = D_MODEL
    if ngrid == 1:
        return pl.BlockSpec((None, 1, d), lambda i: (rows.mod_row(i), 0, which))
    return pl.BlockSpec((None, 1, d), lambda i, j: (rows.mod_row(i), 0, which))


def _proj_kernel(x_ref, g_ref, sc_ref, sh_ref, w_ref, o_ref, h_ref):
    @pl.when(pl.program_id(1) == 0)
    def _():
        h_ref[...] = _norm_mod(x_ref[...], g_ref[...], sc_ref[...], sh_ref[...]).astype(BF16)

    o_ref[...] = _dot(h_ref[...], w_ref[...]).astype(o_ref.dtype)


def _project(x, g, mod, w, layer, col0, n, rows, tn):
    tm, d = rows.tm, D_MODEL
    assert col0 % tn == 0 and n % tn == 0
    return pl.pallas_call(
        _proj_kernel,
        grid=(rows.n_all // tm, n // tn),
        in_specs=[
            pl.BlockSpec((tm, d), lambda i, j: (i, 0)),
            pl.BlockSpec((1, d), lambda i, j: (0, 0)),
            _mod_spec(rows, 1, 2),
            _mod_spec(rows, 0, 2),
            pl.BlockSpec((None, d, tn), lambda i, j: (layer, 0, col0 // tn + j)),
        ],
        out_specs=pl.BlockSpec((tm, tn), lambda i, j: (i, j)),
        out_shape=jax.ShapeDtypeStruct((rows.n_all, n), F32),
        scratch_shapes=[pltpu.VMEM((tm, d), BF16)],
        compiler_params=_params(("parallel", "arbitrary")),
        name="rwkv_proj",
    )(x, g, mod, mod, w)


def _attn_proj_kernel(x_ref, g_ref, sc_ref, sh_ref, w_ref, qg_ref, kg_ref, cos_ref, sin_ref,
                      q_ref, k_ref, vt_ref):
    h = _norm_mod(x_ref[...], g_ref[...], sc_ref[...], sh_ref[...]).astype(BF16)
    f = _dot(h, w_ref[...])
    cos = cos_ref[...]
    sin = sin_ref[...]

    def head(xh, gain, scale):
        ms = jnp.mean(xh * xh, axis=-1, keepdims=True)
        y = xh * lax.rsqrt(ms + NORM_EPS) * gain
        y = y * cos + pltpu.roll(y, HEAD_DIM // 2, 1) * sin
        return (y * scale).astype(BF16)

    qg = qg_ref[...]
    kg = kg_ref[...]
    for hq in range(N_Q_HEADS):
        c = hq * HEAD_DIM
        q_ref[:, c:c + HEAD_DIM] = head(f[:, c:c + HEAD_DIM], qg, ATTN_SCALE * LOG2_E)
    for hk in range(N_KV_HEADS):
        c = hk * HEAD_DIM
        k_ref[:, c:c + HEAD_DIM] = head(f[:, Q_COLS + c:Q_COLS + c + HEAD_DIM], kg, 1.0)
    vt_ref[...] = f[:, Q_COLS + KV_COLS:ATTN_COLS].T.astype(BF16)


def _attn_project(x, g, mod, w, layer, q_gain, k_gain, cos2, sin2, rows):
    tm, d = rows.tm, D_MODEL
    bpb = rows.blocks_per_batch

    def rope_idx(i):
        return (jnp.where(i < rows.n_lat // tm, i % bpb, bpb), 0)

    return pl.pallas_call(
        _attn_proj_kernel,
        grid=(rows.n_all // tm,),
        in_specs=[
            pl.BlockSpec((tm, d), lambda i: (i, 0)),
            pl.BlockSpec((1, d), lambda i: (0, 0)),
            _mod_spec(rows, 1, 1),
            _mod_spec(rows, 0, 1),
            pl.BlockSpec((None, d, ATTN_COLS), lambda i: (layer, 0, 0)),
            pl.BlockSpec((1, HEAD_DIM), lambda i: (0, 0)),
            pl.BlockSpec((1, HEAD_DIM), lambda i: (0, 0)),
            pl.BlockSpec((tm, HEAD_DIM), rope_idx),
            pl.BlockSpec((tm, HEAD_DIM), rope_idx),
        ],
        out_specs=[
            pl.BlockSpec((tm, Q_COLS), lambda i: (i, 0)),
            pl.BlockSpec((tm, KV_COLS), lambda i: (i, 0)),
            pl.BlockSpec((KV_COLS, tm), lambda i: (0, i)),
        ],
        out_shape=[
            jax.ShapeDtypeStruct((rows.n_all, Q_COLS), BF16),
            jax.ShapeDtypeStruct((rows.n_all, KV_COLS), BF16),
            jax.ShapeDtypeStruct((KV_COLS, rows.n_all), BF16),
        ],
        compiler_params=_params(("parallel",)),
        name="attn_proj",
    )(x, g, mod, mod, w, q_gain, k_gain, cos2, sin2)


def _attn_kernel(q_ref, *refs, n_kv, kv_chunk):
    k_refs = refs[:n_kv]
    vt_refs = refs[n_kv:2 * n_kv]
    o_ref = refs[-1]
    tq = q_ref.shape[0]
    heads = range(GQA_GROUP)
    each = lambda f, *ls: [f(*a) for a in zip(*ls)]
    q = [q_ref[:, h * HEAD_DIM:(h + 1) * HEAD_DIM] for h in heads]
    m = [jnp.full((1, tq), -jnp.inf, F32) for _ in heads]
    l = [jnp.zeros((1, tq), F32) for _ in heads]
    acc = [jnp.zeros((HEAD_DIM, tq), F32) for _ in heads]
    for k_ref, vt_ref in zip(k_refs, vt_refs):
        n = k_ref.shape[0]
        ck = min(kv_chunk, n)
        for c0 in range(0, n, ck):
            kc = k_ref[c0:c0 + ck, :]
            vtc = vt_ref[:, c0:c0 + ck]
            st = each(lambda qh: _dot(kc, qh, NT), q)
            m_new = each(lambda mh, sh: jnp.maximum(mh, jnp.max(sh, axis=0, keepdims=True)), m, st)
            alpha = each(lambda mh, mn: jnp.exp2(mh - mn), m, m_new)
            p = each(lambda sh, mn: jnp.exp2(sh - mn), st, m_new)
            l = each(lambda ah, lh, ph: ah * lh + jnp.sum(ph, axis=0, keepdims=True), alpha, l, p)
            acc = each(lambda ah, ch, ph: ah * ch + _dot(vtc, ph.astype(BF16)), alpha, acc, p)
            m = m_new
    for h in heads:
        o_ref[:, h * HEAD_DIM:(h + 1) * HEAD_DIM] = (acc[h] / l[h]).T.astype(o_ref.dtype)


def _attention(q, k, v, rows, latent, out=None):
    b, s, cl = rows.batch, rows.seq, rows.ctx_len
    gw = GQA_GROUP * HEAD_DIM
    ctx_blk0 = rows.n_lat // cl
    if latent:
        tq = min(256, s)
        nq = s // tq
        q_spec = pl.BlockSpec((tq, gw), lambda bi, hi, qi: (bi * nq + qi, hi))
        k_specs = [pl.BlockSpec((s, HEAD_DIM), lambda bi, hi, qi: (bi, hi)),
                   pl.BlockSpec((cl, HEAD_DIM), lambda bi, hi, qi: (ctx_blk0 + bi, hi))]
        vt_specs = [pl.BlockSpec((HEAD_DIM, s), lambda bi, hi, qi: (hi, bi)),
                    pl.BlockSpec((HEAD_DIM, cl), lambda bi, hi, qi: (hi, ctx_blk0 + bi))]
        name = "attention_latent"
    else:
        tq = cl
        nq = 1
        q_spec = pl.BlockSpec((tq, gw), lambda bi, hi, qi: (ctx_blk0 + bi, hi))
        k_specs = [pl.BlockSpec((cl, HEAD_DIM), lambda bi, hi, qi: (ctx_blk0 + bi, hi))]
        vt_specs = [pl.BlockSpec((HEAD_DIM, cl), lambda bi, hi, qi: (hi, ctx_blk0 + bi))]
        name = "attention_context"
    n_kv = len(k_specs)
    in_specs = [q_spec] + k_specs + vt_specs
    args = [q] + [k] * n_kv + [v] * n_kv
    aliases = {}
    if out is not None:
        aliases = {len(args): 0}
        in_specs.append(pl.BlockSpec(memory_space=pl.ANY))
        args.append(out)
    return pl.pallas_call(
        functools.partial(_attn_kernel, n_kv=n_kv, kv_chunk=ATTN_KV_CHUNK),
        grid=(b, N_KV_HEADS, nq),
        in_specs=in_specs,
        out_specs=q_spec,
        out_shape=jax.ShapeDtypeStruct((rows.n_all, Q_COLS), BF16),
        input_output_aliases=aliases,
        compiler_params=_params(("parallel", "parallel", "arbitrary")),
        name=name,
    )(*args)


def _rwkv_prep_kernel(f_ref, l_ref, fp_ref, lp_ref, fn_ref, ln_ref,
                      mupf_ref, munf_ref, mupl_ref, munl_ref,
                      g2_ref, wmix_ref, w0_ref, a0_ref, kk_scale_ref, ka_ref, rk_ref,
                      r_o, kk_o, v_o, g_o, bonus_o, lw_o, kd_o, bd_o,
                      *, n_lat_blocks, bps_lat, bps_ctx):
    i = pl.program_id(0)
    tm = f_ref.shape[0]
    is_lat = i < n_lat_blocks
    j = jnp.where(is_lat, i, i - n_lat_blocks)
    bps = jnp.where(is_lat, bps_lat, bps_ctx)
    pos = j % bps
    first = pos == 0
    last = pos == bps - 1
    row = lax.broadcasted_iota(jnp.int32, (tm, 1), 0)

    def shifted(x, prev_row, next_row, mu_prev, mu_next):
        prev_row = jnp.where(first, 0.0, prev_row)
        next_row = jnp.where(last, 0.0, next_row)
        down = pltpu.roll(x, 1, 0)
        up = pltpu.roll(x, tm - 1, 0)
        s = SUBLANES
        prev = jnp.concatenate([jnp.where(row[:s] == 0, prev_row, down[:s]), down[s:]], axis=0)
        nxt = jnp.concatenate([up[:tm - s], jnp.where(row[tm - s:] == tm - 1, next_row, up[tm - s:])], axis=0)
        return (1.0 - mu_prev - mu_next) * x + mu_prev * prev + mu_next * nxt

    w = RWKV_WIDTH

    def panel(p):
        c = slice(p * w, (p + 1) * w)
        return shifted(f_ref[:, c], fp_ref[SUBLANES - 1:SUBLANES, c], fn_ref[0:1, c],
                       mupf_ref[:, c], munf_ref[:, c])

    r = panel(0)
    k = panel(1)
    v = panel(2)
    lora = shifted(l_ref[...], lp_ref[SUBLANES - 1:SUBLANES, :], ln_ref[0:1, :],
                   mupl_ref[...], munl_ref[...])

    gate_in = _sigmoid(lora[:, :LORA_GATE])
    mix = lora[:, LORA_GATE:]
    lane = lax.broadcasted_iota(jnp.int32, (1, LORA_MIX), 1)
    mix_in = jnp.where(lane < 2 * LORA_DECAY, jnp.tanh(mix), mix)
    g = _dot(gate_in.astype(BF16), g2_ref[...])
    wa = _dot(mix_in.astype(BF16), wmix_ref[...])

    ones = _head_ones(2 * LANES)
    kk = k * kk_scale_ref[...]
    kk = kk * lax.rsqrt(_head_sum(kk * kk, ones) + L2_EPS)

    r_o[...] = r
    kk_o[...] = kk
    v_o[...] = v
    g_o[...] = g
    rk = rk_ref[...]
    ka = ka_ref[...]
    bonus = jnp.zeros_like(r)
    for d in range(2):
        z = w0_ref[d:d + 1, :] + wa[:, d * w:(d + 1) * w]
        lw_o[d] = -DECAY_SCALE * _sigmoid(z)
        a = _sigmoid(a0_ref[d:d + 1, :] + wa[:, (2 + d) * w:(3 + d) * w])
        kd = k * (1.0 + (a - 1.0) * ka)
        kd_o[d] = kd
        bd_o[d] = kk * a
        bonus = bonus + _head_sum(r * kd * rk, ones) * v
    bonus_o[...] = bonus


def _rwkv_prepare(f_rkv, f_lora, p, rows):
    tm = min(256, rows.ctx_len)
    n_all = rows.n_all
    nblk = n_all // tm
    sub_per_blk = tm // SUBLANES
    n_sub = n_all // SUBLANES
    w = RWKV_WIDTH

    def prev_idx(i):
        return (jnp.maximum(i * sub_per_blk - 1, 0), 0)

    def next_idx(i):
        return (jnp.minimum((i + 1) * sub_per_blk, n_sub - 1), 0)

    full = lambda shape: pl.BlockSpec(shape, lambda i: (0,) * len(shape))
    row_spec = lambda width: pl.BlockSpec((tm, width), lambda i: (i, 0))
    dir_spec = pl.BlockSpec((2, tm, w), lambda i: (0, i, 0))
    kern = functools.partial(_rwkv_prep_kernel, n_lat_blocks=rows.n_lat // tm,
                             bps_lat=rows.seq // tm, bps_ctx=rows.ctx_len // tm)
    sds = jax.ShapeDtypeStruct
    return pl.pallas_call(
        kern,
        grid=(nblk,),
        in_specs=[
            row_spec(RKV_COLS), row_spec(LORA_COLS),
            pl.BlockSpec((SUBLANES, RKV_COLS), prev_idx), pl.BlockSpec((SUBLANES, LORA_COLS), prev_idx),
            pl.BlockSpec((SUBLANES, RKV_COLS), next_idx), pl.BlockSpec((SUBLANES, LORA_COLS), next_idx),
            full((1, RKV_COLS)), full((1, RKV_COLS)), full((1, LORA_COLS)), full((1, LORA_COLS)),
            full((LORA_GATE, w)), full((LORA_MIX, 4 * w)), full((2, w)), full((2, w)),
            full((1, w)), full((1, w)), full((1, w)),
        ],
        out_specs=[row_spec(w)] * 5 + [dir_spec] * 3,
        out_shape=[sds((n_all, w), F32)] * 5 + [sds((2, n_all, w), F32)] * 3,
        compiler_params=_params(("parallel",)),
        name="rwkv_prepare",
    )(f_rkv, f_lora, f_rkv, f_lora, f_rkv, f_lora,
      p['mupf'], p['munf'], p['mupl'], p['munl'],
      p['g2'], p['wmix'], p['w0'], p['a0'], p['k_k'], p['k_a'], p['r_k'])


def _scan_kernel(r_ref, kk_ref, v_ref, lw_ref, kd_ref, bd_ref, y_ref, h_ref,
                 lincl_ref, msd_ref, mso_ref, mincl_ref, eye_ref, hm_ref, hmb_ref,
                 *, chunk, heads, groups):
    c = chunk
    wd = heads * RWKV_HEAD
    step = pl.program_id(3)

    @pl.when(step == 0)
    def _():
        h_ref[...] = jnp.zeros_like(h_ref)
        sgn = 1 - 2 * pl.program_id(0)
        iota = lambda shape, ax: lax.broadcasted_iota(jnp.int32, shape, ax)
        rel_c = (iota((c, c), 0) - iota((c, c), 1)) * sgn
        lincl_ref[...] = jnp.where(rel_c >= 0, 1.0, 0.0).astype(BF16)
        ti = iota((c, wd), 0)
        si = iota((c, wd), 1) & (c - 1)
        rel = (ti - si) * sgn
        log2b = INV_BLOCK.bit_length() - 1
        same_blk = (ti >> log2b) == (si >> log2b)
        strict = jnp.where(rel > 0, 1.0, 0.0)
        msd_ref[...] = jnp.where(same_blk, strict, 0.0)
        mso_ref[...] = jnp.where(same_blk, 0.0, strict)
        mincl_ref[...] = jnp.where(rel >= 0, 1.0, 0.0)
        eye_ref[...] = jnp.where(ti == si, 1.0, 0.0)
        hm = jnp.where((iota((wd, wd), 0) >> 6) == (iota((wd, wd), 1) >> 6), 1.0, 0.0)
        hm_ref[...] = hm
        hmb_ref[...] = hm.astype(BF16)

    n_sub = r_ref.shape[0] // c
    d = pl.program_id(0)
    rows = [pl.ds(pl.multiple_of(jnp.where(d == 0, i, n_sub - 1 - i) * c, c), c) for i in range(n_sub)]
    lanes = [slice(q * wd, (q + 1) * wd) for q in range(groups)]
    ins = [[ref[rw, ln] for rw in rows for ln in lanes]
           for ref in (r_ref, kk_ref, v_ref, lw_ref, kd_ref, bd_ref)]
    m_c, n_c, rhat, oloc = _chunk_operators(*ins, lincl_ref, msd_ref, mso_ref, mincl_ref, eye_ref,
                                            hm_ref, hmb_ref, heads)
    h = [h_ref[q] for q in range(groups)]
    for i, rw in enumerate(rows):
        part = slice(i * groups, (i + 1) * groups)
        h, y = _advance(m_c[part], n_c[part], rhat[part], oloc[part], h, hmb_ref, heads)
        for q in range(groups):
            y_ref[rw, lanes[q]] = y[q]
    for q in range(groups):
        h_ref[q] = h[q]


def _advance(m_c, n_c, rhat, oloc, h, hmb_ref, heads):
    c = RWKV_HEAD
    hmb = hmb_ref[...]

    def stacked(zb):
        return jnp.concatenate([zb] * heads, axis=0) * hmb

    def step(mi, ri, hi):
        lhs = jnp.concatenate([mi, ri], axis=0).astype(BF16)
        return _dot(lhs, stacked(hi.astype(BF16)))

    res = [step(*a) for a in zip(m_c, rhat, h)]
    h_new = [z[:c] + ni for z, ni in zip(res, n_c)]
    y = [z[c:] + oi for z, oi in zip(res, oloc)]
    return h_new, y


def _chunk_operators(r, kk, v, lw, k, b, lincl_ref, msd_ref, mso_ref, mincl_ref, eye_ref,
                     hm_ref, hmb_ref, heads):
    c = RWKV_HEAD
    wd = heads * RWKV_HEAD
    hmb = hmb_ref[...]
    each = lambda f, *ls: [f(*a) for a in zip(*ls)]

    def stacked(z):
        zb = z.astype(BF16)
        return jnp.concatenate([zb] * heads, axis=0) * hmb

    def mm(x, y_stacked, dims=NN):
        return _dot(x.astype(BF16), y_stacked, dims)

    def head_blocks(full):
        z = full * hm_ref[...]
        out = z[0:c]
        for hh in range(1, heads):
            out = out + z[hh * c:(hh + 1) * c]
        return out

    cat0 = lambda *xs: jnp.concatenate(xs, axis=0)
    cat1 = lambda *xs: jnp.concatenate(xs, axis=1)
    mul = lambda x, y: x * y
    sub = lambda x, y: x - y
    add = lambda x, y: x + y
    left = lambda z: z[:, :wd]
    right = lambda z: z[:, wd:]
    top = lambda z: z[:c]
    bottom = lambda z: z[c:]
    eye = eye_ref[...]
    lincl = lincl_ref[...]
    g = each(lambda x: _mm_exact_rhs_left(lincl, x), lw)
    gt = each(lambda x: jnp.sum(x, axis=0, keepdims=True), lw)
    e_x = each(lambda gi, lwi: jnp.exp(gi - lwi), g, lw)
    e_g = each(jnp.exp, g)
    e_n = each(lambda gi: jnp.exp(-gi), g)
    e_c = each(lambda gti, gi: jnp.exp(gti - gi), gt, g)
    kt = each(mul, kk, e_x)
    rt = each(mul, r, e_g)
    bh = each(mul, b, e_n)
    kh = each(mul, k, e_n)
    kb = each(mul, k, e_c)
    bb = each(mul, b, e_c)

    gm = each(lambda kti, rti, bhi, khi: mm(cat0(kti, rti), cat0(stacked(bhi), stacked(khi)), NT),
              kt, rt, bh, kh)
    msd = msd_ref[...]
    mso = mso_ref[...]
    m_incl = mincl_ref[...]
    a_d = each(lambda z: z[:c, :wd] * msd, gm)
    a_o = each(lambda z: z[:c, :wd] * mso, gm)
    b_m = each(lambda z: z[:c, wd:] * (msd + mso), gm)
    e_m = each(lambda z: z[c:, :wd] * m_incl, gm)
    c_m = each(lambda z: z[c:, wd:] * m_incl, gm)

    mm_st = lambda x, y: mm(x, stacked(y))
    x1 = each(lambda z: -z, a_d)
    x2 = each(mm_st, x1, x1)
    s2 = each(stacked, x2)
    x4 = each(mm, x2, s2)
    s4 = each(stacked, x4)
    x8 = each(mm, x4, s4)
    dm = each(lambda z: eye - z, a_d)
    dm = each(lambda di, si: di + mm(di, si), dm, s2)
    dm = each(lambda di, si: di + mm(di, si), dm, s4)
    dm = each(lambda di, xi: di + mm_st(di, xi), dm, x8)
    n1 = each(mm_st, dm, a_o)
    n2 = each(mm_st, n1, n1)
    t1 = each(lambda z: eye - z, n1)
    t1 = each(lambda ti, ni: ti + mm_st(ti, ni), t1, n2)
    t_m = each(mm_st, t1, dm)

    bcv = each(lambda bi, ci, vi: mm(cat0(bi, ci), stacked(vi)), b_m, c_m, v)
    bv = each(top, bcv)
    cv = each(bottom, bcv)
    wu = each(lambda ti, kti, bvi: mm(ti, cat1(stacked(kti), stacked(bvi))), t_m, kt, bv)
    wt = each(left, wu)
    ut = each(right, wu)
    ew = each(lambda ei, wi, ui: mm(ei, cat1(stacked(wi), stacked(ui))), e_m, wt, ut)
    rhat = each(lambda ri, z: ri - z[:, :wd], rt, ew)
    oloc = each(lambda ci, z: ci - z[:, wd:], cv, ew)
    full_m = each(lambda bi, wi: _dot(bi.astype(BF16), wi.astype(BF16), TN), bb, wt)
    full_n = each(lambda ki, bi, vi, ui: _dot(cat0(ki, bi).astype(BF16), cat0(vi, -ui).astype(BF16), TN),
                  kb, bb, v, ut)
    m_c = each(lambda gti, fi: eye * jnp.exp(gti) - head_blocks(fi), gt, full_m)
    n_c = each(head_blocks, full_n)
    return m_c, n_c, rhat, oloc


def _mm_exact_rhs_left(a_exact, b):
    bh = b.astype(BF16)
    r1 = b - bh.astype(F32)
    bm = r1.astype(BF16)
    bl = (r1 - bm.astype(F32)).astype(BF16)
    return _dot(a_exact, bh) + (_dot(a_exact, bm) + _dot(a_exact, bl))


def _rwkv_scan(r, kk, v, lw, kd, bd, rows):
    c, heads = SCAN_CHUNK, SCAN_HEADS
    wd = heads * RWKV_HEAD
    nq = RWKV_WIDTH // wd
    rb = c * SCAN_SUBCHUNKS
    assert rows.ctx_len % rb == 0 and rows.seq % rb == 0
    nc_ctx = rows.ctx_len // rb
    nc_lat = rows.seq // rb
    lat_blocks = rows.n_lat // rb

    def chunk_idx(d, bi, st):
        in_ctx = st < nc_ctx
        t_ctx = jnp.where(d == 0, st, nc_ctx - 1 - st)
        sl = st - nc_ctx
        t_lat = jnp.where(d == 0, sl, nc_lat - 1 - sl)
        return jnp.where(in_ctx, lat_blocks + bi * nc_ctx + t_ctx, bi * nc_lat + t_lat)

    assert c == RWKV_HEAD
    groups = SCAN_GROUPS
    bw = wd * groups
    shared = pl.BlockSpec((rb, bw), lambda d, bi, q, st: (chunk_idx(d, bi, st), q))
    per_dir = pl.BlockSpec((None, rb, bw), lambda d, bi, q, st: (d, chunk_idx(d, bi, st), q))
    return pl.pallas_call(
        functools.partial(_scan_kernel, chunk=c, heads=heads, groups=groups),
        grid=(2, rows.batch, nq // groups, nc_ctx + nc_lat),
        in_specs=[shared, shared, shared, per_dir, per_dir, per_dir],
        out_specs=per_dir,
        out_shape=jax.ShapeDtypeStruct((2, rows.n_all, RWKV_WIDTH), F32),
        scratch_shapes=[
            pltpu.VMEM((groups, c, wd), F32),
            pltpu.VMEM((c, c), BF16),
            pltpu.VMEM((c, wd), F32),
            pltpu.VMEM((c, wd), F32),
            pltpu.VMEM((c, wd), F32),
            pltpu.VMEM((c, wd), F32),
            pltpu.VMEM((wd, wd), F32),
            pltpu.VMEM((wd, wd), BF16),
        ],
        compiler_params=_params(("parallel", "parallel", "parallel", "arbitrary")),
        name="rwkv_scan",
    )(r, kk, v, lw, kd, bd)


def _out_kernel(attn_ref, y_ref, bonus_ref, g_ref, lnw_ref, lnb_ref, x_ref, gate_ref,
                wa_ref, wr_ref, o_ref, rw_ref):
    @pl.when(pl.program_id(1) == 0)
    def _():
        ones = _head_ones(2 * LANES)
        y = y_ref[0] + y_ref[1]
        mean = _head_sum(y, ones) * (1.0 / RWKV_HEAD)
        yc = y - mean
        var = _head_sum(yc * yc, ones) * (1.0 / RWKV_HEAD)
        yn = yc * lax.rsqrt(var + GN_EPS) * lnw_ref[...] + lnb_ref[...]
        rw_ref[...] = ((yn + bonus_ref[...]) * g_ref[...]).astype(BF16)

    out = _dot(attn_ref[...], wa_ref[...]) + _dot(rw_ref[...], wr_ref[...])
    o_ref[...] = x_ref[...] + gate_ref[...] * out


def _mix_output(attn, y, bonus, g, ln_w, ln_b, x, mod, w_out, layer, rows, n_rows):
    d = D_MODEL
    tm = min(256, rows.tm)
    tn = d
    w = RWKV_WIDTH
    return pl.pallas_call(
        _out_kernel,
        grid=(n_rows // tm, d // tn),
        in_specs=[
            pl.BlockSpec((tm, Q_COLS), lambda i, j: (i, 0)),
            pl.BlockSpec((2, tm, w), lambda i, j: (0, i, 0)),
            pl.BlockSpec((tm, w), lambda i, j: (i, 0)),
            pl.BlockSpec((tm, w), lambda i, j: (i, 0)),
            pl.BlockSpec((1, w), lambda i, j: (0, 0)),
            pl.BlockSpec((1, w), lambda i, j: (0, 0)),
            pl.BlockSpec((tm, tn), lambda i, j: (i, j)),
            pl.BlockSpec((None, 1, tn), lambda i, j: (rows.mod_row(i, tm), 0, 2 * (d // tn) + j)),
            pl.BlockSpec((None, Q_COLS, tn), lambda i, j: (layer, 0, j)),
            pl.BlockSpec((None, w, tn), lambda i, j: (layer, 1, j)),
        ],
        out_specs=pl.BlockSpec((tm, tn), lambda i, j: (i, j)),
        out_shape=jax.ShapeDtypeStruct((n_rows, d), F32),
        scratch_shapes=[pltpu.VMEM((tm, w), BF16)],
        compiler_params=_params(("parallel", "arbitrary")),
        name="mix_output",
    )(attn, y, bonus, g, ln_w, ln_b, x, mod, w_out, w_out)


def _ffn_kernel(x_ref, g_ref, sc_ref, sh_ref, gate_ref, w1_ref, w3_ref, w2_ref, fg_ref,
                o_ref, h_ref, acc_ref, *, final):
    j = pl.program_id(1)

    @pl.when(j == 0)
    def _():
        h_ref[...] = _norm_mod(x_ref[...], g_ref[...], sc_ref[...], sh_ref[...]).astype(BF16)
        acc_ref[...] = jnp.zeros_like(acc_ref)

    h = h_ref[...]
    u = _dot(h, w1_ref[...])
    a = (u * _sigmoid(u)) * _dot(h, w3_ref[...])
    acc_ref[...] += _dot(a.astype(BF16), w2_ref[...])

    @pl.when(j == pl.num_programs(1) - 1)
    def _():
        y = x_ref[...] + gate_ref[...] * acc_ref[...]
        if final:
            ms = jnp.mean(y * y, axis=-1, keepdims=True)
            y = y * lax.rsqrt(ms + NORM_EPS) * fg_ref[...]
        o_ref[...] = y


def _ffn(x, g, mod, w1, w3, w2, layer, final_g, rows, n_rows, final):
    tm, d = rows.tm, D_MODEL
    tf = 512
    return pl.pallas_call(
        functools.partial(_ffn_kernel, final=final),
        grid=(n_rows // tm, D_FF // tf),
        in_specs=[
            pl.BlockSpec((tm, d), lambda i, j: (i, 0)),
            pl.BlockSpec((1, d), lambda i, j: (0, 0)),
            _mod_spec(rows, 4, 2),
            _mod_spec(rows, 3, 2),
            _mod_spec(rows, 5, 2),
            pl.BlockSpec((None, d, tf), lambda i, j: (layer, 0, j)),
            pl.BlockSpec((None, d, tf), lambda i, j: (layer, 0, j)),
            pl.BlockSpec((None, tf, d), lambda i, j: (layer, j, 0)),
            pl.BlockSpec((1, d), lambda i, j: (0, 0)),
        ],
        out_specs=pl.BlockSpec((tm, d), lambda i, j: (i, 0)),
        out_shape=jax.ShapeDtypeStruct((n_rows, d), F32),
        scratch_shapes=[pltpu.VMEM((tm, d), BF16), pltpu.VMEM((tm, d), F32)],
        compiler_params=_params(("parallel", "arbitrary")),
        name="ffn_final" if final else "ffn",
    )(x, g, mod, mod, mod, w1, w3, w2, final_g)


def _rope_tables(seq, tm):
    n_rows = seq // GRID_W
    row = jnp.broadcast_to(jnp.arange(n_rows)[:, None], (n_rows, GRID_W)).reshape(-1)
    col = jnp.broadcast_to(jnp.arange(GRID_W)[None, :], (n_rows, GRID_W)).reshape(-1)
    inv = ROPE_THETA ** (-jnp.arange(ROPE_FREQS, dtype=F32) / ROPE_FREQS)
    ang = jnp.concatenate([row[:, None].astype(F32) * inv, col[:, None].astype(F32) * inv], axis=-1)
    cos, sin = jnp.cos(ang), jnp.sin(ang)
    cos2 = jnp.concatenate([cos, cos], axis=-1)
    sin2 = jnp.concatenate([-sin, sin], axis=-1)
    cos2 = jnp.concatenate([cos2, jnp.ones((tm, HEAD_DIM), F32)], axis=0)
    sin2 = jnp.concatenate([sin2, jnp.zeros((tm, HEAD_DIM), F32)], axis=0)
    return cos2, sin2


def _mix_weights(w2, a2):
    w = RWKV_WIDTH
    out = jnp.zeros((LORA_MIX, 4 * w), F32)
    out = out.at[0:LORA_DECAY, 0:w].set(w2[0])
    out = out.at[LORA_DECAY:2 * LORA_DECAY, w:2 * w].set(w2[1])
    o = 2 * LORA_DECAY
    out = out.at[o:o + LORA_ICLR, 2 * w:3 * w].set(a2[0])
    out = out.at[o + LORA_ICLR:o + 2 * LORA_ICLR, 3 * w:4 * w].set(a2[1])
    return out


def kernel(x, c, ctx, c_ctx, norm1_g, norm2_g, ada_w, ada_b, w_in, q_gain, k_gain, mu_prev, mu_next,
           w0, w2, a0, a2, g2, k_k, k_a, r_k, ln_x_w, ln_x_b, w_out, ffn_w1, ffn_w3, ffn_w2, final_g):
    batch, seq, d = x.shape
    ctx_len = ctx.shape[1]
    depth = w_in.shape[0]
    rows = _Rows(batch, seq, ctx_len, min(512, batch * ctx_len))
    w = RWKV_WIDTH

    cond = jnp.concatenate([c, c_ctx[None, :], jnp.zeros((SUBLANES - batch - 1, d), F32)], axis=0)
    mod_all = _modulation(cond, ada_w, ada_b)
    mod_all = mod_all.reshape(depth, SUBLANES, 1, 6 * d)
    cos2, sin2 = _rope_tables(seq, rows.tm)

    w_main_b = w_in[:, :, :ATTN_COLS + RKV_COLS].astype(BF16)
    w_lora_b = w_in[:, :, ATTN_COLS + RKV_COLS:].astype(BF16)
    w_out_b = w_out.astype(BF16)
    w1_b = ffn_w1.astype(BF16)
    w3_b = ffn_w3.astype(BF16)
    w2_b = ffn_w2.astype(BF16)

    tok = jnp.concatenate([x.reshape(batch * seq, d), ctx.reshape(batch * ctx_len, d)], axis=0)
    row1 = lambda a: a.reshape(1, -1)
    for i in range(depth):
        last = i == depth - 1
        mod = mod_all[i]
        g1 = row1(norm1_g[i])
        q, k, vt = _attn_project(tok, g1, mod, w_main_b, i, row1(q_gain[i]), row1(k_gain[i]),
                                 cos2, sin2, rows)
        f_rkv = _project(tok, g1, mod, w_main_b, i, ATTN_COLS, RKV_COLS, rows, ATTN_COLS)
        f_lora = _project(tok, g1, mod, w_lora_b, i, 0, LORA_COLS, rows, LORA_COLS)
        attn = _attention(q, k, vt, rows, latent=True)
        if not last:
            attn = _attention(q, k, vt, rows, latent=False, out=attn)
        prep = {
            'mupf': row1(mu_prev[i, :RKV_COLS]), 'munf': row1(mu_next[i, :RKV_COLS]),
            'mupl': row1(mu_prev[i, RKV_COLS:]), 'munl': row1(mu_next[i, RKV_COLS:]),
            'g2': g2[i].astype(BF16), 'wmix': _mix_weights(w2[i], a2[i]).astype(BF16),
            'w0': w0[i], 'a0': a0[i],
            'k_k': row1(k_k[i]), 'k_a': row1(k_a[i]), 'r_k': row1(r_k[i]),
        }
        r_s, kk_s, v_s, g_s, bonus, lw, kd, bd = _rwkv_prepare(f_rkv, f_lora, prep, rows)
        y = _rwkv_scan(r_s, kk_s, v_s, lw, kd, bd, rows)
        n_rows = rows.n_lat if last else rows.n_all
        tok = _mix_output(attn, y, bonus, g_s, row1(ln_x_w[i]), row1(ln_x_b[i]), tok, mod,
                          w_out_b, i, rows, n_rows)
        tok = _ffn(tok, row1(norm2_g[i]), mod, w1_b, w3_b, w2_b, i, row1(final_g),
                   rows, n_rows, final=last)
    return tok.reshape(batch, seq, d)
```

```python
import functools

import jax
import jax.numpy as jnp
from jax import lax
from jax.experimental import pallas as pl
from jax.experimental.pallas import tpu as pltpu

F32 = jnp.float32
BF16 = jnp.bfloat16

D_MODEL = 2048
HEAD_DIM = 128
N_Q_HEADS = 8
N_KV_HEADS = 2
GQA_GROUP = N_Q_HEADS // N_KV_HEADS
Q_COLS = N_Q_HEADS * HEAD_DIM
KV_COLS = N_KV_HEADS * HEAD_DIM
ATTN_COLS = Q_COLS + 2 * KV_COLS
RWKV_WIDTH = 1024
RWKV_HEAD = 64
LORA_GATE = 256
LORA_DECAY = 96
LORA_ICLR = 96
LORA_COLS = LORA_GATE + 2 * LORA_DECAY + 2 * LORA_ICLR
LORA_MIX = 2 * LORA_DECAY + 2 * LORA_ICLR
RKV_COLS = 3 * RWKV_WIDTH
D_FF = 5632
GRID_W = 64
ROPE_THETA = 10000.0
ROPE_FREQS = HEAD_DIM // 4
ATTN_SCALE = HEAD_DIM ** -0.5
LOG2_E = 1.4426950408889634
NORM_EPS = 1e-6
GN_EPS = 64e-5
L2_EPS = 1e-12
DECAY_SCALE = 0.6065306597126334

LANES = 128
SUBLANES = 8
VMEM_LIMIT = 56 * 1024 * 1024

ATTN_KV_CHUNK = 1024
RWKV_IN_ROWS = 256
SCAN_CHUNK = 64
SCAN_HEADS = 4
INV_BLOCK = 16
SCAN_GROUPS = 4
SCAN_SUBCHUNKS = 2

NN = ((1,), (0,))
NT = ((1,), (1,))
TN = ((0,), (0,))


def _dot(a, b, dims=NN):
    return lax.dot_general(a, b, (dims, ((), ())), preferred_element_type=F32)


def _split(x):
    hi = x.astype(BF16)
    lo = (x - hi.astype(F32)).astype(BF16)
    return hi, lo


def _mm3(a, b, dims=NN):
    ah, al = a
    bh, bl = b
    return _dot(ah, bh, dims) + (_dot(ah, bl, dims) + _dot(al, bh, dims))


def _mm3f(a, b, dims=NN):
    return _mm3(_split(a), _split(b), dims)


def _mm_exact_rhs(a, b_exact, dims=NN):
    ah, al = _split(a)
    return _dot(ah, b_exact, dims) + _dot(al, b_exact, dims)


def _sigmoid(z):
    return 1.0 / (1.0 + jnp.exp(-z))


def _norm_mod(x, g, scale, shift):
    ms = jnp.mean(x * x, axis=-1, keepdims=True)
    y = x * lax.rsqrt(ms + NORM_EPS) * g
    return y * (1.0 + scale) + shift


def _head_ones(width):
    r = lax.broadcasted_iota(jnp.int32, (width, width), 0) >> 6
    c = lax.broadcasted_iota(jnp.int32, (width, width), 1) >> 6
    return jnp.where(r == c, 1.0, 0.0).astype(BF16)


def _head_sum(x, ones):
    w = ones.shape[0]
    xb = x.astype(BF16)
    parts = [_dot(xb[:, c:c + w], ones) for c in range(0, x.shape[1], w)]
    return jnp.concatenate(parts, axis=1)


def _params(sem):
    return pltpu.CompilerParams(dimension_semantics=sem, vmem_limit_bytes=VMEM_LIMIT)


def _mod_kernel(c_ref, w_ref, b_ref, o_ref):
    c = c_ref[...]
    s = c * _sigmoid(c)
    o_ref[...] = _mm3f(s, w_ref[...]) + b_ref[...]


def _modulation(cond, ada_w, ada_b):
    depth, d, n = ada_w.shape
    tn = 512
    rows = cond.shape[0]
    return pl.pallas_call(
        _mod_kernel,
        grid=(depth, n // tn),
        in_specs=[
            pl.BlockSpec((rows, d), lambda l, j: (0, 0)),
            pl.BlockSpec((None, d, tn), lambda l, j: (l, 0, j)),
            pl.BlockSpec((None, 1, tn), lambda l, j: (l, 0, j)),
        ],
        out_specs=pl.BlockSpec((None, rows, tn), lambda l, j: (l, 0, j)),
        out_shape=jax.ShapeDtypeStruct((depth, rows, n), F32),
        compiler_params=_params(("parallel", "parallel")),
        name="modulation",
    )(cond, ada_w, ada_b.reshape(depth, 1, n))


class _Rows:
    def __init__(self, batch, seq, ctx_len, tm):
        self.batch, self.seq, self.ctx_len, self.tm = batch, seq, ctx_len, tm
        self.n_lat = batch * seq
        self.n_ctx = batch * ctx_len
        self.n_all = self.n_lat + self.n_ctx
        assert seq % tm == 0 and self.n_ctx % tm == 0
        self.blocks_per_batch = seq // tm

    def mod_row(self, i, tm=None):
        per_batch = self.blocks_per_batch if tm is None else self.seq // tm
        return jnp.minimum(i // per_batch, self.batch)


def _mod_spec(rows, which, ngrid):
    d = D_MODEL
    if ngrid == 1:
        return pl.BlockSpec((None, 1, d), lambda i: (rows.mod_row(i), 0, which))
    return pl.BlockSpec((None, 1, d), lambda i, j: (rows.mod_row(i), 0, which))


def _attn_proj_kernel(x_ref, g_ref, sc_ref, sh_ref, w_ref, qg_ref, kg_ref, cos_ref, sin_ref,
                      q_ref, k_ref, vt_ref):
    h = _norm_mod(x_ref[...], g_ref[...], sc_ref[...], sh_ref[...]).astype(BF16)
    f = _dot(h, w_ref[...])
    cos = cos_ref[...]
    sin = sin_ref[...]

    def head(xh, gain, scale):
        ms = jnp.mean(xh * xh, axis=-1, keepdims=True)
        y = xh * lax.rsqrt(ms + NORM_EPS) * gain
        y = y * cos + pltpu.roll(y, HEAD_DIM // 2, 1) * sin
        return (y * scale).astype(BF16)

    qg = qg_ref[...]
    kg = kg_ref[...]
    for hq in range(N_Q_HEADS):
        c = hq * HEAD_DIM
        q_ref[:, c:c + HEAD_DIM] = head(f[:, c:c + HEAD_DIM], qg, ATTN_SCALE * LOG2_E)
    for hk in range(N_KV_HEADS):
        c = hk * HEAD_DIM
        k_ref[:, c:c + HEAD_DIM] = head(f[:, Q_COLS + c:Q_COLS + c + HEAD_DIM], kg, 1.0)
    vt_ref[...] = f[:, Q_COLS + KV_COLS:ATTN_COLS].T.astype(BF16)


def _attn_project(x, g, mod, w, layer, q_gain, k_gain, cos2, sin2, rows):
    tm, d = rows.tm, D_MODEL
    bpb = rows.blocks_per_batch

    def rope_idx(i):
        return (jnp.where(i < rows.n_lat // tm, i % bpb, bpb), 0)

    return pl.pallas_call(
        _attn_proj_kernel,
        grid=(rows.n_all // tm,),
        in_specs=[
            pl.BlockSpec((tm, d), lambda i: (i, 0)),
            pl.BlockSpec((1, d), lambda i: (0, 0)),
            _mod_spec(rows, 1, 1),
            _mod_spec(rows, 0, 1),
            pl.BlockSpec((None, d, ATTN_COLS), lambda i: (layer, 0, 0)),
            pl.BlockSpec((1, HEAD_DIM), lambda i: (0, 0)),
            pl.BlockSpec((1, HEAD_DIM), lambda i: (0, 0)),
            pl.BlockSpec((tm, HEAD_DIM), rope_idx),
            pl.BlockSpec((tm, HEAD_DIM), rope_idx),
        ],
        out_specs=[
            pl.BlockSpec((tm, Q_COLS), lambda i: (i, 0)),
            pl.BlockSpec((tm, KV_COLS), lambda i: (i, 0)),
            pl.BlockSpec((KV_COLS, tm), lambda i: (0, i)),
        ],
        out_shape=[
            jax.ShapeDtypeStruct((rows.n_all, Q_COLS), BF16),
            jax.ShapeDtypeStruct((rows.n_all, KV_COLS), BF16),
            jax.ShapeDtypeStruct((KV_COLS, rows.n_all), BF16),
        ],
        compiler_params=_params(("parallel",)),
        name="attn_proj",
    )(x, g, mod, mod, w, q_gain, k_gain, cos2, sin2)


def _attn_kernel(q_ref, *refs, n_kv, kv_chunk):
    k_refs = refs[:n_kv]
    vt_refs = refs[n_kv:2 * n_kv]
    o_ref = refs[-1]
    tq = q_ref.shape[0]
    heads = range(GQA_GROUP)
    each = lambda f, *ls: [f(*a) for a in zip(*ls)]
    q = [q_ref[:, h * HEAD_DIM:(h + 1) * HEAD_DIM] for h in heads]
    m = [jnp.full((1, tq), -jnp.inf, F32) for _ in heads]
    l = [jnp.zeros((1, tq), F32) for _ in heads]
    acc = [jnp.zeros((HEAD_DIM, tq), F32) for _ in heads]
    for k_ref, vt_ref in zip(k_refs, vt_refs):
        n = k_ref.shape[0]
        ck = min(kv_chunk, n)
        for c0 in range(0, n, ck):
            kc = k_ref[c0:c0 + ck, :]
            vtc = vt_ref[:, c0:c0 + ck]
            st = each(lambda qh: _dot(kc, qh, NT), q)
            m_new = each(lambda mh, sh: jnp.maximum(mh, jnp.max(sh, axis=0, keepdims=True)), m, st)
            alpha = each(lambda mh, mn: jnp.exp2(mh - mn), m, m_new)
            p = each(lambda sh, mn: jnp.exp2(sh - mn), st, m_new)
            l = each(lambda ah, lh, ph: ah * lh + jnp.sum(ph, axis=0, keepdims=True), alpha, l, p)
            acc = each(lambda ah, ch, ph: ah * ch + _dot(vtc, ph.astype(BF16)), alpha, acc, p)
            m = m_new
    for h in heads:
        o_ref[:, h * HEAD_DIM:(h + 1) * HEAD_DIM] = (acc[h] / l[h]).T.astype(o_ref.dtype)


def _attention(q, k, v, rows, latent, out=None):
    b, s, cl = rows.batch, rows.seq, rows.ctx_len
    gw = GQA_GROUP * HEAD_DIM
    ctx_blk0 = rows.n_lat // cl
    if latent:
        tq = min(256, s)
        nq = s // tq
        q_spec = pl.BlockSpec((tq, gw), lambda bi, hi, qi: (bi * nq + qi, hi))
        k_specs = [pl.BlockSpec((s, HEAD_DIM), lambda bi, hi, qi: (bi, hi)),
                   pl.BlockSpec((cl, HEAD_DIM), lambda bi, hi, qi: (ctx_blk0 + bi, hi))]
        vt_specs = [pl.BlockSpec((HEAD_DIM, s), lambda bi, hi, qi: (hi, bi)),
                    pl.BlockSpec((HEAD_DIM, cl), lambda bi, hi, qi: (hi, ctx_blk0 + bi))]
        name = "attention_latent"
    else:
        tq = cl
        nq = 1
        q_spec = pl.BlockSpec((tq, gw), lambda bi, hi, qi: (ctx_blk0 + bi, hi))
        k_specs = [pl.BlockSpec((cl, HEAD_DIM), lambda bi, hi, qi: (ctx_blk0 + bi, hi))]
        vt_specs = [pl.BlockSpec((HEAD_DIM, cl), lambda bi, hi, qi: (hi, ctx_blk0 + bi))]
        name = "attention_context"
    n_kv = len(k_specs)
    in_specs = [q_spec] + k_specs + vt_specs
    args = [q] + [k] * n_kv + [v] * n_kv
    aliases = {}
    if out is not None:
        aliases = {len(args): 0}
        in_specs.append(pl.BlockSpec(memory_space=pl.ANY))
        args.append(out)
    return pl.pallas_call(
        functools.partial(_attn_kernel, n_kv=n_kv, kv_chunk=ATTN_KV_CHUNK),
        grid=(b, N_KV_HEADS, nq),
        in_specs=in_specs,
        out_specs=q_spec,
        out_shape=jax.ShapeDtypeStruct((rows.n_all, Q_COLS), BF16),
        input_output_aliases=aliases,
        compiler_params=_params(("parallel", "parallel", "arbitrary")),
        name=name,
    )(*args)


def _rwkv_in_kernel(x_ref, xp_ref, xn_ref, g_ref, sc_ref, sh_ref, w_ref, mup_ref, mun_ref,
                    g2_ref, wmix_ref, w0_ref, a0_ref, kk_scale_ref, ka_ref, rk_ref,
                    r_o, kk_o, v_o, g_o, bonus_o, lw_o, kd_o, bd_o,
                    *, n_lat_blocks, bps_lat, bps_ctx):
    i = pl.program_id(0)
    tm = x_ref.shape[0]
    s = SUBLANES
    is_lat = i < n_lat_blocks
    j = jnp.where(is_lat, i, i - n_lat_blocks)
    bps = jnp.where(is_lat, bps_lat, bps_ctx)
    pos = j % bps
    keep_prev = jnp.where(pos == 0, 0.0, 1.0)
    keep_next = jnp.where(pos == bps - 1, 0.0, 1.0)
    row = lax.broadcasted_iota(jnp.int32, (s, 1), 0)
    first_row = jnp.where(row == 0, keep_prev, 1.0)
    last_row = jnp.where(row == s - 1, keep_next, 1.0)

    norm = lambda ref: _norm_mod(ref[...], g_ref[...], sc_ref[...], sh_ref[...]).astype(BF16)
    h_ext = jnp.concatenate([norm(xp_ref), norm(x_ref), norm(xn_ref)], axis=0)
    f = _dot(h_ext, w_ref[...])
    n_ext = tm + 2 * s

    def shifted(c):
        x = f[:, c]
        prev = pltpu.roll(x, 1, 0)[s:s + tm]
        nxt = pltpu.roll(x, n_ext - 1, 0)[s:s + tm]
        prev = jnp.concatenate([prev[:s] * first_row, prev[s:]], axis=0)
        nxt = jnp.concatenate([nxt[:tm - s], nxt[tm - s:] * last_row], axis=0)
        mu_prev = mup_ref[:, c]
        mu_next = mun_ref[:, c]
        return (1.0 - mu_prev - mu_next) * x[s:s + tm] + mu_prev * prev + mu_next * nxt

    w = RWKV_WIDTH
    r = shifted(slice(0, w))
    k = shifted(slice(w, 2 * w))
    v = shifted(slice(2 * w, 3 * w))
    lora = shifted(slice(3 * w, 3 * w + LORA_COLS))

    gate_in = _sigmoid(lora[:, :LORA_GATE])
    mix = lora[:, LORA_GATE:]
    lane = lax.broadcasted_iota(jnp.int32, (1, LORA_MIX), 1)
    mix_in = jnp.where(lane < 2 * LORA_DECAY, jnp.tanh(mix), mix)
    g = _dot(gate_in.astype(BF16), g2_ref[...])
    wa = _dot(mix_in.astype(BF16), wmix_ref[...])

    ones = _head_ones(2 * LANES)
    kk = k * kk_scale_ref[...]
    kk = kk * lax.rsqrt(_head_sum(kk * kk, ones) + L2_EPS)

    r_o[...] = r
    kk_o[...] = kk
    v_o[...] = v
    g_o[...] = g
    rk = rk_ref[...]
    ka = ka_ref[...]
    bonus = jnp.zeros_like(r)
    for d in range(2):
        z = w0_ref[d:d + 1, :] + wa[:, d * w:(d + 1) * w]
        lw_o[d] = -DECAY_SCALE * _sigmoid(z)
        a = _sigmoid(a0_ref[d:d + 1, :] + wa[:, (2 + d) * w:(3 + d) * w])
        kd = k * (1.0 + (a - 1.0) * ka)
        kd_o[d] = kd
        bd_o[d] = kk * a
        bonus = bonus + _head_sum(r * kd * rk, ones) * v
    bonus_o[...] = bonus


def _rwkv_in(x, g, mod, w_all, layer, p, rows):
    tm = min(RWKV_IN_ROWS, rows.ctx_len)
    d = D_MODEL
    n_all = rows.n_all
    nblk = n_all // tm
    sub_per_blk = tm // SUBLANES
    n_sub = n_all // SUBLANES
    w = RWKV_WIDTH
    cols = RKV_COLS + LORA_COLS

    def prev_idx(i):
        return (jnp.maximum(i * sub_per_blk - 1, 0), 0)

    def next_idx(i):
        return (jnp.minimum((i + 1) * sub_per_blk, n_sub - 1), 0)

    once = pl.Buffered(1)
    full = lambda shape: pl.BlockSpec(shape, lambda i: (0,) * len(shape), pipeline_mode=once)
    mod_spec = lambda which: pl.BlockSpec((None, 1, d), lambda i: (rows.mod_row(i, tm), 0, which))
    row_spec = lambda width: pl.BlockSpec((tm, width), lambda i: (i, 0))
    dir_spec = pl.BlockSpec((2, tm, w), lambda i: (0, i, 0))
    kern = functools.partial(_rwkv_in_kernel, n_lat_blocks=rows.n_lat // tm,
                             bps_lat=rows.seq // tm, bps_ctx=rows.ctx_len // tm)
    sds = jax.ShapeDtypeStruct
    return pl.pallas_call(
        kern,
        grid=(nblk,),
        in_specs=[
            row_spec(d), pl.BlockSpec((SUBLANES, d), prev_idx), pl.BlockSpec((SUBLANES, d), next_idx),
            full((1, d)), mod_spec(1), mod_spec(0),
            pl.BlockSpec((None, d, cols), lambda i: (layer, 0, 0), pipeline_mode=once),
            full((1, cols)), full((1, cols)),
            full((LORA_GATE, w)), full((LORA_MIX, 4 * w)), full((2, w)), full((2, w)),
            full((1, w)), full((1, w)), full((1, w)),
        ],
        out_specs=[row_spec(w)] * 5 + [dir_spec] * 3,
        out_shape=[sds((n_all, w), F32)] * 5 + [sds((2, n_all, w), F32)] * 3,
        compiler_params=_params(("parallel",)),
        name="rwkv_in",
    )(x, x, x, g, mod, mod, w_all, p['mu_prev'], p['mu_next'],
      p['g2'], p['wmix'], p['w0'], p['a0'], p['k_k'], p['k_a'], p['r_k'])


def _scan_kernel(r_ref, kk_ref, v_ref, lw_ref, kd_ref, bd_ref, y_ref, h_ref,
                 lincl_ref, msd_ref, mso_ref, mincl_ref, eye_ref, hm_ref, hmb_ref,
                 *, chunk, heads, groups):
    c = chunk
    wd = heads * RWKV_HEAD
    step = pl.program_id(3)

    @pl.when(step == 0)
    def _():
        h_ref[...] = jnp.zeros_like(h_ref)
        sgn = 1 - 2 * pl.program_id(0)
        iota = lambda shape, ax: lax.broadcasted_iota(jnp.int32, shape, ax)
        rel_c = (iota((c, c), 0) - iota((c, c), 1)) * sgn
        lincl_ref[...] = jnp.where(rel_c >= 0, 1.0, 0.0).astype(BF16)
        ti = iota((c, wd), 0)
        si = iota((c, wd), 1) & (c - 1)
        rel = (ti - si) * sgn
        log2b = INV_BLOCK.bit_length() - 1
        same_blk = (ti >> log2b) == (si >> log2b)
        strict = jnp.where(rel > 0, 1.0, 0.0)
        msd_ref[...] = jnp.where(same_blk, strict, 0.0)
        mso_ref[...] = jnp.where(same_blk, 0.0, strict)
        mincl_ref[...] = jnp.where(rel >= 0, 1.0, 0.0)
        eye_ref[...] = jnp.where(ti == si, 1.0, 0.0)
        hm = jnp.where((iota((wd, wd), 0) >> 6) == (iota((wd, wd), 1) >> 6), 1.0, 0.0)
        hm_ref[...] = hm
        hmb_ref[...] = hm.astype(BF16)

    n_sub = r_ref.shape[0] // c
    d = pl.program_id(0)
    rows = [pl.ds(pl.multiple_of(jnp.where(d == 0, i, n_sub - 1 - i) * c, c), c) for i in range(n_sub)]
    lanes = [slice(q * wd, (q + 1) * wd) for q in range(groups)]
    ins = [[ref[rw, ln] for rw in rows for ln in lanes]
           for ref in (r_ref, kk_ref, v_ref, lw_ref, kd_ref, bd_ref)]
    m_c, n_c, rhat, oloc = _chunk_operators(*ins, lincl_ref, msd_ref, mso_ref, mincl_ref, eye_ref,
                                            hm_ref, hmb_ref, heads)
    h = [h_ref[q] for q in range(groups)]
    for i, rw in enumerate(rows):
        part = slice(i * groups, (i + 1) * groups)
        h, y = _advance(m_c[part], n_c[part], rhat[part], oloc[part], h, hmb_ref, heads)
        for q in range(groups):
            y_ref[rw, lanes[q]] = y[q]
    for q in range(groups):
        h_ref[q] = h[q]


def _advance(m_c, n_c, rhat, oloc, h, hmb_ref, heads):
    c = RWKV_HEAD
    hmb = hmb_ref[...]

    def stacked(zb):
        return jnp.concatenate([zb] * heads, axis=0) * hmb

    def step(mi, ri, hi):
        lhs = jnp.concatenate([mi, ri], axis=0).astype(BF16)
        return _dot(lhs, stacked(hi.astype(BF16)))

    res = [step(*a) for a in zip(m_c, rhat, h)]
    h_new = [z[:c] + ni for z, ni in zip(res, n_c)]
    y = [z[c:] + oi for z, oi in zip(res, oloc)]
    return h_new, y


def _chunk_operators(r, kk, v, lw, k, b, lincl_ref, msd_ref, mso_ref, mincl_ref, eye_ref,
                     hm_ref, hmb_ref, heads):
    c = RWKV_HEAD
    wd = heads * RWKV_HEAD
    hmb = hmb_ref[...]
    each = lambda f, *ls: [f(*a) for a in zip(*ls)]

    def stacked(z):
        zb = z.astype(BF16)
        return jnp.concatenate([zb] * heads, axis=0) * hmb

    def mm(x, y_stacked, dims=NN):
        return _dot(x.astype(BF16), y_stacked, dims)

    def head_blocks(full):
        z = full * hm_ref[...]
        out = z[0:c]
        for hh in range(1, heads):
            out = out + z[hh * c:(hh + 1) * c]
        return out

    cat0 = lambda *xs: jnp.concatenate(xs, axis=0)
    cat1 = lambda *xs: jnp.concatenate(xs, axis=1)
    mul = lambda x, y: x * y
    sub = lambda x, y: x - y
    add = lambda x, y: x + y
    left = lambda z: z[:, :wd]
    right = lambda z: z[:, wd:]
    top = lambda z: z[:c]
    bottom = lambda z: z[c:]
    eye = eye_ref[...]
    lincl = lincl_ref[...]
    g = each(lambda x: _mm_exact_rhs_left(lincl, x), lw)
    gt = each(lambda x: jnp.sum(x, axis=0, keepdims=True), lw)
    e_x = each(lambda gi, lwi: jnp.exp(gi - lwi), g, lw)
    e_g = each(jnp.exp, g)
    e_n = each(lambda gi: jnp.exp(-gi), g)
    e_c = each(lambda gti, gi: jnp.exp(gti - gi), gt, g)
    kt = each(mul, kk, e_x)
    rt = each(mul, r, e_g)
    bh = each(mul, b, e_n)
    kh = each(mul, k, e_n)
    kb = each(mul, k, e_c)
    bb = each(mul, b, e_c)

    gm = each(lambda kti, rti, bhi, khi: mm(cat0(kti, rti), cat0(stacked(bhi), stacked(khi)), NT),
              kt, rt, bh, kh)
    msd = msd_ref[...]
    mso = mso_ref[...]
    m_incl = mincl_ref[...]
    a_d = each(lambda z: z[:c, :wd] * msd, gm)
    a_o = each(lambda z: z[:c, :wd] * mso, gm)
    b_m = each(lambda z: z[:c, wd:] * (msd + mso), gm)
    e_m = each(lambda z: z[c:, :wd] * m_incl, gm)
    c_m = each(lambda z: z[c:, wd:] * m_incl, gm)

    mm_st = lambda x, y: mm(x, stacked(y))
    x1 = each(lambda z: -z, a_d)
    x2 = each(mm_st, x1, x1)
    s2 = each(stacked, x2)
    x4 = each(mm, x2, s2)
    s4 = each(stacked, x4)
    x8 = each(mm, x4, s4)
    dm = each(lambda z: eye - z, a_d)
    dm = each(lambda di, si: di + mm(di, si), dm, s2)
    dm = each(lambda di, si: di + mm(di, si), dm, s4)
    dm = each(lambda di, xi: di + mm_st(di, xi), dm, x8)
    n1 = each(mm_st, dm, a_o)
    n2 = each(mm_st, n1, n1)
    t1 = each(lambda z: eye - z, n1)
    t1 = each(lambda ti, ni: ti + mm_st(ti, ni), t1, n2)
    t_m = each(mm_st, t1, dm)

    bcv = each(lambda bi, ci, vi: mm(cat0(bi, ci), stacked(vi)), b_m, c_m, v)
    bv = each(top, bcv)
    cv = each(bottom, bcv)
    wu = each(lambda ti, kti, bvi: mm(ti, cat1(stacked(kti), stacked(bvi))), t_m, kt, bv)
    wt = each(left, wu)
    ut = each(right, wu)
    ew = each(lambda ei, wi, ui: mm(ei, cat1(stacked(wi), stacked(ui))), e_m, wt, ut)
    rhat = each(lambda ri, z: ri - z[:, :wd], rt, ew)
    oloc = each(lambda ci, z: ci - z[:, wd:], cv, ew)
    full_m = each(lambda bi, wi: _dot(bi.astype(BF16), wi.astype(BF16), TN), bb, wt)
    full_n = each(lambda ki, bi, vi, ui: _dot(cat0(ki, bi).astype(BF16), cat0(vi, -ui).astype(BF16), TN),
                  kb, bb, v, ut)
    m_c = each(lambda gti, fi: eye * jnp.exp(gti) - head_blocks(fi), gt, full_m)
    n_c = each(head_blocks, full_n)
    return m_c, n_c, rhat, oloc


def _mm_exact_rhs_left(a_exact, b):
    bh = b.astype(BF16)
    r1 = b - bh.astype(F32)
    bm = r1.astype(BF16)
    bl = (r1 - bm.astype(F32)).astype(BF16)
    return _dot(a_exact, bh) + (_dot(a_exact, bm) + _dot(a_exact, bl))


def _rwkv_scan(r, kk, v, lw, kd, bd, rows):
    c, heads = SCAN_CHUNK, SCAN_HEADS
    wd = heads * RWKV_HEAD
    nq = RWKV_WIDTH // wd
    rb = c * SCAN_SUBCHUNKS
    assert rows.ctx_len % rb == 0 and rows.seq % rb == 0
    nc_ctx = rows.ctx_len // rb
    nc_lat = rows.seq // rb
    lat_blocks = rows.n_lat // rb

    def chunk_idx(d, bi, st):
        in_ctx = st < nc_ctx
        t_ctx = jnp.where(d == 0, st, nc_ctx - 1 - st)
        sl = st - nc_ctx
        t_lat = jnp.where(d == 0, sl, nc_lat - 1 - sl)
        return jnp.where(in_ctx, lat_blocks + bi * nc_ctx + t_ctx, bi * nc_lat + t_lat)

    assert c == RWKV_HEAD
    groups = SCAN_GROUPS
    bw = wd * groups
    shared = pl.BlockSpec((rb, bw), lambda d, bi, q, st: (chunk_idx(d, bi, st), q))
    per_dir = pl.BlockSpec((None, rb, bw), lambda d, bi, q, st: (d, chunk_idx(d, bi, st), q))
    return pl.pallas_call(
        functools.partial(_scan_kernel, chunk=c, heads=heads, groups=groups),
        grid=(2, rows.batch, nq // groups, nc_ctx + nc_lat),
        in_specs=[shared, shared, shared, per_dir, per_dir, per_dir],
        out_specs=per_dir,
        out_shape=jax.ShapeDtypeStruct((2, rows.n_all, RWKV_WIDTH), F32),
        scratch_shapes=[
            pltpu.VMEM((groups, c, wd), F32),
            pltpu.VMEM((c, c), BF16),
            pltpu.VMEM((c, wd), F32),
            pltpu.VMEM((c, wd), F32),
            pltpu.VMEM((c, wd), F32),
            pltpu.VMEM((c, wd), F32),
            pltpu.VMEM((wd, wd), F32),
            pltpu.VMEM((wd, wd), BF16),
        ],
        compiler_params=_params(("parallel", "parallel", "parallel", "arbitrary")),
        name="rwkv_scan",
    )(r, kk, v, lw, kd, bd)


def _out_kernel(attn_ref, y_ref, bonus_ref, g_ref, lnw_ref, lnb_ref, x_ref, gate_ref,
                wa_ref, wr_ref, o_ref, rw_ref):
    @pl.when(pl.program_id(1) == 0)
    def _():
        ones = _head_ones(2 * LANES)
        y = y_ref[0] + y_ref[1]
        mean = _head_sum(y, ones) * (1.0 / RWKV_HEAD)
        yc = y - mean
        var = _head_sum(yc * yc, ones) * (1.0 / RWKV_HEAD)
        yn = yc * lax.rsqrt(var + GN_EPS) * lnw_ref[...] + lnb_ref[...]
        rw_ref[...] = ((yn + bonus_ref[...]) * g_ref[...]).astype(BF16)

    out = _dot(attn_ref[...], wa_ref[...]) + _dot(rw_ref[...], wr_ref[...])
    o_ref[...] = x_ref[...] + gate_ref[...] * out


def _mix_output(attn, y, bonus, g, ln_w, ln_b, x, mod, w_out, layer, rows, n_rows):
    d = D_MODEL
    tm = min(256, rows.tm)
    tn = d
    w = RWKV_WIDTH
    return pl.pallas_call(
        _out_kernel,
        grid=(n_rows // tm, d // tn),
        in_specs=[
            pl.BlockSpec((tm, Q_COLS), lambda i, j: (i, 0)),
            pl.BlockSpec((2, tm, w), lambda i, j: (0, i, 0)),
            pl.BlockSpec((tm, w), lambda i, j: (i, 0)),
            pl.BlockSpec((tm, w), lambda i, j: (i, 0)),
            pl.BlockSpec((1, w), lambda i, j: (0, 0)),
            pl.BlockSpec((1, w), lambda i, j: (0, 0)),
            pl.BlockSpec((tm, tn), lambda i, j: (i, j)),
            pl.BlockSpec((None, 1, tn), lambda i, j: (rows.mod_row(i, tm), 0, 2 * (d // tn) + j)),
            pl.BlockSpec((None, Q_COLS, tn), lambda i, j: (layer, 0, j)),
            pl.BlockSpec((None, w, tn), lambda i, j: (layer, 1, j)),
        ],
        out_specs=pl.BlockSpec((tm, tn), lambda i, j: (i, j)),
        out_shape=jax.ShapeDtypeStruct((n_rows, d), F32),
        scratch_shapes=[pltpu.VMEM((tm, w), BF16)],
        compiler_params=_params(("parallel", "arbitrary")),
        name="mix_output",
    )(attn, y, bonus, g, ln_w, ln_b, x, mod, w_out, w_out)


def _ffn_kernel(x_ref, g_ref, sc_ref, sh_ref, gate_ref, w1_ref, w3_ref, w2_ref, fg_ref,
                o_ref, h_ref, acc_ref, *, final):
    j = pl.program_id(1)

    @pl.when(j == 0)
    def _():
        h_ref[...] = _norm_mod(x_ref[...], g_ref[...], sc_ref[...], sh_ref[...]).astype(BF16)
        acc_ref[...] = jnp.zeros_like(acc_ref)

    h = h_ref[...]
    u = _dot(h, w1_ref[...])
    a = (u * _sigmoid(u)) * _dot(h, w3_ref[...])
    acc_ref[...] += _dot(a.astype(BF16), w2_ref[...])

    @pl.when(j == pl.num_programs(1) - 1)
    def _():
        y = x_ref[...] + gate_ref[...] * acc_ref[...]
        if final:
            ms = jnp.mean(y * y, axis=-1, keepdims=True)
            y = y * lax.rsqrt(ms + NORM_EPS) * fg_ref[...]
        o_ref[...] = y


def _ffn(x, g, mod, w1, w3, w2, layer, final_g, rows, n_rows, final):
    tm, d = rows.tm, D_MODEL
    tf = 512
    return pl.pallas_call(
        functools.partial(_ffn_kernel, final=final),
        grid=(n_rows // tm, D_FF // tf),
        in_specs=[
            pl.BlockSpec((tm, d), lambda i, j: (i, 0)),
            pl.BlockSpec((1, d), lambda i, j: (0, 0)),
            _mod_spec(rows, 4, 2),
            _mod_spec(rows, 3, 2),
            _mod_spec(rows, 5, 2),
            pl.BlockSpec((None, d, tf), lambda i, j: (layer, 0, j)),
            pl.BlockSpec((None, d, tf), lambda i, j: (layer, 0, j)),
            pl.BlockSpec((None, tf, d), lambda i, j: (layer, j, 0)),
            pl.BlockSpec((1, d), lambda i, j: (0, 0)),
        ],
        out_specs=pl.BlockSpec((tm, d), lambda i, j: (i, 0)),
        out_shape=jax.ShapeDtypeStruct((n_rows, d), F32),
        scratch_shapes=[pltpu.VMEM((tm, d), BF16), pltpu.VMEM((tm, d), F32)],
        compiler_params=_params(("parallel", "arbitrary")),
        name="ffn_final" if final else "ffn",
    )(x, g, mod, mod, mod, w1, w3, w2, final_g)


def _rope_tables(seq, tm):
    n_rows = seq // GRID_W
    row = jnp.broadcast_to(jnp.arange(n_rows)[:, None], (n_rows, GRID_W)).reshape(-1)
    col = jnp.broadcast_to(jnp.arange(GRID_W)[None, :], (n_rows, GRID_W)).reshape(-1)
    inv = ROPE_THETA ** (-jnp.arange(ROPE_FREQS, dtype=F32) / ROPE_FREQS)
    ang = jnp.concatenate([row[:, None].astype(F32) * inv, col[:, None].astype(F32) * inv], axis=-1)
    cos, sin = jnp.cos(ang), jnp.sin(ang)
    cos2 = jnp.concatenate([cos, cos], axis=-1)
    sin2 = jnp.concatenate([-sin, sin], axis=-1)
    cos2 = jnp.concatenate([cos2, jnp.ones((tm, HEAD_DIM), F32)], axis=0)
    sin2 = jnp.concatenate([sin2, jnp.zeros((tm, HEAD_DIM), F32)], axis=0)
    return cos2, sin2


def _mix_weights(w2, a2):
    w = RWKV_WIDTH
    out = jnp.zeros((LORA_MIX, 4 * w), F32)
    out = out.at[0:LORA_DECAY, 0:w].set(w2[0])
    out = out.at[LORA_DECAY:2 * LORA_DECAY, w:2 * w].set(w2[1])
    o = 2 * LORA_DECAY
    out = out.at[o:o + LORA_ICLR, 2 * w:3 * w].set(a2[0])
    out = out.at[o + LORA_ICLR:o + 2 * LORA_ICLR, 3 * w:4 * w].set(a2[1])
    return out


def kernel(x, c, ctx, c_ctx, norm1_g, norm2_g, ada_w, ada_b, w_in, q_gain, k_gain, mu_prev, mu_next,
           w0, w2, a0, a2, g2, k_k, k_a, r_k, ln_x_w, ln_x_b, w_out, ffn_w1, ffn_w3, ffn_w2, final_g):
    batch, seq, d = x.shape
    ctx_len = ctx.shape[1]
    depth = w_in.shape[0]
    rows = _Rows(batch, seq, ctx_len, min(512, batch * ctx_len))
    w = RWKV_WIDTH

    cond = jnp.concatenate([c, c_ctx[None, :], jnp.zeros((SUBLANES - batch - 1, d), F32)], axis=0)
    mod_all = _modulation(cond, ada_w, ada_b)
    mod_all = mod_all.reshape(depth, SUBLANES, 1, 6 * d)
    cos2, sin2 = _rope_tables(seq, rows.tm)

    w_attn_b = w_in[:, :, :ATTN_COLS].astype(BF16)
    w_rwkv_b = w_in[:, :, ATTN_COLS:].astype(BF16)
    w_out_b = w_out.astype(BF16)
    w1_b = ffn_w1.astype(BF16)
    w3_b = ffn_w3.astype(BF16)
    w2_b = ffn_w2.astype(BF16)

    tok = jnp.concatenate([x.reshape(batch * seq, d), ctx.reshape(batch * ctx_len, d)], axis=0)
    row1 = lambda a: a.reshape(1, -1)
    for i in range(depth):
        last = i == depth - 1
        mod = mod_all[i]
        g1 = row1(norm1_g[i])
        q, k, vt = _attn_project(tok, g1, mod, w_attn_b, i, row1(q_gain[i]), row1(k_gain[i]),
                                 cos2, sin2, rows)
        attn = _attention(q, k, vt, rows, latent=True)
        if not last:
            attn = _attention(q, k, vt, rows, latent=False, out=attn)
        prep = {
            'mu_prev': row1(mu_prev[i]), 'mu_next': row1(mu_next[i]),
            'g2': g2[i].astype(BF16), 'wmix': _mix_weights(w2[i], a2[i]).astype(BF16),
            'w0': w0[i], 'a0': a0[i],
            'k_k': row1(k_k[i]), 'k_a': row1(k_a[i]), 'r_k': row1(r_k[i]),
        }
        r_s, kk_s, v_s, g_s, bonus, lw, kd, bd = _rwkv_in(tok, g1, mod, w_rwkv_b, i, prep, rows)
        y = _rwkv_scan(r_s, kk_s, v_s, lw, kd, bd, rows)
        n_rows = rows.n_lat if last else rows.n_all
        tok = _mix_output(attn, y, bonus, g_s, row1(ln_x_w[i]), row1(ln_x_b[i]), tok, mod,
                          w_out_b, i, rows, n_rows)
        tok = _ffn(tok, row1(norm2_g[i]), mod, w1_b, w3_b, w2_b, i, row1(final_g),
                   rows, n_rows, final=last)
    return tok.reshape(batch, seq, d)
```

```python
import functools

import jax
import jax.numpy as jnp
from jax import lax
from jax.experimental import pallas as pl
from jax.experimental.pallas import tpu as pltpu

F32 = jnp.float32
BF16 = jnp.bfloat16

D_MODEL = 2048
HEAD_DIM = 128
N_Q_HEADS = 8
N_KV_HEADS = 2
GQA_GROUP = N_Q_HEADS // N_KV_HEADS
Q_COLS = N_Q_HEADS * HEAD_DIM
KV_COLS = N_KV_HEADS * HEAD_DIM
ATTN_COLS = Q_COLS + 2 * KV_COLS
RWKV_WIDTH = 1024
RWKV_HEAD = 64
LORA_GATE = 256
LORA_DECAY = 96
LORA_ICLR = 96
LORA_COLS = LORA_GATE + 2 * LORA_DECAY + 2 * LORA_ICLR
LORA_MIX = 2 * LORA_DECAY + 2 * LORA_ICLR
RKV_COLS = 3 * RWKV_WIDTH
D_FF = 5632
GRID_W = 64
ROPE_THETA = 10000.0
ROPE_FREQS = HEAD_DIM // 4
ATTN_SCALE = HEAD_DIM ** -0.5
LOG2_E = 1.4426950408889634
NORM_EPS = 1e-6
GN_EPS = 64e-5
L2_EPS = 1e-12
DECAY_SCALE = 0.6065306597126334

LANES = 128
SUBLANES = 8
VMEM_LIMIT = 56 * 1024 * 1024

ATTN_KV_CHUNK = 1024
ATTN_LOOKAHEAD = 8
RWKV_IN_ROWS = 256
SCAN_CHUNK = 64
SCAN_HEADS = 4
INV_BLOCK = 16
SCAN_GROUPS = 4
SCAN_SUBCHUNKS = 4
SCAN_WAVE_SKEW = 2

NN = ((1,), (0,))
NT = ((1,), (1,))
TN = ((0,), (0,))


def _dot(a, b, dims=NN):
    return lax.dot_general(a, b, (dims, ((), ())), preferred_element_type=F32)


def _split(x):
    hi = x.astype(BF16)
    lo = (x - hi.astype(F32)).astype(BF16)
    return hi, lo


def _mm3(a, b, dims=NN):
    ah, al = a
    bh, bl = b
    return _dot(ah, bh, dims) + (_dot(ah, bl, dims) + _dot(al, bh, dims))


def _mm3f(a, b, dims=NN):
    return _mm3(_split(a), _split(b), dims)


def _mm_exact_rhs(a, b_exact, dims=NN):
    ah, al = _split(a)
    return _dot(ah, b_exact, dims) + _dot(al, b_exact, dims)


def _sigmoid(z):
    return 1.0 / (1.0 + jnp.exp(-z))


def _norm_mod(x, g, scale, shift):
    ms = jnp.mean(x * x, axis=-1, keepdims=True)
    y = x * lax.rsqrt(ms + NORM_EPS) * g
    return y * (1.0 + scale) + shift


def _head_ones(width):
    r = lax.broadcasted_iota(jnp.int32, (width, width), 0) >> 6
    c = lax.broadcasted_iota(jnp.int32, (width, width), 1) >> 6
    return jnp.where(r == c, 1.0, 0.0).astype(BF16)


def _head_sum(x, ones):
    w = ones.shape[0]
    xb = x.astype(BF16)
    parts = [_dot(xb[:, c:c + w], ones) for c in range(0, x.shape[1], w)]
    return jnp.concatenate(parts, axis=1)


def _params(sem):
    return pltpu.CompilerParams(dimension_semantics=sem, vmem_limit_bytes=VMEM_LIMIT)


def _mod_kernel(c_ref, w_ref, b_ref, o_ref):
    c = c_ref[...]
    s = c * _sigmoid(c)
    o_ref[...] = _mm3f(s, w_ref[...]) + b_ref[...]


def _modulation(cond, ada_w, ada_b):
    depth, d, n = ada_w.shape
    tn = 512
    rows = cond.shape[0]
    return pl.pallas_call(
        _mod_kernel,
        grid=(depth, n // tn),
        in_specs=[
            pl.BlockSpec((rows, d), lambda l, j: (0, 0)),
            pl.BlockSpec((None, d, tn), lambda l, j: (l, 0, j)),
            pl.BlockSpec((None, 1, tn), lambda l, j: (l, 0, j)),
        ],
        out_specs=pl.BlockSpec((None, rows, tn), lambda l, j: (l, 0, j)),
        out_shape=jax.ShapeDtypeStruct((depth, rows, n), F32),
        compiler_params=_params(("parallel", "parallel")),
        name="modulation",
    )(cond, ada_w, ada_b.reshape(depth, 1, n))


class _Rows:
    def __init__(self, batch, seq, ctx_len, tm):
        self.batch, self.seq, self.ctx_len, self.tm = batch, seq, ctx_len, tm
        self.n_lat = batch * seq
        self.n_ctx = batch * ctx_len
        self.n_all = self.n_lat + self.n_ctx
        assert seq % tm == 0 and self.n_ctx % tm == 0
        self.blocks_per_batch = seq // tm

    def mod_row(self, i, tm=None):
        per_batch = self.blocks_per_batch if tm is None else self.seq // tm
        return jnp.minimum(i // per_batch, self.batch)


def _mod_spec(rows, which, ngrid):
    d = D_MODEL
    if ngrid == 1:
        return pl.BlockSpec((None, 1, d), lambda i: (rows.mod_row(i), 0, which))
    return pl.BlockSpec((None, 1, d), lambda i, j: (rows.mod_row(i), 0, which))


def _attn_proj_kernel(x_ref, g_ref, sc_ref, sh_ref, w_ref, qg_ref, kg_ref, cos_ref, sin_ref,
                      q_ref, k_ref, vt_ref):
    h = _norm_mod(x_ref[...], g_ref[...], sc_ref[...], sh_ref[...]).astype(BF16)
    f = _dot(h, w_ref[...])
    cos = cos_ref[...]
    sin = sin_ref[...]

    def head(xh, gain, scale):
        ms = jnp.mean(xh * xh, axis=-1, keepdims=True)
        y = xh * lax.rsqrt(ms + NORM_EPS) * gain
        y = y * cos + pltpu.roll(y, HEAD_DIM // 2, 1) * sin
        return (y * scale).astype(BF16)

    qg = qg_ref[...]
    kg = kg_ref[...]
    for hq in range(N_Q_HEADS):
        c = hq * HEAD_DIM
        q_ref[:, c:c + HEAD_DIM] = head(f[:, c:c + HEAD_DIM], qg, ATTN_SCALE * LOG2_E)
    for hk in range(N_KV_HEADS):
        c = hk * HEAD_DIM
        k_ref[:, c:c + HEAD_DIM] = head(f[:, Q_COLS + c:Q_COLS + c + HEAD_DIM], kg, 1.0)
    vt_ref[...] = f[:, Q_COLS + KV_COLS:ATTN_COLS].T.astype(BF16)


def _attn_project(x, g, mod, w, layer, q_gain, k_gain, cos2, sin2, rows):
    tm, d = rows.tm, D_MODEL
    bpb = rows.blocks_per_batch

    def rope_idx(i):
        return (jnp.where(i < rows.n_lat // tm, i % bpb, bpb), 0)

    return pl.pallas_call(
        _attn_proj_kernel,
        grid=(rows.n_all // tm,),
        in_specs=[
            pl.BlockSpec((tm, d), lambda i: (i, 0)),
            pl.BlockSpec((1, d), lambda i: (0, 0)),
            _mod_spec(rows, 1, 1),
            _mod_spec(rows, 0, 1),
            pl.BlockSpec((None, d, ATTN_COLS), lambda i: (layer, 0, 0)),
            pl.BlockSpec((1, HEAD_DIM), lambda i: (0, 0)),
            pl.BlockSpec((1, HEAD_DIM), lambda i: (0, 0)),
            pl.BlockSpec((tm, HEAD_DIM), rope_idx),
            pl.BlockSpec((tm, HEAD_DIM), rope_idx),
        ],
        out_specs=[
            pl.BlockSpec((tm, Q_COLS), lambda i: (i, 0)),
            pl.BlockSpec((tm, KV_COLS), lambda i: (i, 0)),
            pl.BlockSpec((KV_COLS, tm), lambda i: (0, i)),
        ],
        out_shape=[
            jax.ShapeDtypeStruct((rows.n_all, Q_COLS), BF16),
            jax.ShapeDtypeStruct((rows.n_all, KV_COLS), BF16),
            jax.ShapeDtypeStruct((KV_COLS, rows.n_all), BF16),
        ],
        compiler_params=_params(("parallel",)),
        name="attn_proj",
    )(x, g, mod, mod, w, q_gain, k_gain, cos2, sin2)


def _attn_kernel(q_ref, *refs, n_kv, kv_chunk):
    k_refs = refs[:n_kv]
    vt_refs = refs[n_kv:2 * n_kv]
    o_ref = refs[-1]
    tq = q_ref.shape[0]
    heads = range(GQA_GROUP)
    q = [q_ref[:, h * HEAD_DIM:(h + 1) * HEAD_DIM] for h in heads]
    m = [jnp.full((1, tq), -jnp.inf, F32) for _ in heads]
    l = [jnp.zeros((1, tq), F32) for _ in heads]
    acc = [jnp.zeros((HEAD_DIM, tq), F32) for _ in heads]
    items = []
    for k_ref, vt_ref in zip(k_refs, vt_refs):
        n = k_ref.shape[0]
        ck = min(kv_chunk, n)
        items += [(k_ref, vt_ref, c0, ck, h) for c0 in range(0, n, ck) for h in heads]

    def scores(item):
        k_ref, _, c0, ck, h = item
        return _dot(k_ref[c0:c0 + ck, :], q[h], NT)

    ahead = min(ATTN_LOOKAHEAD, len(items))
    pending = [scores(it) for it in items[:ahead]]
    for i, (_, vt_ref, c0, ck, h) in enumerate(items):
        if i + ahead < len(items):
            pending.append(scores(items[i + ahead]))
        st = pending.pop(0)
        m_new = jnp.maximum(m[h], jnp.max(st, axis=0, keepdims=True))
        alpha = jnp.exp2(m[h] - m_new)
        p = jnp.exp2(st - m_new)
        l[h] = alpha * l[h] + jnp.sum(p, axis=0, keepdims=True)
        acc[h] = alpha * acc[h] + _dot(vt_ref[:, c0:c0 + ck], p.astype(BF16))
        m[h] = m_new
    for h in heads:
        o_ref[:, h * HEAD_DIM:(h + 1) * HEAD_DIM] = (acc[h] / l[h]).T.astype(o_ref.dtype)


def _attention(q, k, v, rows, latent, out=None):
    b, s, cl = rows.batch, rows.seq, rows.ctx_len
    gw = GQA_GROUP * HEAD_DIM
    ctx_blk0 = rows.n_lat // cl
    if latent:
        tq = min(256, s)
        nq = s // tq
        q_spec = pl.BlockSpec((tq, gw), lambda bi, hi, qi: (bi * nq + qi, hi))
        k_specs = [pl.BlockSpec((s, HEAD_DIM), lambda bi, hi, qi: (bi, hi)),
                   pl.BlockSpec((cl, HEAD_DIM), lambda bi, hi, qi: (ctx_blk0 + bi, hi))]
        vt_specs = [pl.BlockSpec((HEAD_DIM, s), lambda bi, hi, qi: (hi, bi)),
                    pl.BlockSpec((HEAD_DIM, cl), lambda bi, hi, qi: (hi, ctx_blk0 + bi))]
        name = "attention_latent"
    else:
        tq = cl
        nq = 1
        q_spec = pl.BlockSpec((tq, gw), lambda bi, hi, qi: (ctx_blk0 + bi, hi))
        k_specs = [pl.BlockSpec((cl, HEAD_DIM), lambda bi, hi, qi: (ctx_blk0 + bi, hi))]
        vt_specs = [pl.BlockSpec((HEAD_DIM, cl), lambda bi, hi, qi: (hi, ctx_blk0 + bi))]
        name = "attention_context"
    n_kv = len(k_specs)
    in_specs = [q_spec] + k_specs + vt_specs
    args = [q] + [k] * n_kv + [v] * n_kv
    aliases = {}
    if out is not None:
        aliases = {len(args): 0}
        in_specs.append(pl.BlockSpec(memory_space=pl.ANY))
        args.append(out)
    return pl.pallas_call(
        functools.partial(_attn_kernel, n_kv=n_kv, kv_chunk=ATTN_KV_CHUNK),
        grid=(b, N_KV_HEADS, nq),
        in_specs=in_specs,
        out_specs=q_spec,
        out_shape=jax.ShapeDtypeStruct((rows.n_all, Q_COLS), BF16),
        input_output_aliases=aliases,
        compiler_params=_params(("parallel", "parallel", "arbitrary")),
        name=name,
    )(*args)


def _rwkv_in_kernel(x_ref, xp_ref, xn_ref, g_ref, sc_ref, sh_ref, w_ref, mup_ref, mun_ref,
                    g2_ref, wmix_ref, w0_ref, a0_ref, kk_scale_ref, ka_ref, rk_ref,
                    r_o, kk_o, v_o, g_o, bonus_o, lw_o, kd_o, bd_o,
                    *, n_lat_blocks, bps_lat, bps_ctx):
    i = pl.program_id(0)
    tm = x_ref.shape[0]
    s = SUBLANES
    is_lat = i < n_lat_blocks
    j = jnp.where(is_lat, i, i - n_lat_blocks)
    bps = jnp.where(is_lat, bps_lat, bps_ctx)
    pos = j % bps
    keep_prev = jnp.where(pos == 0, 0.0, 1.0)
    keep_next = jnp.where(pos == bps - 1, 0.0, 1.0)
    row = lax.broadcasted_iota(jnp.int32, (s, 1), 0)
    first_row = jnp.where(row == 0, keep_prev, 1.0)
    last_row = jnp.where(row == s - 1, keep_next, 1.0)

    norm = lambda ref: _norm_mod(ref[...], g_ref[...], sc_ref[...], sh_ref[...]).astype(BF16)
    h_ext = jnp.concatenate([norm(xp_ref), norm(x_ref), norm(xn_ref)], axis=0)
    f = _dot(h_ext, w_ref[...])
    n_ext = tm + 2 * s

    def shifted(c):
        x = f[:, c]
        prev = pltpu.roll(x, 1, 0)[s:s + tm]
        nxt = pltpu.roll(x, n_ext - 1, 0)[s:s + tm]
        prev = jnp.concatenate([prev[:s] * first_row, prev[s:]], axis=0)
        nxt = jnp.concatenate([nxt[:tm - s], nxt[tm - s:] * last_row], axis=0)
        mu_prev = mup_ref[:, c]
        mu_next = mun_ref[:, c]
        return (1.0 - mu_prev - mu_next) * x[s:s + tm] + mu_prev * prev + mu_next * nxt

    w = RWKV_WIDTH
    r = shifted(slice(0, w))
    k = shifted(slice(w, 2 * w))
    v = shifted(slice(2 * w, 3 * w))
    lora = shifted(slice(3 * w, 3 * w + LORA_COLS))

    gate_in = _sigmoid(lora[:, :LORA_GATE])
    mix = lora[:, LORA_GATE:]
    lane = lax.broadcasted_iota(jnp.int32, (1, LORA_MIX), 1)
    mix_in = jnp.where(lane < 2 * LORA_DECAY, jnp.tanh(mix), mix)
    g = _dot(gate_in.astype(BF16), g2_ref[...])
    wa = _dot(mix_in.astype(BF16), wmix_ref[...])

    ones = _head_ones(2 * LANES)
    kk = k * kk_scale_ref[...]
    kk = kk * lax.rsqrt(_head_sum(kk * kk, ones) + L2_EPS)

    r_o[...] = r
    kk_o[...] = kk
    v_o[...] = v
    g_o[...] = g
    rk = rk_ref[...]
    ka = ka_ref[...]
    bonus = jnp.zeros_like(r)
    for d in range(2):
        z = w0_ref[d:d + 1, :] + wa[:, d * w:(d + 1) * w]
        lw_o[d] = -DECAY_SCALE * _sigmoid(z)
        a = _sigmoid(a0_ref[d:d + 1, :] + wa[:, (2 + d) * w:(3 + d) * w])
        kd = k * (1.0 + (a - 1.0) * ka)
        kd_o[d] = kd
        bd_o[d] = kk * a
        bonus = bonus + _head_sum(r * kd * rk, ones) * v
    bonus_o[...] = bonus


def _rwkv_in(x, g, mod, w_all, layer, p, rows):
    tm = min(RWKV_IN_ROWS, rows.ctx_len)
    d = D_MODEL
    n_all = rows.n_all
    nblk = n_all // tm
    sub_per_blk = tm // SUBLANES
    n_sub = n_all // SUBLANES
    w = RWKV_WIDTH
    cols = RKV_COLS + LORA_COLS

    def prev_idx(i):
        return (jnp.maximum(i * sub_per_blk - 1, 0), 0)

    def next_idx(i):
        return (jnp.minimum((i + 1) * sub_per_blk, n_sub - 1), 0)

    once = pl.Buffered(1)
    full = lambda shape: pl.BlockSpec(shape, lambda i: (0,) * len(shape), pipeline_mode=once)
    mod_spec = lambda which: pl.BlockSpec((None, 1, d), lambda i: (rows.mod_row(i, tm), 0, which))
    row_spec = lambda width: pl.BlockSpec((tm, width), lambda i: (i, 0))
    dir_spec = pl.BlockSpec((2, tm, w), lambda i: (0, i, 0))
    kern = functools.partial(_rwkv_in_kernel, n_lat_blocks=rows.n_lat // tm,
                             bps_lat=rows.seq // tm, bps_ctx=rows.ctx_len // tm)
    sds = jax.ShapeDtypeStruct
    return pl.pallas_call(
        kern,
        grid=(nblk,),
        in_specs=[
            row_spec(d), pl.BlockSpec((SUBLANES, d), prev_idx), pl.BlockSpec((SUBLANES, d), next_idx),
            full((1, d)), mod_spec(1), mod_spec(0),
            pl.BlockSpec((None, d, cols), lambda i: (layer, 0, 0), pipeline_mode=once),
            full((1, cols)), full((1, cols)),
            full((LORA_GATE, w)), full((LORA_MIX, 4 * w)), full((2, w)), full((2, w)),
            full((1, w)), full((1, w)), full((1, w)),
        ],
        out_specs=[row_spec(w)] * 5 + [dir_spec] * 3,
        out_shape=[sds((n_all, w), F32)] * 5 + [sds((2, n_all, w), F32)] * 3,
        compiler_params=_params(("parallel",)),
        name="rwkv_in",
    )(x, x, x, g, mod, mod, w_all, p['mu_prev'], p['mu_next'],
      p['g2'], p['wmix'], p['w0'], p['a0'], p['k_k'], p['k_a'], p['r_k'])


def _scan_kernel(r_ref, kk_ref, v_ref, lw_ref, kd_ref, bd_ref, y_ref, h_ref,
                 lincl_ref, msd_ref, mso_ref, mincl_ref, eye_ref, hm_ref, hmb_ref,
                 *, chunk, heads, groups):
    c = chunk
    wd = heads * RWKV_HEAD
    step = pl.program_id(3)

    @pl.when(step == 0)
    def _():
        h_ref[...] = jnp.zeros_like(h_ref)
        sgn = 1 - 2 * pl.program_id(0)
        iota = lambda shape, ax: lax.broadcasted_iota(jnp.int32, shape, ax)
        rel_c = (iota((c, c), 0) - iota((c, c), 1)) * sgn
        lincl_ref[...] = jnp.where(rel_c >= 0, 1.0, 0.0).astype(BF16)
        ti = iota((c, wd), 0)
        si = iota((c, wd), 1) & (c - 1)
        rel = (ti - si) * sgn
        log2b = INV_BLOCK.bit_length() - 1
        same_blk = (ti >> log2b) == (si >> log2b)
        strict = jnp.where(rel > 0, 1.0, 0.0)
        msd_ref[...] = jnp.where(same_blk, strict, 0.0)
        mso_ref[...] = jnp.where(same_blk, 0.0, strict)
        mincl_ref[...] = jnp.where(rel >= 0, 1.0, 0.0)
        eye_ref[...] = jnp.where(ti == si, 1.0, 0.0)
        hm = jnp.where((iota((wd, wd), 0) >> 6) == (iota((wd, wd), 1) >> 6), 1.0, 0.0)
        hm_ref[...] = hm
        hmb_ref[...] = hm.astype(BF16)

    n_sub = r_ref.shape[0] // c
    d = pl.program_id(0)
    rows = [pl.ds(pl.multiple_of(jnp.where(d == 0, i, n_sub - 1 - i) * c, c), c) for i in range(n_sub)]
    lanes = [slice(q * wd, (q + 1) * wd) for q in range(groups)]
    refs = (r_ref, kk_ref, v_ref, lw_ref, kd_ref, bd_ref)
    waves = [_chunk_operator_stages(*[[ref[rw, ln] for ln in lanes] for ref in refs],
                                    lincl_ref, msd_ref, mso_ref, mincl_ref, eye_ref, hm_ref, hmb_ref, heads)
             for rw in rows]
    h = [h_ref[q] for q in range(groups)]
    done = 0
    tick = 0
    while done < n_sub:
        for i in range(done, n_sub):
            if tick < i * SCAN_WAVE_SKEW:
                break
            try:
                next(waves[i])
            except StopIteration as fin:
                m_c, n_c, rhat, oloc = fin.value
                h, y = _advance(m_c, n_c, rhat, oloc, h, hmb_ref, heads)
                for q in range(groups):
                    y_ref[rows[i], lanes[q]] = y[q]
                done += 1
        tick += 1
    for q in range(groups):
        h_ref[q] = h[q]


def _advance(m_c, n_c, rhat, oloc, h, hmb_ref, heads):
    c = RWKV_HEAD
    hmb = hmb_ref[...]

    def stacked(zb):
        return jnp.concatenate([zb] * heads, axis=0) * hmb

    def step(mi, ri, hi):
        lhs = jnp.concatenate([mi, ri], axis=0).astype(BF16)
        return _dot(lhs, stacked(hi.astype(BF16)))

    res = [step(*a) for a in zip(m_c, rhat, h)]
    h_new = [z[:c] + ni for z, ni in zip(res, n_c)]
    y = [z[c:] + oi for z, oi in zip(res, oloc)]
    return h_new, y


def _chunk_operator_stages(r, kk, v, lw, k, b, lincl_ref, msd_ref, mso_ref, mincl_ref, eye_ref,
                           hm_ref, hmb_ref, heads):
    c = RWKV_HEAD
    wd = heads * RWKV_HEAD
    hmb = hmb_ref[...]
    each = lambda f, *ls: [f(*a) for a in zip(*ls)]

    def stacked(z):
        zb = z.astype(BF16)
        return jnp.concatenate([zb] * heads, axis=0) * hmb

    def mm(x, y_stacked, dims=NN):
        return _dot(x.astype(BF16), y_stacked, dims)

    def head_blocks(full):
        z = full * hm_ref[...]
        out = z[0:c]
        for hh in range(1, heads):
            out = out + z[hh * c:(hh + 1) * c]
        return out

    cat0 = lambda *xs: jnp.concatenate(xs, axis=0)
    cat1 = lambda *xs: jnp.concatenate(xs, axis=1)
    mul = lambda x, y: x * y
    sub = lambda x, y: x - y
    add = lambda x, y: x + y
    left = lambda z: z[:, :wd]
    right = lambda z: z[:, wd:]
    top = lambda z: z[:c]
    bottom = lambda z: z[c:]
    eye = eye_ref[...]
    lincl = lincl_ref[...]
    g = each(lambda x: _mm_exact_rhs_left(lincl, x), lw)
    yield
    gt = each(lambda x: jnp.sum(x, axis=0, keepdims=True), lw)
    e_x = each(lambda gi, lwi: jnp.exp(gi - lwi), g, lw)
    e_g = each(jnp.exp, g)
    e_n = each(lambda gi: jnp.exp(-gi), g)
    e_c = each(lambda gti, gi: jnp.exp(gti - gi), gt, g)
    kt = each(mul, kk, e_x)
    rt = each(mul, r, e_g)
    bh = each(mul, b, e_n)
    kh = each(mul, k, e_n)
    kb = each(mul, k, e_c)
    bb = each(mul, b, e_c)

    gm = each(lambda kti, rti, bhi, khi: mm(cat0(kti, rti), cat0(stacked(bhi), stacked(khi)).T),
              kt, rt, bh, kh)
    yield
    msd = msd_ref[...]
    mso = mso_ref[...]
    m_incl = mincl_ref[...]
    a_d = each(lambda z: z[:c, :wd] * msd, gm)
    a_o = each(lambda z: z[:c, :wd] * mso, gm)
    b_m = each(lambda z: z[:c, wd:] * (msd + mso), gm)
    e_m = each(lambda z: z[c:, :wd] * m_incl, gm)
    c_m = each(lambda z: z[c:, wd:] * m_incl, gm)

    mm_st = lambda x, y: mm(x, stacked(y))
    unzip = lambda pairs: [list(t) for t in zip(*pairs)]

    def fold(xi, di):
        z = mm(cat0(xi, di), stacked(xi))
        return z[:c], di + z[c:]

    x1 = each(lambda z: -z, a_d)
    dm = each(lambda z: eye - z, a_d)
    x2 = each(mm_st, x1, x1)
    bcv = each(lambda bi, ci, vi: mm(cat0(bi, ci), stacked(vi)), b_m, c_m, v)
    bv = each(top, bcv)
    cv = each(bottom, bcv)
    yield
    x4, dm = unzip(each(fold, x2, dm))
    yield
    x8, dm = unzip(each(fold, x4, dm))
    yield
    dm = each(lambda di, xi: di + mm_st(di, xi), dm, x8)
    yield
    n1 = each(mm_st, dm, a_o)
    yield
    n2 = each(mm_st, n1, n1)
    yield
    t1 = each(lambda z: eye - z, n1)
    t1 = each(lambda ti, ni: ti + mm_st(ti, ni), t1, n2)
    yield
    t_m = each(mm_st, t1, dm)
    yield
    et = each(mm_st, e_m, t_m)
    yield
    both = each(lambda ti, ei, kti, bvi: mm(cat0(ti, ei), cat1(stacked(kti), stacked(bvi))),
                t_m, et, kt, bv)
    yield
    wt = each(lambda z: z[:c, :wd], both)
    ut = each(lambda z: z[:c, wd:], both)
    rhat = each(lambda ri, z: ri - z[c:, :wd], rt, both)
    oloc = each(lambda ci, z: ci - z[c:, wd:], cv, both)
    full_m = each(lambda bi, wi: _dot(bi.astype(BF16), wi.astype(BF16), TN), bb, wt)
    full_n = each(lambda ki, bi, vi, ui: _dot(cat0(ki, bi).astype(BF16), cat0(vi, -ui).astype(BF16), TN),
                  kb, bb, v, ut)
    m_c = each(lambda gti, fi: eye * jnp.exp(gti) - head_blocks(fi), gt, full_m)
    n_c = each(head_blocks, full_n)
    return m_c, n_c, rhat, oloc


def _mm_exact_rhs_left(a_exact, b):
    bh = b.astype(BF16)
    r1 = b - bh.astype(F32)
    bm = r1.astype(BF16)
    bl = (r1 - bm.astype(F32)).astype(BF16)
    return _dot(a_exact, bh) + (_dot(a_exact, bm) + _dot(a_exact, bl))


def _rwkv_scan(r, kk, v, lw, kd, bd, rows):
    c, heads = SCAN_CHUNK, SCAN_HEADS
    wd = heads * RWKV_HEAD
    nq = RWKV_WIDTH // wd
    rb = c * SCAN_SUBCHUNKS
    assert rows.ctx_len % rb == 0 and rows.seq % rb == 0
    nc_ctx = rows.ctx_len // rb
    nc_lat = rows.seq // rb
    lat_blocks = rows.n_lat // rb

    def chunk_idx(d, bi, st):
        in_ctx = st < nc_ctx
        t_ctx = jnp.where(d == 0, st, nc_ctx - 1 - st)
        sl = st - nc_ctx
        t_lat = jnp.where(d == 0, sl, nc_lat - 1 - sl)
        return jnp.where(in_ctx, lat_blocks + bi * nc_ctx + t_ctx, bi * nc_lat + t_lat)

    assert c == RWKV_HEAD
    groups = SCAN_GROUPS
    bw = wd * groups
    shared = pl.BlockSpec((rb, bw), lambda d, bi, q, st: (chunk_idx(d, bi, st), q))
    per_dir = pl.BlockSpec((None, rb, bw), lambda d, bi, q, st: (d, chunk_idx(d, bi, st), q))
    return pl.pallas_call(
        functools.partial(_scan_kernel, chunk=c, heads=heads, groups=groups),
        grid=(2, rows.batch, nq // groups, nc_ctx + nc_lat),
        in_specs=[shared, shared, shared, per_dir, per_dir, per_dir],
        out_specs=per_dir,
        out_shape=jax.ShapeDtypeStruct((2, rows.n_all, RWKV_WIDTH), F32),
        scratch_shapes=[
            pltpu.VMEM((groups, c, wd), F32),
            pltpu.VMEM((c, c), BF16),
            pltpu.VMEM((c, wd), F32),
            pltpu.VMEM((c, wd), F32),
            pltpu.VMEM((c, wd), F32),
            pltpu.VMEM((c, wd), F32),
            pltpu.VMEM((wd, wd), F32),
            pltpu.VMEM((wd, wd), BF16),
        ],
        compiler_params=_params(("parallel", "parallel", "parallel", "arbitrary")),
        name="rwkv_scan",
    )(r, kk, v, lw, kd, bd)


def _out_kernel(attn_ref, y_ref, bonus_ref, g_ref, lnw_ref, lnb_ref, x_ref, gate_ref,
                wa_ref, wr_ref, o_ref, rw_ref):
    @pl.when(pl.program_id(1) == 0)
    def _():
        ones = _head_ones(2 * LANES)
        y = y_ref[0] + y_ref[1]
        mean = _head_sum(y, ones) * (1.0 / RWKV_HEAD)
        yc = y - mean
        var = _head_sum(yc * yc, ones) * (1.0 / RWKV_HEAD)
        yn = yc * lax.rsqrt(var + GN_EPS) * lnw_ref[...] + lnb_ref[...]
        rw_ref[...] = ((yn + bonus_ref[...]) * g_ref[...]).astype(BF16)

    out = _dot(attn_ref[...], wa_ref[...]) + _dot(rw_ref[...], wr_ref[...])
    o_ref[...] = x_ref[...] + gate_ref[...] * out


def _mix_output(attn, y, bonus, g, ln_w, ln_b, x, mod, w_out, layer, rows, n_rows):
    d = D_MODEL
    tm = min(256, rows.tm)
    tn = d
    w = RWKV_WIDTH
    return pl.pallas_call(
        _out_kernel,
        grid=(n_rows // tm, d // tn),
        in_specs=[
            pl.BlockSpec((tm, Q_COLS), lambda i, j: (i, 0)),
            pl.BlockSpec((2, tm, w), lambda i, j: (0, i, 0)),
            pl.BlockSpec((tm, w), lambda i, j: (i, 0)),
            pl.BlockSpec((tm, w), lambda i, j: (i, 0)),
            pl.BlockSpec((1, w), lambda i, j: (0, 0)),
            pl.BlockSpec((1, w), lambda i, j: (0, 0)),
            pl.BlockSpec((tm, tn), lambda i, j: (i, j)),
            pl.BlockSpec((None, 1, tn), lambda i, j: (rows.mod_row(i, tm), 0, 2 * (d // tn) + j)),
            pl.BlockSpec((None, Q_COLS, tn), lambda i, j: (layer, 0, j)),
            pl.BlockSpec((None, w, tn), lambda i, j: (layer, 1, j)),
        ],
        out_specs=pl.BlockSpec((tm, tn), lambda i, j: (i, j)),
        out_shape=jax.ShapeDtypeStruct((n_rows, d), F32),
        scratch_shapes=[pltpu.VMEM((tm, w), BF16)],
        compiler_params=_params(("parallel", "arbitrary")),
        name="mix_output",
    )(attn, y, bonus, g, ln_w, ln_b, x, mod, w_out, w_out)


def _ffn_kernel(x_ref, g_ref, sc_ref, sh_ref, gate_ref, w1_ref, w3_ref, w2_ref, fg_ref,
                o_ref, h_ref, acc_ref, *, final):
    j = pl.program_id(1)

    @pl.when(j == 0)
    def _():
        h_ref[...] = _norm_mod(x_ref[...], g_ref[...], sc_ref[...], sh_ref[...]).astype(BF16)
        acc_ref[...] = jnp.zeros_like(acc_ref)

    h = h_ref[...]
    u = _dot(h, w1_ref[...])
    a = (u * _sigmoid(u)) * _dot(h, w3_ref[...])
    acc_ref[...] += _dot(a.astype(BF16), w2_ref[...])

    @pl.when(j == pl.num_programs(1) - 1)
    def _():
        y = x_ref[...] + gate_ref[...] * acc_ref[...]
        if final:
            ms = jnp.mean(y * y, axis=-1, keepdims=True)
            y = y * lax.rsqrt(ms + NORM_EPS) * fg_ref[...]
        o_ref[...] = y


def _ffn(x, g, mod, w1, w3, w2, layer, final_g, rows, n_rows, final):
    tm, d = rows.tm, D_MODEL
    tf = 512
    return pl.pallas_call(
        functools.partial(_ffn_kernel, final=final),
        grid=(n_rows // tm, D_FF // tf),
        in_specs=[
            pl.BlockSpec((tm, d), lambda i, j: (i, 0)),
            pl.BlockSpec((1, d), lambda i, j: (0, 0)),
            _mod_spec(rows, 4, 2),
            _mod_spec(rows, 3, 2),
            _mod_spec(rows, 5, 2),
            pl.BlockSpec((None, d, tf), lambda i, j: (layer, 0, j)),
            pl.BlockSpec((None, d, tf), lambda i, j: (layer, 0, j)),
            pl.BlockSpec((None, tf, d), lambda i, j: (layer, j, 0)),
            pl.BlockSpec((1, d), lambda i, j: (0, 0)),
        ],
        out_specs=pl.BlockSpec((tm, d), lambda i, j: (i, 0)),
        out_shape=jax.ShapeDtypeStruct((n_rows, d), F32),
        scratch_shapes=[pltpu.VMEM((tm, d), BF16), pltpu.VMEM((tm, d), F32)],
        compiler_params=_params(("parallel", "arbitrary")),
        name="ffn_final" if final else "ffn",
    )(x, g, mod, mod, mod, w1, w3, w2, final_g)


def _rope_tables(seq, tm):
    n_rows = seq // GRID_W
    row = jnp.broadcast_to(jnp.arange(n_rows)[:, None], (n_rows, GRID_W)).reshape(-1)
    col = jnp.broadcast_to(jnp.arange(GRID_W)[None, :], (n_rows, GRID_W)).reshape(-1)
    inv = ROPE_THETA ** (-jnp.arange(ROPE_FREQS, dtype=F32) / ROPE_FREQS)
    ang = jnp.concatenate([row[:, None].astype(F32) * inv, col[:, None].astype(F32) * inv], axis=-1)
    cos, sin = jnp.cos(ang), jnp.sin(ang)
    cos2 = jnp.concatenate([cos, cos], axis=-1)
    sin2 = jnp.concatenate([-sin, sin], axis=-1)
    cos2 = jnp.concatenate([cos2, jnp.ones((tm, HEAD_DIM), F32)], axis=0)
    sin2 = jnp.concatenate([sin2, jnp.zeros((tm, HEAD_DIM), F32)], axis=0)
    return cos2, sin2


def _mix_weights(w2, a2):
    w = RWKV_WIDTH
    out = jnp.zeros((LORA_MIX, 4 * w), F32)
    out = out.at[0:LORA_DECAY, 0:w].set(w2[0])
    out = out.at[LORA_DECAY:2 * LORA_DECAY, w:2 * w].set(w2[1])
    o = 2 * LORA_DECAY
    out = out.at[o:o + LORA_ICLR, 2 * w:3 * w].set(a2[0])
    out = out.at[o + LORA_ICLR:o + 2 * LORA_ICLR, 3 * w:4 * w].set(a2[1])
    return out


def kernel(x, c, ctx, c_ctx, norm1_g, norm2_g, ada_w, ada_b, w_in, q_gain, k_gain, mu_prev, mu_next,
           w0, w2, a0, a2, g2, k_k, k_a, r_k, ln_x_w, ln_x_b, w_out, ffn_w1, ffn_w3, ffn_w2, final_g):
    batch, seq, d = x.shape
    ctx_len = ctx.shape[1]
    depth = w_in.shape[0]
    rows = _Rows(batch, seq, ctx_len, min(512, batch * ctx_len))
    w = RWKV_WIDTH

    cond = jnp.concatenate([c, c_ctx[None, :], jnp.zeros((SUBLANES - batch - 1, d), F32)], axis=0)
    mod_all = _modulation(cond, ada_w, ada_b)
    mod_all = mod_all.reshape(depth, SUBLANES, 1, 6 * d)
    cos2, sin2 = _rope_tables(seq, rows.tm)

    w_attn_b = w_in[:, :, :ATTN_COLS].astype(BF16)
    w_rwkv_b = w_in[:, :, ATTN_COLS:].astype(BF16)
    w_out_b = w_out.astype(BF16)
    w1_b = ffn_w1.astype(BF16)
    w3_b = ffn_w3.astype(BF16)
    w2_b = ffn_w2.astype(BF16)

    tok = jnp.concatenate([x.reshape(batch * seq, d), ctx.reshape(batch * ctx_len, d)], axis=0)
    row1 = lambda a: a.reshape(1, -1)
    for i in range(depth):
        last = i == depth - 1
        mod = mod_all[i]
        g1 = row1(norm1_g[i])
        q, k, vt = _attn_project(tok, g1, mod, w_attn_b, i, row1(q_gain[i]), row1(k_gain[i]),
                                 cos2, sin2, rows)
        attn = _attention(q, k, vt, rows, latent=True)
        if not last:
            attn = _attention(q, k, vt, rows, latent=False, out=attn)
        prep = {
            'mu_prev': row1(mu_prev[i]), 'mu_next': row1(mu_next[i]),
            'g2': g2[i].astype(BF16), 'wmix': _mix_weights(w2[i], a2[i]).astype(BF16),
            'w0': w0[i], 'a0': a0[i],
            'k_k': row1(k_k[i]), 'k_a': row1(k_a[i]), 'r_k': row1(r_k[i]),
        }
        r_s, kk_s, v_s, g_s, bonus, lw, kd, bd = _rwkv_in(tok, g1, mod, w_rwkv_b, i, prep, rows)
        y = _rwkv_scan(r_s, kk_s, v_s, lw, kd, bd, rows)
        n_rows = rows.n_lat if last else rows.n_all
        tok = _mix_output(attn, y, bonus, g_s, row1(ln_x_w[i]), row1(ln_x_b[i]), tok, mod,
                          w_out_b, i, rows, n_rows)
        tok = _ffn(tok, row1(norm2_g[i]), mod, w1_b, w3_b, w2_b, i, row1(final_g),
                   rows, n_rows, final=last)
    return tok.reshape(batch, seq, d)
```

```python
import functools

import jax
import jax.numpy as jnp
from jax import lax
from jax.experimental import pallas as pl
from jax.experimental.pallas import tpu as pltpu

F32 = jnp.float32
BF16 = jnp.bfloat16

D_MODEL = 2048
HEAD_DIM = 128
N_Q_HEADS = 8
N_KV_HEADS = 2
GQA_GROUP = N_Q_HEADS // N_KV_HEADS
Q_COLS = N_Q_HEADS * HEAD_DIM
KV_COLS = N_KV_HEADS * HEAD_DIM
ATTN_COLS = Q_COLS + 2 * KV_COLS
RWKV_WIDTH = 1024
RWKV_HEAD = 64
LORA_GATE = 256
LORA_DECAY = 96
LORA_ICLR = 96
LORA_COLS = LORA_GATE + 2 * LORA_DECAY + 2 * LORA_ICLR
LORA_MIX = 2 * LORA_DECAY + 2 * LORA_ICLR
RKV_COLS = 3 * RWKV_WIDTH
D_FF = 5632
GRID_W = 64
ROPE_THETA = 10000.0
ROPE_FREQS = HEAD_DIM // 4
ATTN_SCALE = HEAD_DIM ** -0.5
LOG2_E = 1.4426950408889634
NORM_EPS = 1e-6
GN_EPS = 64e-5
L2_EPS = 1e-12
DECAY_SCALE = 0.6065306597126334

LANES = 128
SUBLANES = 8
VMEM_LIMIT = 56 * 1024 * 1024

ATTN_KV_CHUNK = 1024
ATTN_LOOKAHEAD = 8
RWKV_IN_ROWS = 256
SCAN_CHUNK = 64
SCAN_HEADS = 4
INV_BLOCK = 16
SCAN_GROUPS = 4
SCAN_SUBCHUNKS = 4
SCAN_WAVE_SKEW = 2

NN = ((1,), (0,))
NT = ((1,), (1,))
TN = ((0,), (0,))


def _dot(a, b, dims=NN):
    return lax.dot_general(a, b, (dims, ((), ())), preferred_element_type=F32)


def _split(x):
    hi = x.astype(BF16)
    lo = (x - hi.astype(F32)).astype(BF16)
    return hi, lo


def _mm3(a, b, dims=NN):
    ah, al = a
    bh, bl = b
    return _dot(ah, bh, dims) + (_dot(ah, bl, dims) + _dot(al, bh, dims))


def _mm3f(a, b, dims=NN):
    return _mm3(_split(a), _split(b), dims)


def _mm_exact_rhs(a, b_exact, dims=NN):
    ah, al = _split(a)
    return _dot(ah, b_exact, dims) + _dot(al, b_exact, dims)


def _sigmoid(z):
    return 1.0 / (1.0 + jnp.exp(-z))


def _norm_mod(x, g, scale, shift):
    ms = jnp.mean(x * x, axis=-1, keepdims=True)
    y = x * lax.rsqrt(ms + NORM_EPS) * g
    return y * (1.0 + scale) + shift


def _head_ones(width):
    r = lax.broadcasted_iota(jnp.int32, (width, width), 0) >> 6
    c = lax.broadcasted_iota(jnp.int32, (width, width), 1) >> 6
    return jnp.where(r == c, 1.0, 0.0).astype(BF16)


def _head_sum(x, ones):
    w = ones.shape[0]
    xb = x.astype(BF16)
    parts = [_dot(xb[:, c:c + w], ones) for c in range(0, x.shape[1], w)]
    return jnp.concatenate(parts, axis=1)


def _params(sem):
    return pltpu.CompilerParams(dimension_semantics=sem, vmem_limit_bytes=VMEM_LIMIT)


def _mod_kernel(c_ref, w_ref, b_ref, o_ref):
    c = c_ref[...]
    s = c * _sigmoid(c)
    o_ref[...] = _mm3f(s, w_ref[...]) + b_ref[...]


def _modulation(cond, ada_w, ada_b):
    depth, d, n = ada_w.shape
    tn = 512
    rows = cond.shape[0]
    return pl.pallas_call(
        _mod_kernel,
        grid=(depth, n // tn),
        in_specs=[
            pl.BlockSpec((rows, d), lambda l, j: (0, 0)),
            pl.BlockSpec((None, d, tn), lambda l, j: (l, 0, j)),
            pl.BlockSpec((None, 1, tn), lambda l, j: (l, 0, j)),
        ],
        out_specs=pl.BlockSpec((None, rows, tn), lambda l, j: (l, 0, j)),
        out_shape=jax.ShapeDtypeStruct((depth, rows, n), F32),
        compiler_params=_params(("parallel", "parallel")),
        name="modulation",
    )(cond, ada_w, ada_b.reshape(depth, 1, n))


class _Rows:
    def __init__(self, batch, seq, ctx_len, tm):
        self.batch, self.seq, self.ctx_len, self.tm = batch, seq, ctx_len, tm
        self.n_lat = batch * seq
        self.n_ctx = batch * ctx_len
        self.n_all = self.n_lat + self.n_ctx
        assert seq % tm == 0 and self.n_ctx % tm == 0
        self.blocks_per_batch = seq // tm

    def mod_row(self, i, tm=None):
        per_batch = self.blocks_per_batch if tm is None else self.seq // tm
        return jnp.minimum(i // per_batch, self.batch)


def _mod_spec(rows, which, ngrid):
    d = D_MODEL
    if ngrid == 1:
        return pl.BlockSpec((None, 1, d), lambda i: (rows.mod_row(i), 0, which))
    return pl.BlockSpec((None, 1, d), lambda i, j: (rows.mod_row(i), 0, which))


def _attn_proj_kernel(x_ref, g_ref, sc_ref, sh_ref, w_ref, qg_ref, kg_ref, cos_ref, sin_ref,
                      q_ref, k_ref, vt_ref):
    qg = qg_ref[...]
    kg = kg_ref[...]
    half = x_ref.shape[0] // 2
    halves = [slice(0, half), slice(half, 2 * half)]
    hs = [_norm_mod(x_ref[rs, :], g_ref[...], sc_ref[...], sh_ref[...]).astype(BF16) for rs in halves]
    fs = [_dot(h, w_ref[...]) for h in hs]
    for rs, f in zip(halves, fs):
        cos = cos_ref[rs, :]
        sin = sin_ref[rs, :]

        def head(xh, gain, scale):
            ms = jnp.mean(xh * xh, axis=-1, keepdims=True)
            y = xh * lax.rsqrt(ms + NORM_EPS) * gain
            y = y * cos + pltpu.roll(y, HEAD_DIM // 2, 1) * sin
            return (y * scale).astype(BF16)

        for hq in range(N_Q_HEADS):
            c = hq * HEAD_DIM
            q_ref[rs, c:c + HEAD_DIM] = head(f[:, c:c + HEAD_DIM], qg, ATTN_SCALE * LOG2_E)
        for hk in range(N_KV_HEADS):
            c = hk * HEAD_DIM
            k_ref[rs, c:c + HEAD_DIM] = head(f[:, Q_COLS + c:Q_COLS + c + HEAD_DIM], kg, 1.0)
        vt_ref[:, rs] = f[:, Q_COLS + KV_COLS:ATTN_COLS].T.astype(BF16)


def _attn_project(x, g, mod, w, layer, q_gain, k_gain, cos2, sin2, rows):
    tm, d = rows.tm, D_MODEL
    bpb = rows.blocks_per_batch

    def rope_idx(i):
        return (jnp.where(i < rows.n_lat // tm, i % bpb, bpb), 0)

    return pl.pallas_call(
        _attn_proj_kernel,
        grid=(rows.n_all // tm,),
        in_specs=[
            pl.BlockSpec((tm, d), lambda i: (i, 0)),
            pl.BlockSpec((1, d), lambda i: (0, 0)),
            _mod_spec(rows, 1, 1),
            _mod_spec(rows, 0, 1),
            pl.BlockSpec((None, d, ATTN_COLS), lambda i: (layer, 0, 0)),
            pl.BlockSpec((1, HEAD_DIM), lambda i: (0, 0)),
            pl.BlockSpec((1, HEAD_DIM), lambda i: (0, 0)),
            pl.BlockSpec((tm, HEAD_DIM), rope_idx),
            pl.BlockSpec((tm, HEAD_DIM), rope_idx),
        ],
        out_specs=[
            pl.BlockSpec((tm, Q_COLS), lambda i: (i, 0)),
            pl.BlockSpec((tm, KV_COLS), lambda i: (i, 0)),
            pl.BlockSpec((KV_COLS, tm), lambda i: (0, i)),
        ],
        out_shape=[
            jax.ShapeDtypeStruct((rows.n_all, Q_COLS), BF16),
            jax.ShapeDtypeStruct((rows.n_all, KV_COLS), BF16),
            jax.ShapeDtypeStruct((KV_COLS, rows.n_all), BF16),
        ],
        compiler_params=_params(("parallel",)),
        name="attn_proj",
    )(x, g, mod, mod, w, q_gain, k_gain, cos2, sin2)


def _attn_kernel(q_ref, *refs, n_kv, kv_chunk):
    k_refs = refs[:n_kv]
    vt_refs = refs[n_kv:2 * n_kv]
    o_ref = refs[-1]
    tq = q_ref.shape[0]
    heads = range(GQA_GROUP)
    q = [q_ref[:, h * HEAD_DIM:(h + 1) * HEAD_DIM] for h in heads]
    m = [jnp.full((1, tq), -jnp.inf, F32) for _ in heads]
    l = [jnp.zeros((1, tq), F32) for _ in heads]
    acc = [jnp.zeros((HEAD_DIM, tq), F32) for _ in heads]
    items = []
    for k_ref, vt_ref in zip(k_refs, vt_refs):
        n = k_ref.shape[0]
        ck = min(kv_chunk, n)
        items += [(k_ref, vt_ref, c0, ck, h) for c0 in range(0, n, ck) for h in heads]

    def scores(item):
        k_ref, _, c0, ck, h = item
        return _dot(k_ref[c0:c0 + ck, :], q[h], NT)

    ahead = min(ATTN_LOOKAHEAD, len(items))
    pending = [scores(it) for it in items[:ahead]]
    for i, (_, vt_ref, c0, ck, h) in enumerate(items):
        if i + ahead < len(items):
            pending.append(scores(items[i + ahead]))
        st = pending.pop(0)
        m_new = jnp.maximum(m[h], jnp.max(st, axis=0, keepdims=True))
        alpha = jnp.exp2(m[h] - m_new)
        p = jnp.exp2(st - m_new)
        l[h] = alpha * l[h] + jnp.sum(p, axis=0, keepdims=True)
        acc[h] = alpha * acc[h] + _dot(vt_ref[:, c0:c0 + ck], p.astype(BF16))
        m[h] = m_new
    for h in heads:
        o_ref[:, h * HEAD_DIM:(h + 1) * HEAD_DIM] = (acc[h] / l[h]).T.astype(o_ref.dtype)


def _attention(q, k, v, rows, latent, out=None):
    b, s, cl = rows.batch, rows.seq, rows.ctx_len
    gw = GQA_GROUP * HEAD_DIM
    ctx_blk0 = rows.n_lat // cl
    if latent:
        tq = min(256, s)
        nq = s // tq
        q_spec = pl.BlockSpec((tq, gw), lambda bi, hi, qi: (bi * nq + qi, hi))
        k_specs = [pl.BlockSpec((s, HEAD_DIM), lambda bi, hi, qi: (bi, hi)),
                   pl.BlockSpec((cl, HEAD_DIM), lambda bi, hi, qi: (ctx_blk0 + bi, hi))]
        vt_specs = [pl.BlockSpec((HEAD_DIM, s), lambda bi, hi, qi: (hi, bi)),
                    pl.BlockSpec((HEAD_DIM, cl), lambda bi, hi, qi: (hi, ctx_blk0 + bi))]
        name = "attention_latent"
    else:
        tq = cl
        nq = 1
        q_spec = pl.BlockSpec((tq, gw), lambda bi, hi, qi: (ctx_blk0 + bi, hi))
        k_specs = [pl.BlockSpec((cl, HEAD_DIM), lambda bi, hi, qi: (ctx_blk0 + bi, hi))]
        vt_specs = [pl.BlockSpec((HEAD_DIM, cl), lambda bi, hi, qi: (hi, ctx_blk0 + bi))]
        name = "attention_context"
    n_kv = len(k_specs)
    in_specs = [q_spec] + k_specs + vt_specs
    args = [q] + [k] * n_kv + [v] * n_kv
    aliases = {}
    if out is not None:
        aliases = {len(args): 0}
        in_specs.append(pl.BlockSpec(memory_space=pl.ANY))
        args.append(out)
    return pl.pallas_call(
        functools.partial(_attn_kernel, n_kv=n_kv, kv_chunk=ATTN_KV_CHUNK),
        grid=(b, N_KV_HEADS, nq),
        in_specs=in_specs,
        out_specs=q_spec,
        out_shape=jax.ShapeDtypeStruct((rows.n_all, Q_COLS), BF16),
        input_output_aliases=aliases,
        compiler_params=_params(("parallel", "parallel", "arbitrary")),
        name=name,
    )(*args)


def _rwkv_in_kernel(x_ref, xp_ref, xn_ref, g_ref, sc_ref, sh_ref, w_ref, mup_ref, mun_ref,
                    g2_ref, wmix_ref, w0_ref, a0_ref, kk_scale_ref, ka_ref, rk_ref,
                    r_o, kk_o, v_o, g_o, bonus_o, lw_o, kd_o, bd_o,
                    *, n_lat_blocks, bps_lat, bps_ctx):
    i = pl.program_id(0)
    tm = x_ref.shape[0]
    s = SUBLANES
    is_lat = i < n_lat_blocks
    j = jnp.where(is_lat, i, i - n_lat_blocks)
    bps = jnp.where(is_lat, bps_lat, bps_ctx)
    pos = j % bps
    keep_prev = jnp.where(pos == 0, 0.0, 1.0)
    keep_next = jnp.where(pos == bps - 1, 0.0, 1.0)
    row = lax.broadcasted_iota(jnp.int32, (s, 1), 0)
    first_row = jnp.where(row == 0, keep_prev, 1.0)
    last_row = jnp.where(row == s - 1, keep_next, 1.0)

    norm = lambda ref: _norm_mod(ref[...], g_ref[...], sc_ref[...], sh_ref[...]).astype(BF16)
    h_ext = jnp.concatenate([norm(xp_ref), norm(x_ref), norm(xn_ref)], axis=0)
    n_ext = tm + 2 * s
    w = RWKV_WIDTH
    panels = [slice(0, w), slice(w, 2 * w), slice(2 * w, 3 * w), slice(3 * w, 3 * w + LORA_COLS)]
    project = lambda c: _dot(h_ext, w_ref[:, c])

    def shifted(x, c):
        prev = pltpu.roll(x, 1, 0)[s:s + tm]
        nxt = pltpu.roll(x, n_ext - 1, 0)[s:s + tm]
        prev = jnp.concatenate([prev[:s] * first_row, prev[s:]], axis=0)
        nxt = jnp.concatenate([nxt[:tm - s], nxt[tm - s:] * last_row], axis=0)
        mu_prev = mup_ref[:, c]
        mu_next = mun_ref[:, c]
        return (1.0 - mu_prev - mu_next) * x[s:s + tm] + mu_prev * prev + mu_next * nxt

    f_next = project(panels[0])
    outs = []
    for pi, c in enumerate(panels):
        f_cur = f_next
        if pi + 1 < len(panels):
            f_next = project(panels[pi + 1])
        outs.append(shifted(f_cur, c))
    r, k, v, lora = outs

    gate_in = _sigmoid(lora[:, :LORA_GATE])
    mix = lora[:, LORA_GATE:]
    lane = lax.broadcasted_iota(jnp.int32, (1, LORA_MIX), 1)
    mix_in = jnp.where(lane < 2 * LORA_DECAY, jnp.tanh(mix), mix)
    g = _dot(gate_in.astype(BF16), g2_ref[...])
    wa = _dot(mix_in.astype(BF16), wmix_ref[...])

    ones = _head_ones(2 * LANES)
    kk = k * kk_scale_ref[...]
    kk = kk * lax.rsqrt(_head_sum(kk * kk, ones) + L2_EPS)

    r_o[...] = r
    kk_o[...] = kk
    v_o[...] = v
    g_o[...] = g
    rk = rk_ref[...]
    ka = ka_ref[...]
    bonus = jnp.zeros_like(r)
    for d in range(2):
        z = w0_ref[d:d + 1, :] + wa[:, d * w:(d + 1) * w]
        lw_o[d] = -DECAY_SCALE * _sigmoid(z)
        a = _sigmoid(a0_ref[d:d + 1, :] + wa[:, (2 + d) * w:(3 + d) * w])
        kd = k * (1.0 + (a - 1.0) * ka)
        kd_o[d] = kd
        bd_o[d] = kk * a
        bonus = bonus + _head_sum(r * kd * rk, ones) * v
    bonus_o[...] = bonus


def _rwkv_in(x, g, mod, w_all, layer, p, rows):
    tm = min(RWKV_IN_ROWS, rows.ctx_len)
    d = D_MODEL
    n_all = rows.n_all
    nblk = n_all // tm
    sub_per_blk = tm // SUBLANES
    n_sub = n_all // SUBLANES
    w = RWKV_WIDTH
    cols = RKV_COLS + LORA_COLS

    def prev_idx(i):
        return (jnp.maximum(i * sub_per_blk - 1, 0), 0)

    def next_idx(i):
        return (jnp.minimum((i + 1) * sub_per_blk, n_sub - 1), 0)

    once = pl.Buffered(1)
    full = lambda shape: pl.BlockSpec(shape, lambda i: (0,) * len(shape), pipeline_mode=once)
    mod_spec = lambda which: pl.BlockSpec((None, 1, d), lambda i: (rows.mod_row(i, tm), 0, which))
    row_spec = lambda width: pl.BlockSpec((tm, width), lambda i: (i, 0))
    dir_spec = pl.BlockSpec((2, tm, w), lambda i: (0, i, 0))
    kern = functools.partial(_rwkv_in_kernel, n_lat_blocks=rows.n_lat // tm,
                             bps_lat=rows.seq // tm, bps_ctx=rows.ctx_len // tm)
    sds = jax.ShapeDtypeStruct
    return pl.pallas_call(
        kern,
        grid=(nblk,),
        in_specs=[
            row_spec(d), pl.BlockSpec((SUBLANES, d), prev_idx), pl.BlockSpec((SUBLANES, d), next_idx),
            full((1, d)), mod_spec(1), mod_spec(0),
            pl.BlockSpec((None, d, cols), lambda i: (layer, 0, 0), pipeline_mode=once),
            full((1, cols)), full((1, cols)),
            full((LORA_GATE, w)), full((LORA_MIX, 4 * w)), full((2, w)), full((2, w)),
            full((1, w)), full((1, w)), full((1, w)),
        ],
        out_specs=[row_spec(w)] * 5 + [dir_spec] * 3,
        out_shape=[sds((n_all, w), F32)] * 5 + [sds((2, n_all, w), F32)] * 3,
        compiler_params=_params(("parallel",)),
        name="rwkv_in",
    )(x, x, x, g, mod, mod, w_all, p['mu_prev'], p['mu_next'],
      p['g2'], p['wmix'], p['w0'], p['a0'], p['k_k'], p['k_a'], p['r_k'])


def _scan_kernel(r_ref, kk_ref, v_ref, lw_ref, kd_ref, bd_ref, y_ref, h_ref,
                 lincl_ref, msd_ref, mso_ref, mincl_ref, eye_ref, hm_ref, hmb_ref,
                 *, chunk, heads, groups):
    c = chunk
    wd = heads * RWKV_HEAD
    step = pl.program_id(3)

    @pl.when(step == 0)
    def _():
        h_ref[...] = jnp.zeros_like(h_ref)
        sgn = 1 - 2 * pl.program_id(0)
        iota = lambda shape, ax: lax.broadcasted_iota(jnp.int32, shape, ax)
        rel_c = (iota((c, c), 0) - iota((c, c), 1)) * sgn
        lincl_ref[...] = jnp.where(rel_c >= 0, 1.0, 0.0).astype(BF16)
        ti = iota((c, wd), 0)
        si = iota((c, wd), 1) & (c - 1)
        rel = (ti - si) * sgn
        log2b = INV_BLOCK.bit_length() - 1
        same_blk = (ti >> log2b) == (si >> log2b)
        strict = jnp.where(rel > 0, 1.0, 0.0)
        msd_ref[...] = jnp.where(same_blk, strict, 0.0)
        mso_ref[...] = jnp.where(same_blk, 0.0, strict)
        mincl_ref[...] = jnp.where(rel >= 0, 1.0, 0.0)
        eye_ref[...] = jnp.where(ti == si, 1.0, 0.0)
        hm = jnp.where((iota((wd, wd), 0) >> 6) == (iota((wd, wd), 1) >> 6), 1.0, 0.0)
        hm_ref[...] = hm
        hmb_ref[...] = hm.astype(BF16)

    n_sub = r_ref.shape[0] // c
    d = pl.program_id(0)
    rows = [pl.ds(pl.multiple_of(jnp.where(d == 0, i, n_sub - 1 - i) * c, c), c) for i in range(n_sub)]
    lanes = [slice(q * wd, (q + 1) * wd) for q in range(groups)]
    refs = (r_ref, kk_ref, v_ref, lw_ref, kd_ref, bd_ref)
    waves = [_chunk_operator_stages(*[[ref[rw, ln] for ln in lanes] for ref in refs],
                                    lincl_ref, msd_ref, mso_ref, mincl_ref, eye_ref, hm_ref, hmb_ref, heads)
             for rw in rows]
    h = [h_ref[q] for q in range(groups)]
    done = 0
    tick = 0
    while done < n_sub:
        for i in range(done, n_sub):
            if tick < i * SCAN_WAVE_SKEW:
                break
            try:
                next(waves[i])
            except StopIteration as fin:
                m_c, n_c, rhat, oloc = fin.value
                h, y = _advance(m_c, n_c, rhat, oloc, h, hmb_ref, heads)
                for q in range(groups):
                    y_ref[rows[i], lanes[q]] = y[q]
                done += 1
        tick += 1
    for q in range(groups):
        h_ref[q] = h[q]


def _advance(m_c, n_c, rhat, oloc, h, hmb_ref, heads):
    c = RWKV_HEAD
    hmb = hmb_ref[...]

    def stacked(zb):
        return jnp.concatenate([zb] * heads, axis=0) * hmb

    def step(mi, ri, hi):
        lhs = jnp.concatenate([mi, ri], axis=0).astype(BF16)
        return _dot(lhs, stacked(hi.astype(BF16)))

    res = [step(*a) for a in zip(m_c, rhat, h)]
    h_new = [z[:c] + ni for z, ni in zip(res, n_c)]
    y = [z[c:] + oi for z, oi in zip(res, oloc)]
    return h_new, y


def _chunk_operator_stages(r, kk, v, lw, k, b, lincl_ref, msd_ref, mso_ref, mincl_ref, eye_ref,
                           hm_ref, hmb_ref, heads):
    c = RWKV_HEAD
    wd = heads * RWKV_HEAD
    hmb = hmb_ref[...]
    each = lambda f, *ls: [f(*a) for a in zip(*ls)]

    def stacked(z):
        zb = z.astype(BF16)
        return jnp.concatenate([zb] * heads, axis=0) * hmb

    def mm(x, y_stacked, dims=NN):
        return _dot(x.astype(BF16), y_stacked, dims)

    def head_blocks(full):
        z = full * hm_ref[...]
        out = z[0:c]
        for hh in range(1, heads):
            out = out + z[hh * c:(hh + 1) * c]
        return out

    cat0 = lambda *xs: jnp.concatenate(xs, axis=0)
    cat1 = lambda *xs: jnp.concatenate(xs, axis=1)
    mul = lambda x, y: x * y
    sub = lambda x, y: x - y
    add = lambda x, y: x + y
    left = lambda z: z[:, :wd]
    right = lambda z: z[:, wd:]
    top = lambda z: z[:c]
    bottom = lambda z: z[c:]
    eye = eye_ref[...]
    lincl = lincl_ref[...]
    g = each(lambda x: _mm_exact_rhs_left(lincl, x), lw)
    yield
    gt = each(lambda x: jnp.sum(x, axis=0, keepdims=True), lw)
    e_x = each(lambda gi, lwi: jnp.exp(gi - lwi), g, lw)
    e_g = each(jnp.exp, g)
    e_n = each(lambda gi: jnp.exp(-gi), g)
    e_c = each(lambda gti, gi: jnp.exp(gti - gi), gt, g)
    kt = each(mul, kk, e_x)
    rt = each(mul, r, e_g)
    bh = each(mul, b, e_n)
    kh = each(mul, k, e_n)
    kb = each(mul, k, e_c)
    bb = each(mul, b, e_c)

    gm = each(lambda kti, rti, bhi, khi: mm(cat0(kti, rti), cat0(stacked(bhi), stacked(khi)).T),
              kt, rt, bh, kh)
    yield
    msd = msd_ref[...]
    mso = mso_ref[...]
    m_incl = mincl_ref[...]
    a_d = each(lambda z: z[:c, :wd] * msd, gm)
    a_o = each(lambda z: z[:c, :wd] * mso, gm)
    b_m = each(lambda z: z[:c, wd:] * (msd + mso), gm)
    e_m = each(lambda z: z[c:, :wd] * m_incl, gm)
    c_m = each(lambda z: z[c:, wd:] * m_incl, gm)

    mm_st = lambda x, y: mm(x, stacked(y))
    unzip = lambda pairs: [list(t) for t in zip(*pairs)]

    def fold(xi, di):
        z = mm(cat0(xi, di), stacked(xi))
        return z[:c], di + z[c:]

    x1 = each(lambda z: -z, a_d)
    dm = each(lambda z: eye - z, a_d)
    x2 = each(mm_st, x1, x1)
    bcv = each(lambda bi, ci, vi: mm(cat0(bi, ci), stacked(vi)), b_m, c_m, v)
    bv = each(top, bcv)
    cv = each(bottom, bcv)
    yield
    x4, dm = unzip(each(fold, x2, dm))
    yield
    x8, dm = unzip(each(fold, x4, dm))
    yield
    dm = each(lambda di, xi: di + mm_st(di, xi), dm, x8)
    yield
    n1 = each(mm_st, dm, a_o)
    yield
    n2 = each(mm_st, n1, n1)
    yield
    t1 = each(lambda z: eye - z, n1)
    t1 = each(lambda ti, ni: ti + mm_st(ti, ni), t1, n2)
    yield
    t_m = each(mm_st, t1, dm)
    yield
    et = each(mm_st, e_m, t_m)
    yield
    both = each(lambda ti, ei, kti, bvi: mm(cat0(ti, ei), cat1(stacked(kti), stacked(bvi))),
                t_m, et, kt, bv)
    yield
    wt = each(lambda z: z[:c, :wd], both)
    ut = each(lambda z: z[:c, wd:], both)
    rhat = each(lambda ri, z: ri - z[c:, :wd], rt, both)
    oloc = each(lambda ci, z: ci - z[c:, wd:], cv, both)
    full_m = each(lambda bi, wi: _dot(bi.astype(BF16), wi.astype(BF16), TN), bb, wt)
    full_n = each(lambda ki, bi, vi, ui: _dot(cat0(ki, bi).astype(BF16), cat0(vi, -ui).astype(BF16), TN),
                  kb, bb, v, ut)
    m_c = each(lambda gti, fi: eye * jnp.exp(gti) - head_blocks(fi), gt, full_m)
    n_c = each(head_blocks, full_n)
    return m_c, n_c, rhat, oloc


def _mm_exact_rhs_left(a_exact, b):
    bh = b.astype(BF16)
    r1 = b - bh.astype(F32)
    bm = r1.astype(BF16)
    bl = (r1 - bm.astype(F32)).astype(BF16)
    return _dot(a_exact, bh) + (_dot(a_exact, bm) + _dot(a_exact, bl))


def _rwkv_scan(r, kk, v, lw, kd, bd, rows):
    c, heads = SCAN_CHUNK, SCAN_HEADS
    wd = heads * RWKV_HEAD
    nq = RWKV_WIDTH // wd
    rb = c * SCAN_SUBCHUNKS
    assert rows.ctx_len % rb == 0 and rows.seq % rb == 0
    nc_ctx = rows.ctx_len // rb
    nc_lat = rows.seq // rb
    lat_blocks = rows.n_lat // rb

    def chunk_idx(d, bi, st):
        in_ctx = st < nc_ctx
        t_ctx = jnp.where(d == 0, st, nc_ctx - 1 - st)
        sl = st - nc_ctx
        t_lat = jnp.where(d == 0, sl, nc_lat - 1 - sl)
        return jnp.where(in_ctx, lat_blocks + bi * nc_ctx + t_ctx, bi * nc_lat + t_lat)

    assert c == RWKV_HEAD
    groups = SCAN_GROUPS
    bw = wd * groups
    shared = pl.BlockSpec((rb, bw), lambda d, bi, q, st: (chunk_idx(d, bi, st), q))
    per_dir = pl.BlockSpec((None, rb, bw), lambda d, bi, q, st: (d, chunk_idx(d, bi, st), q))
    return pl.pallas_call(
        functools.partial(_scan_kernel, chunk=c, heads=heads, groups=groups),
        grid=(2, rows.batch, nq // groups, nc_ctx + nc_lat),
        in_specs=[shared, shared, shared, per_dir, per_dir, per_dir],
        out_specs=per_dir,
        out_shape=jax.ShapeDtypeStruct((2, rows.n_all, RWKV_WIDTH), F32),
        scratch_shapes=[
            pltpu.VMEM((groups, c, wd), F32),
            pltpu.VMEM((c, c), BF16),
            pltpu.VMEM((c, wd), F32),
            pltpu.VMEM((c, wd), F32),
            pltpu.VMEM((c, wd), F32),
            pltpu.VMEM((c, wd), F32),
            pltpu.VMEM((wd, wd), F32),
            pltpu.VMEM((wd, wd), BF16),
        ],
        compiler_params=_params(("parallel", "parallel", "parallel", "arbitrary")),
        name="rwkv_scan",
    )(r, kk, v, lw, kd, bd)


def _out_kernel(attn_ref, y_ref, bonus_ref, g_ref, lnw_ref, lnb_ref, x_ref, gate_ref,
                wa_ref, wr_ref, o_ref):
    out_a = _dot(attn_ref[...], wa_ref[...])
    ones = _head_ones(2 * LANES)
    y = y_ref[0] + y_ref[1]
    mean = _head_sum(y, ones) * (1.0 / RWKV_HEAD)
    yc = y - mean
    var = _head_sum(yc * yc, ones) * (1.0 / RWKV_HEAD)
    yn = yc * lax.rsqrt(var + GN_EPS) * lnw_ref[...] + lnb_ref[...]
    rw = ((yn + bonus_ref[...]) * g_ref[...]).astype(BF16)
    out = out_a + _dot(rw, wr_ref[...])
    o_ref[...] = x_ref[...] + gate_ref[...] * out


def _mix_output(attn, y, bonus, g, ln_w, ln_b, x, mod, w_out, layer, rows, n_rows):
    d = D_MODEL
    tm = min(256, rows.tm)
    tn = d
    w = RWKV_WIDTH
    return pl.pallas_call(
        _out_kernel,
        grid=(n_rows // tm, d // tn),
        in_specs=[
            pl.BlockSpec((tm, Q_COLS), lambda i, j: (i, 0)),
            pl.BlockSpec((2, tm, w), lambda i, j: (0, i, 0)),
            pl.BlockSpec((tm, w), lambda i, j: (i, 0)),
            pl.BlockSpec((tm, w), lambda i, j: (i, 0)),
            pl.BlockSpec((1, w), lambda i, j: (0, 0)),
            pl.BlockSpec((1, w), lambda i, j: (0, 0)),
            pl.BlockSpec((tm, tn), lambda i, j: (i, j)),
            pl.BlockSpec((None, 1, tn), lambda i, j: (rows.mod_row(i, tm), 0, 2 * (d // tn) + j)),
            pl.BlockSpec((None, Q_COLS, tn), lambda i, j: (layer, 0, j)),
            pl.BlockSpec((None, w, tn), lambda i, j: (layer, 1, j)),
        ],
        out_specs=pl.BlockSpec((tm, tn), lambda i, j: (i, j)),
        out_shape=jax.ShapeDtypeStruct((n_rows, d), F32),
        compiler_params=_params(("parallel", "arbitrary")),
        name="mix_output",
    )(attn, y, bonus, g, ln_w, ln_b, x, mod, w_out, w_out)


def _ffn_kernel(x_ref, g_ref, sc_ref, sh_ref, gate_ref, w1_ref, w3_ref, w2_ref, fg_ref,
                o_ref, h_ref, acc_ref, *, final):
    j = pl.program_id(1)

    @pl.when(j == 0)
    def _():
        h_ref[...] = _norm_mod(x_ref[...], g_ref[...], sc_ref[...], sh_ref[...]).astype(BF16)
        acc_ref[...] = jnp.zeros_like(acc_ref)

    h = h_ref[...]
    u = _dot(h, w1_ref[...])
    a = (u * _sigmoid(u)) * _dot(h, w3_ref[...])
    acc_ref[...] += _dot(a.astype(BF16), w2_ref[...])

    @pl.when(j == pl.num_programs(1) - 1)
    def _():
        y = x_ref[...] + gate_ref[...] * acc_ref[...]
        if final:
            ms = jnp.mean(y * y, axis=-1, keepdims=True)
            y = y * lax.rsqrt(ms + NORM_EPS) * fg_ref[...]
        o_ref[...] = y


def _ffn(x, g, mod, w1, w3, w2, layer, final_g, rows, n_rows, final):
    tm, d = rows.tm, D_MODEL
    tf = 512
    return pl.pallas_call(
        functools.partial(_ffn_kernel, final=final),
        grid=(n_rows // tm, D_FF // tf),
        in_specs=[
            pl.BlockSpec((tm, d), lambda i, j: (i, 0)),
            pl.BlockSpec((1, d), lambda i, j: (0, 0)),
            _mod_spec(rows, 4, 2),
            _mod_spec(rows, 3, 2),
            _mod_spec(rows, 5, 2),
            pl.BlockSpec((None, d, tf), lambda i, j: (layer, 0, j)),
            pl.BlockSpec((None, d, tf), lambda i, j: (layer, 0, j)),
            pl.BlockSpec((None, tf, d), lambda i, j: (layer, j, 0)),
            pl.BlockSpec((1, d), lambda i, j: (0, 0)),
        ],
        out_specs=pl.BlockSpec((tm, d), lambda i, j: (i, 0)),
        out_shape=jax.ShapeDtypeStruct((n_rows, d), F32),
        scratch_shapes=[pltpu.VMEM((tm, d), BF16), pltpu.VMEM((tm, d), F32)],
        compiler_params=_params(("parallel", "arbitrary")),
        name="ffn_final" if final else "ffn",
    )(x, g, mod, mod, mod, w1, w3, w2, final_g)


def _rope_tables(seq, tm):
    n_rows = seq // GRID_W
    row = jnp.broadcast_to(jnp.arange(n_rows)[:, None], (n_rows, GRID_W)).reshape(-1)
    col = jnp.broadcast_to(jnp.arange(GRID_W)[None, :], (n_rows, GRID_W)).reshape(-1)
    inv = ROPE_THETA ** (-jnp.arange(ROPE_FREQS, dtype=F32) / ROPE_FREQS)
    ang = jnp.concatenate([row[:, None].astype(F32) * inv, col[:, None].astype(F32) * inv], axis=-1)
    cos, sin = jnp.cos(ang), jnp.sin(ang)
    cos2 = jnp.concatenate([cos, cos], axis=-1)
    sin2 = jnp.concatenate([-sin, sin], axis=-1)
    cos2 = jnp.concatenate([cos2, jnp.ones((tm, HEAD_DIM), F32)], axis=0)
    sin2 = jnp.concatenate([sin2, jnp.zeros((tm, HEAD_DIM), F32)], axis=0)
    return cos2, sin2


def _mix_weights(w2, a2):
    w = RWKV_WIDTH
    out = jnp.zeros((LORA_MIX, 4 * w), F32)
    out = out.at[0:LORA_DECAY, 0:w].set(w2[0])
    out = out.at[LORA_DECAY:2 * LORA_DECAY, w:2 * w].set(w2[1])
    o = 2 * LORA_DECAY
    out = out.at[o:o + LORA_ICLR, 2 * w:3 * w].set(a2[0])
    out = out.at[o + LORA_ICLR:o + 2 * LORA_ICLR, 3 * w:4 * w].set(a2[1])
    return out


def kernel(x, c, ctx, c_ctx, norm1_g, norm2_g, ada_w, ada_b, w_in, q_gain, k_gain, mu_prev, mu_next,
           w0, w2, a0, a2, g2, k_k, k_a, r_k, ln_x_w, ln_x_b, w_out, ffn_w1, ffn_w3, ffn_w2, final_g):
    batch, seq, d = x.shape
    ctx_len = ctx.shape[1]
    depth = w_in.shape[0]
    rows = _Rows(batch, seq, ctx_len, min(512, batch * ctx_len))
    w = RWKV_WIDTH

    cond = jnp.concatenate([c, c_ctx[None, :], jnp.zeros((SUBLANES - batch - 1, d), F32)], axis=0)
    mod_all = _modulation(cond, ada_w, ada_b)
    mod_all = mod_all.reshape(depth, SUBLANES, 1, 6 * d)
    cos2, sin2 = _rope_tables(seq, rows.tm)

    w_attn_b = w_in[:, :, :ATTN_COLS].astype(BF16)
    w_rwkv_b = w_in[:, :, ATTN_COLS:].astype(BF16)
    w_out_b = w_out.astype(BF16)
    w1_b = ffn_w1.astype(BF16)
    w3_b = ffn_w3.astype(BF16)
    w2_b = ffn_w2.astype(BF16)

    tok = jnp.concatenate([x.reshape(batch * seq, d), ctx.reshape(batch * ctx_len, d)], axis=0)
    row1 = lambda a: a.reshape(1, -1)
    for i in range(depth):
        last = i == depth - 1
        mod = mod_all[i]
        g1 = row1(norm1_g[i])
        q, k, vt = _attn_project(tok, g1, mod, w_attn_b, i, row1(q_gain[i]), row1(k_gain[i]),
                                 cos2, sin2, rows)
        attn = _attention(q, k, vt, rows, latent=True)
        if not last:
            attn = _attention(q, k, vt, rows, latent=False, out=attn)
        prep = {
            'mu_prev': row1(mu_prev[i]), 'mu_next': row1(mu_next[i]),
            'g2': g2[i].astype(BF16), 'wmix': _mix_weights(w2[i], a2[i]).astype(BF16),
            'w0': w0[i], 'a0': a0[i],
            'k_k': row1(k_k[i]), 'k_a': row1(k_a[i]), 'r_k': row1(r_k[i]),
        }
        r_s, kk_s, v_s, g_s, bonus, lw, kd, bd = _rwkv_in(tok, g1, mod, w_rwkv_b, i, prep, rows)
        y = _rwkv_scan(r_s, kk_s, v_s, lw, kd, bd, rows)
        n_rows = rows.n_lat if last else rows.n_all
        tok = _mix_output(attn, y, bonus, g_s, row1(ln_x_w[i]), row1(ln_x_b[i]), tok, mod,
                          w_out_b, i, rows, n_rows)
        tok = _ffn(tok, row1(norm2_g[i]), mod, w1_b, w3_b, w2_b, i, row1(final_g),
                   rows, n_rows, final=last)
    return tok.reshape(batch, seq, d)
```

```python
import functools

import jax
import jax.numpy as jnp
from jax import lax
from jax.experimental import pallas as pl
from jax.experimental.pallas import tpu as pltpu

F32 = jnp.float32
BF16 = jnp.bfloat16

D_MODEL = 2048
HEAD_DIM = 128
N_Q_HEADS = 8
N_KV_HEADS = 2
GQA_GROUP = N_Q_HEADS // N_KV_HEADS
Q_COLS = N_Q_HEADS * HEAD_DIM
KV_COLS = N_KV_HEADS * HEAD_DIM
ATTN_COLS = Q_COLS + 2 * KV_COLS
RWKV_WIDTH = 1024
RWKV_HEAD = 64
LORA_GATE = 256
LORA_DECAY = 96
LORA_ICLR = 96
LORA_COLS = LORA_GATE + 2 * LORA_DECAY + 2 * LORA_ICLR
LORA_PAD = 128
LORA_COLS_PADDED = LORA_GATE + 4 * LORA_PAD
RKV_COLS = 3 * RWKV_WIDTH
D_FF = 5632
GRID_W = 64
ROPE_THETA = 10000.0
ROPE_FREQS = HEAD_DIM // 4
ATTN_SCALE = HEAD_DIM ** -0.5
LOG2_E = 1.4426950408889634
NORM_EPS = 1e-6
GN_EPS = 64e-5
L2_EPS = 1e-12
DECAY_SCALE = 0.6065306597126334

LANES = 128
SUBLANES = 8
VMEM_LIMIT = 56 * 1024 * 1024

ATTN_Q_ROWS = 256
ATTN_KV_CHUNK = 1024
ATTN_LOOKAHEAD = 8
RWKV_IN_ROWS = 256
SCAN_CHUNK = 64
SCAN_HEADS = 4
INV_BLOCK = 16
SCAN_GROUPS = 4
SCAN_SUBCHUNKS = 4
SCAN_WAVE_SKEW = 2

NN = ((1,), (0,))
NT = ((1,), (1,))
TN = ((0,), (0,))


def _dot(a, b, dims=NN):
    return lax.dot_general(a, b, (dims, ((), ())), preferred_element_type=F32)


def _split(x):
    hi = x.astype(BF16)
    lo = (x - hi.astype(F32)).astype(BF16)
    return hi, lo


def _mm3(a, b, dims=NN):
    ah, al = a
    bh, bl = b
    return _dot(ah, bh, dims) + (_dot(ah, bl, dims) + _dot(al, bh, dims))


def _mm3f(a, b, dims=NN):
    return _mm3(_split(a), _split(b), dims)


def _mm_exact_rhs(a, b_exact, dims=NN):
    ah, al = _split(a)
    return _dot(ah, b_exact, dims) + _dot(al, b_exact, dims)


def _sigmoid(z):
    return 1.0 / (1.0 + jnp.exp(-z))


def _norm_mod(x, g, scale, shift):
    ms = jnp.mean(x * x, axis=-1, keepdims=True)
    y = x * lax.rsqrt(ms + NORM_EPS) * g
    return y * (1.0 + scale) + shift


def _head_ones(width):
    r = lax.broadcasted_iota(jnp.int32, (width, width), 0) >> 6
    c = lax.broadcasted_iota(jnp.int32, (width, width), 1) >> 6
    return jnp.where(r == c, 1.0, 0.0).astype(BF16)


def _head_sum(x, ones):
    w = ones.shape[0]
    xb = x.astype(BF16)
    parts = [_dot(xb[:, c:c + w], ones) for c in range(0, x.shape[1], w)]
    return jnp.concatenate(parts, axis=1)


def _params(sem):
    return pltpu.CompilerParams(dimension_semantics=sem, vmem_limit_bytes=VMEM_LIMIT)


def _mod_kernel(c_ref, w_ref, b_ref, o_ref):
    c = c_ref[...]
    s = c * _sigmoid(c)
    o_ref[...] = _mm3f(s, w_ref[...]) + b_ref[...]


def _modulation(cond, ada_w, ada_b):
    depth, d, n = ada_w.shape
    tn = 1024
    rows = cond.shape[0]
    return pl.pallas_call(
        _mod_kernel,
        grid=(depth, n // tn),
        in_specs=[
            pl.BlockSpec((rows, d), lambda l, j: (0, 0)),
            pl.BlockSpec((None, d, tn), lambda l, j: (l, 0, j)),
            pl.BlockSpec((None, 1, tn), lambda l, j: (l, 0, j)),
        ],
        out_specs=pl.BlockSpec((None, rows, tn), lambda l, j: (l, 0, j)),
        out_shape=jax.ShapeDtypeStruct((depth, rows, n), F32),
        compiler_params=_params(("parallel", "parallel")),
        name="modulation",
    )(cond, ada_w, ada_b.reshape(depth, 1, n))


class _Rows:
    def __init__(self, batch, seq, ctx_len, tm):
        self.batch, self.seq, self.ctx_len, self.tm = batch, seq, ctx_len, tm
        self.n_lat = batch * seq
        self.n_ctx = batch * ctx_len
        self.n_all = self.n_lat + self.n_ctx
        assert seq % tm == 0 and self.n_ctx % tm == 0
        self.blocks_per_batch = seq // tm

    def mod_row(self, i, tm=None):
        per_batch = self.blocks_per_batch if tm is None else self.seq // tm
        return jnp.minimum(i // per_batch, self.batch)


def _mod_spec(rows, which, ngrid):
    d = D_MODEL
    if ngrid == 1:
        return pl.BlockSpec((None, 1, d), lambda i: (rows.mod_row(i), 0, which))
    return pl.BlockSpec((None, 1, d), lambda i, j: (rows.mod_row(i), 0, which))


def _attn_proj_kernel(x_ref, g_ref, sc_ref, sh_ref, w_ref, qg_ref, kg_ref, cos_ref, sin_ref,
                      q_ref, k_ref, vt_ref):
    qg = qg_ref[...]
    kg = kg_ref[...]
    half = x_ref.shape[0] // 2
    halves = [slice(0, half), slice(half, 2 * half)]
    hs = [_norm_mod(x_ref[rs, :], g_ref[...], sc_ref[...], sh_ref[...]).astype(BF16) for rs in halves]
    fs = [_dot(h, w_ref[...]) for h in hs]
    for rs, f in zip(halves, fs):
        cos = cos_ref[rs, :]
        sin = sin_ref[rs, :]

        def head(xh, gain, scale):
            ms = jnp.mean(xh * xh, axis=-1, keepdims=True)
            y = xh * lax.rsqrt(ms + NORM_EPS) * gain
            y = y * cos + pltpu.roll(y, HEAD_DIM // 2, 1) * sin
            return (y * scale).astype(BF16)

        for hq in range(N_Q_HEADS):
            c = hq * HEAD_DIM
            q_ref[rs, c:c + HEAD_DIM] = head(f[:, c:c + HEAD_DIM], qg, ATTN_SCALE * LOG2_E)
        for hk in range(N_KV_HEADS):
            c = hk * HEAD_DIM
            k_ref[rs, c:c + HEAD_DIM] = head(f[:, Q_COLS + c:Q_COLS + c + HEAD_DIM], kg, 1.0)
        vt_ref[:, rs] = f[:, Q_COLS + KV_COLS:ATTN_COLS].T.astype(BF16)


def _attn_project(x, g, mod, w, layer, q_gain, k_gain, cos2, sin2, rows):
    tm, d = rows.tm, D_MODEL
    bpb = rows.blocks_per_batch

    def rope_idx(i):
        return (jnp.where(i < rows.n_lat // tm, i % bpb, bpb), 0)

    return pl.pallas_call(
        _attn_proj_kernel,
        grid=(rows.n_all // tm,),
        in_specs=[
            pl.BlockSpec((tm, d), lambda i: (i, 0)),
            pl.BlockSpec((1, d), lambda i: (0, 0)),
            _mod_spec(rows, 1, 1),
            _mod_spec(rows, 0, 1),
            pl.BlockSpec((None, d, ATTN_COLS), lambda i: (layer, 0, 0)),
            pl.BlockSpec((1, HEAD_DIM), lambda i: (0, 0)),
            pl.BlockSpec((1, HEAD_DIM), lambda i: (0, 0)),
            pl.BlockSpec((tm, HEAD_DIM), rope_idx),
            pl.BlockSpec((tm, HEAD_DIM), rope_idx),
        ],
        out_specs=[
            pl.BlockSpec((tm, Q_COLS), lambda i: (i, 0)),
            pl.BlockSpec((tm, KV_COLS), lambda i: (i, 0)),
            pl.BlockSpec((KV_COLS, tm), lambda i: (0, i)),
        ],
        out_shape=[
            jax.ShapeDtypeStruct((rows.n_all, Q_COLS), BF16),
            jax.ShapeDtypeStruct((rows.n_all, KV_COLS), BF16),
            jax.ShapeDtypeStruct((KV_COLS, rows.n_all), BF16),
        ],
        compiler_params=_params(("parallel",)),
        name="attn_proj",
    )(x, g, mod, mod, w, q_gain, k_gain, cos2, sin2)


def _attn_kernel(q_ref, *refs, n_kv, kv_chunk):
    k_refs = refs[:n_kv]
    vt_refs = refs[n_kv:2 * n_kv]
    o_ref = refs[-1]
    tq = q_ref.shape[0]
    heads = range(GQA_GROUP)
    q = [q_ref[:, h * HEAD_DIM:(h + 1) * HEAD_DIM] for h in heads]
    m = [jnp.full((1, tq), -jnp.inf, F32) for _ in heads]
    l = [jnp.zeros((1, tq), F32) for _ in heads]
    acc = [jnp.zeros((HEAD_DIM, tq), F32) for _ in heads]
    items = []
    for k_ref, vt_ref in zip(k_refs, vt_refs):
        n = k_ref.shape[0]
        ck = min(kv_chunk, n)
        items += [(k_ref, vt_ref, c0, ck, h) for c0 in range(0, n, ck) for h in heads]

    def scores(item):
        k_ref, _, c0, ck, h = item
        return _dot(k_ref[c0:c0 + ck, :], q[h], NT)

    ahead = min(ATTN_LOOKAHEAD, len(items))
    pending = [scores(it) for it in items[:ahead]]
    for i, (_, vt_ref, c0, ck, h) in enumerate(items):
        if i + ahead < len(items):
            pending.append(scores(items[i + ahead]))
        st = pending.pop(0)
        m_new = jnp.maximum(m[h], jnp.max(st, axis=0, keepdims=True))
        alpha = jnp.exp2(m[h] - m_new)
        p = jnp.exp2(st - m_new)
        l[h] = alpha * l[h] + jnp.sum(p, axis=0, keepdims=True)
        acc[h] = alpha * acc[h] + _dot(vt_ref[:, c0:c0 + ck], p.astype(BF16))
        m[h] = m_new
    for h in heads:
        o_ref[:, h * HEAD_DIM:(h + 1) * HEAD_DIM] = (acc[h] / l[h]).T.astype(o_ref.dtype)


def _attention(q, k, v, rows, latent, out=None):
    b, s, cl = rows.batch, rows.seq, rows.ctx_len
    gw = GQA_GROUP * HEAD_DIM
    ctx_blk0 = rows.n_lat // cl
    if latent:
        tq = min(ATTN_Q_ROWS, s)
        nq = s // tq
        q_spec = pl.BlockSpec((tq, gw), lambda bi, hi, qi: (bi * nq + qi, hi))
        k_specs = [pl.BlockSpec((s, HEAD_DIM), lambda bi, hi, qi: (bi, hi)),
                   pl.BlockSpec((cl, HEAD_DIM), lambda bi, hi, qi: (ctx_blk0 + bi, hi))]
        vt_specs = [pl.BlockSpec((HEAD_DIM, s), lambda bi, hi, qi: (hi, bi)),
                    pl.BlockSpec((HEAD_DIM, cl), lambda bi, hi, qi: (hi, ctx_blk0 + bi))]
        name = "attention_latent"
    else:
        tq = cl
        nq = 1
        q_spec = pl.BlockSpec((tq, gw), lambda bi, hi, qi: (ctx_blk0 + bi, hi))
        k_specs = [pl.BlockSpec((cl, HEAD_DIM), lambda bi, hi, qi: (ctx_blk0 + bi, hi))]
        vt_specs = [pl.BlockSpec((HEAD_DIM, cl), lambda bi, hi, qi: (hi, ctx_blk0 + bi))]
        name = "attention_context"
    n_kv = len(k_specs)
    in_specs = [q_spec] + k_specs + vt_specs
    args = [q] + [k] * n_kv + [v] * n_kv
    aliases = {}
    if out is not None:
        aliases = {len(args): 0}
        in_specs.append(pl.BlockSpec(memory_space=pl.ANY))
        args.append(out)
    return pl.pallas_call(
        functools.partial(_attn_kernel, n_kv=n_kv, kv_chunk=ATTN_KV_CHUNK),
        grid=(b, N_KV_HEADS, nq),
        in_specs=in_specs,
        out_specs=q_spec,
        out_shape=jax.ShapeDtypeStruct((rows.n_all, Q_COLS), BF16),
        input_output_aliases=aliases,
        compiler_params=_params(("parallel", "parallel", "arbitrary")),
        name=name,
    )(*args)


def _rwkv_in_kernel(x_ref, xp_ref, xn_ref, g_ref, sc_ref, sh_ref, w_ref, mup_ref, mun_ref,
                    g2_ref, wmix_ref, w0_ref, a0_ref, kk_scale_ref, ka_ref, rk_ref,
                    r_o, kk_o, v_o, g_o, bonus_o, lw_o, kd_o, bd_o,
                    *, n_lat_blocks, bps_lat, bps_ctx):
    i = pl.program_id(0)
    tm = x_ref.shape[0]
    s = SUBLANES
    is_lat = i < n_lat_blocks
    j = jnp.where(is_lat, i, i - n_lat_blocks)
    bps = jnp.where(is_lat, bps_lat, bps_ctx)
    pos = j % bps
    keep_prev = jnp.where(pos == 0, 0.0, 1.0)
    keep_next = jnp.where(pos == bps - 1, 0.0, 1.0)
    row = lax.broadcasted_iota(jnp.int32, (s, 1), 0)
    first_row = jnp.where(row == 0, keep_prev, 1.0)
    last_row = jnp.where(row == s - 1, keep_next, 1.0)

    norm = lambda ref: _norm_mod(ref[...], g_ref[...], sc_ref[...], sh_ref[...]).astype(BF16)
    h_ext = jnp.concatenate([norm(xp_ref), norm(x_ref), norm(xn_ref)], axis=0)
    n_ext = tm + 2 * s
    w = RWKV_WIDTH
    panels = [slice(0, w), slice(w, 2 * w), slice(2 * w, 3 * w), slice(3 * w, 3 * w + LORA_COLS_PADDED)]
    project = lambda c: _dot(h_ext, w_ref[:, c])

    def shifted(x, c):
        prev = pltpu.roll(x, 1, 0)[s:s + tm]
        nxt = pltpu.roll(x, n_ext - 1, 0)[s:s + tm]
        prev = jnp.concatenate([prev[:s] * first_row, prev[s:]], axis=0)
        nxt = jnp.concatenate([nxt[:tm - s], nxt[tm - s:] * last_row], axis=0)
        mu_prev = mup_ref[:, c]
        mu_next = mun_ref[:, c]
        return (1.0 - mu_prev - mu_next) * x[s:s + tm] + mu_prev * prev + mu_next * nxt

    f_next = project(panels[0])
    outs = []
    for pi, c in enumerate(panels):
        f_cur = f_next
        if pi + 1 < len(panels):
            f_next = project(panels[pi + 1])
        outs.append(shifted(f_cur, c))
    r, k, v, lora = outs

    gate_in = _sigmoid(lora[:, :LORA_GATE])
    g = _dot(gate_in.astype(BF16), g2_ref[...])
    group = lambda q: lora[:, LORA_GATE + q * LORA_PAD:LORA_GATE + (q + 1) * LORA_PAD]
    decay_lr = [_dot(jnp.tanh(group(q)).astype(BF16), wmix_ref[q]) for q in range(2)]
    iclr_lr = [_dot(group(2 + q).astype(BF16), wmix_ref[2 + q]) for q in range(2)]

    ones = _head_ones(2 * LANES)
    kk = k * kk_scale_ref[...]
    kk = kk * lax.rsqrt(_head_sum(kk * kk, ones) + L2_EPS)

    r_o[...] = r
    kk_o[...] = kk
    v_o[...] = v
    g_o[...] = g
    rk = rk_ref[...]
    ka = ka_ref[...]
    bonus = jnp.zeros_like(r)
    for d in range(2):
        z = w0_ref[d:d + 1, :] + decay_lr[d]
        lw_o[d] = -DECAY_SCALE * _sigmoid(z)
        a = _sigmoid(a0_ref[d:d + 1, :] + iclr_lr[d])
        kd = k * (1.0 + (a - 1.0) * ka)
        kd_o[d] = kd
        bd_o[d] = kk * a
        bonus = bonus + _head_sum(r * kd * rk, ones) * v
    bonus_o[...] = bonus


def _rwkv_in(x, g, mod, w_all, layer, p, rows):
    tm = min(RWKV_IN_ROWS, rows.ctx_len)
    d = D_MODEL
    n_all = rows.n_all
    nblk = n_all // tm
    sub_per_blk = tm // SUBLANES
    n_sub = n_all // SUBLANES
    w = RWKV_WIDTH
    cols = RKV_COLS + LORA_COLS_PADDED

    def prev_idx(i):
        return (jnp.maximum(i * sub_per_blk - 1, 0), 0)

    def next_idx(i):
        return (jnp.minimum((i + 1) * sub_per_blk, n_sub - 1), 0)

    once = pl.Buffered(1)
    full = lambda shape: pl.BlockSpec(shape, lambda i: (0,) * len(shape), pipeline_mode=once)
    mod_spec = lambda which: pl.BlockSpec((None, 1, d), lambda i: (rows.mod_row(i, tm), 0, which))
    row_spec = lambda width: pl.BlockSpec((tm, width), lambda i: (i, 0))
    dir_spec = pl.BlockSpec((2, tm, w), lambda i: (0, i, 0))
    kern = functools.partial(_rwkv_in_kernel, n_lat_blocks=rows.n_lat // tm,
                             bps_lat=rows.seq // tm, bps_ctx=rows.ctx_len // tm)
    sds = jax.ShapeDtypeStruct
    return pl.pallas_call(
        kern,
        grid=(nblk,),
        in_specs=[
            row_spec(d), pl.BlockSpec((SUBLANES, d), prev_idx), pl.BlockSpec((SUBLANES, d), next_idx),
            full((1, d)), mod_spec(1), mod_spec(0),
            pl.BlockSpec((None, d, cols), lambda i: (layer, 0, 0), pipeline_mode=once),
            full((1, cols)), full((1, cols)),
            full((LORA_GATE, w)), full((4, LORA_PAD, w)), full((2, w)), full((2, w)),
            full((1, w)), full((1, w)), full((1, w)),
        ],
        out_specs=[row_spec(w)] * 5 + [dir_spec] * 3,
        out_shape=[sds((n_all, w), F32)] * 5 + [sds((2, n_all, w), F32)] * 3,
        compiler_params=_params(("parallel",)),
        name="rwkv_in",
    )(x, x, x, g, mod, mod, w_all, p['mu_prev'], p['mu_next'],
      p['g2'], p['wmix'], p['w0'], p['a0'], p['k_k'], p['k_a'], p['r_k'])


def _scan_kernel(r_ref, kk_ref, v_ref, lw_ref, kd_ref, bd_ref, y_ref, h_ref,
                 lincl_ref, msd_ref, mso_ref, mincl_ref, eye_ref, hm_ref, hmb_ref,
                 *, chunk, heads, groups):
    c = chunk
    wd = heads * RWKV_HEAD
    step = pl.program_id(3)

    @pl.when(step == 0)
    def _():
        h_ref[...] = jnp.zeros_like(h_ref)
        sgn = 1 - 2 * pl.program_id(0)
        iota = lambda shape, ax: lax.broadcasted_iota(jnp.int32, shape, ax)
        rel_c = (iota((c, c), 0) - iota((c, c), 1)) * sgn
        lincl_ref[...] = jnp.where(rel_c >= 0, 1.0, 0.0).astype(BF16)
        ti = iota((c, wd), 0)
        si = iota((c, wd), 1) & (c - 1)
        rel = (ti - si) * sgn
        log2b = INV_BLOCK.bit_length() - 1
        same_blk = (ti >> log2b) == (si >> log2b)
        strict = jnp.where(rel > 0, 1.0, 0.0)
        msd_ref[...] = jnp.where(same_blk, strict, 0.0)
        mso_ref[...] = jnp.where(same_blk, 0.0, strict)
        mincl_ref[...] = jnp.where(rel >= 0, 1.0, 0.0)
        eye_ref[...] = jnp.where(ti == si, 1.0, 0.0)
        hm = jnp.where((iota((wd, wd), 0) >> 6) == (iota((wd, wd), 1) >> 6), 1.0, 0.0)
        hm_ref[...] = hm
        hmb_ref[...] = hm.astype(BF16)

    n_sub = r_ref.shape[0] // c
    d = pl.program_id(0)
    rows = [pl.ds(pl.multiple_of(jnp.where(d == 0, i, n_sub - 1 - i) * c, c), c) for i in range(n_sub)]
    lanes = [slice(q * wd, (q + 1) * wd) for q in range(groups)]
    refs = (r_ref, kk_ref, v_ref, lw_ref, kd_ref, bd_ref)
    waves = [_chunk_operator_stages(*[[ref[rw, ln] for ln in lanes] for ref in refs],
                                    lincl_ref, msd_ref, mso_ref, mincl_ref, eye_ref, hm_ref, hmb_ref, heads)
             for rw in rows]
    h = [h_ref[q] for q in range(groups)]
    done = 0
    tick = 0
    while done < n_sub:
        for i in range(done, n_sub):
            if tick < i * SCAN_WAVE_SKEW:
                break
            try:
                next(waves[i])
            except StopIteration as fin:
                m_c, n_c, rhat, oloc = fin.value
                h, y = _advance(m_c, n_c, rhat, oloc, h, hmb_ref, heads)
                for q in range(groups):
                    y_ref[rows[i], lanes[q]] = y[q]
                done += 1
        tick += 1
    for q in range(groups):
        h_ref[q] = h[q]


def _advance(m_c, n_c, rhat, oloc, h, hmb_ref, heads):
    c = RWKV_HEAD
    hmb = hmb_ref[...]

    def stacked(zb):
        return jnp.concatenate([zb] * heads, axis=0) * hmb

    def step(mi, ri, hi):
        lhs = jnp.concatenate([mi, ri], axis=0).astype(BF16)
        return _dot(lhs, stacked(hi.astype(BF16)))

    res = [step(*a) for a in zip(m_c, rhat, h)]
    h_new = [z[:c] + ni for z, ni in zip(res, n_c)]
    y = [z[c:] + oi for z, oi in zip(res, oloc)]
    return h_new, y


def _chunk_operator_stages(r, kk, v, lw, k, b, lincl_ref, msd_ref, mso_ref, mincl_ref, eye_ref,
                           hm_ref, hmb_ref, heads):
    c = RWKV_HEAD
    wd = heads * RWKV_HEAD
    hmb = hmb_ref[...]
    each = lambda f, *ls: [f(*a) for a in zip(*ls)]

    def stacked(z):
        zb = z.astype(BF16)
        return jnp.concatenate([zb] * heads, axis=0) * hmb

    def mm(x, y_stacked, dims=NN):
        return _dot(x.astype(BF16), y_stacked, dims)

    def head_blocks(full):
        z = full * hm_ref[...]
        out = z[0:c]
        for hh in range(1, heads):
            out = out + z[hh * c:(hh + 1) * c]
        return out

    cat0 = lambda *xs: jnp.concatenate(xs, axis=0)
    cat1 = lambda *xs: jnp.concatenate(xs, axis=1)
    mul = lambda x, y: x * y
    sub = lambda x, y: x - y
    add = lambda x, y: x + y
    left = lambda z: z[:, :wd]
    right = lambda z: z[:, wd:]
    top = lambda z: z[:c]
    bottom = lambda z: z[c:]
    eye = eye_ref[...]
    lincl = lincl_ref[...]
    g = each(lambda x: _mm_exact_rhs_left(lincl, x), lw)
    yield
    gt = each(lambda x: jnp.sum(x, axis=0, keepdims=True), lw)
    e_x = each(lambda gi, lwi: jnp.exp(gi - lwi), g, lw)
    e_g = each(jnp.exp, g)
    e_n = each(lambda gi: jnp.exp(-gi), g)
    e_c = each(lambda gti, gi: jnp.exp(gti - gi), gt, g)
    kt = each(mul, kk, e_x)
    rt = each(mul, r, e_g)
    bh = each(mul, b, e_n)
    kh = each(mul, k, e_n)
    kb = each(mul, k, e_c)
    bb = each(mul, b, e_c)

    gm = each(lambda kti, rti, bhi, khi: mm(cat0(kti, rti), cat0(stacked(bhi), stacked(khi)).T),
              kt, rt, bh, kh)
    yield
    msd = msd_ref[...]
    mso = mso_ref[...]
    m_incl = mincl_ref[...]
    a_d = each(lambda z: z[:c, :wd] * msd, gm)
    a_o = each(lambda z: z[:c, :wd] * mso, gm)
    b_m = each(lambda z: z[:c, wd:] * (msd + mso), gm)
    e_m = each(lambda z: z[c:, :wd] * m_incl, gm)
    c_m = each(lambda z: z[c:, wd:] * m_incl, gm)

    mm_st = lambda x, y: mm(x, stacked(y))
    unzip = lambda pairs: [list(t) for t in zip(*pairs)]

    def fold(xi, di):
        z = mm(cat0(xi, di), stacked(xi))
        return z[:c], di + z[c:]

    x1 = each(lambda z: -z, a_d)
    dm = each(lambda z: eye - z, a_d)
    x2 = each(mm_st, x1, x1)
    bcv = each(lambda bi, ci, vi: mm(cat0(bi, ci), stacked(vi)), b_m, c_m, v)
    bv = each(top, bcv)
    cv = each(bottom, bcv)
    yield
    x4, dm = unzip(each(fold, x2, dm))
    yield
    x8, dm = unzip(each(fold, x4, dm))
    yield
    dm = each(lambda di, xi: di + mm_st(di, xi), dm, x8)
    yield
    n1 = each(mm_st, dm, a_o)
    yield
    n2 = each(mm_st, n1, n1)
    yield
    t1 = each(lambda z: eye - z, n1)
    t1 = each(lambda ti, ni: ti + mm_st(ti, ni), t1, n2)
    yield
    t_m = each(mm_st, t1, dm)
    yield
    et = each(mm_st, e_m, t_m)
    yield
    both = each(lambda ti, ei, kti, bvi: mm(cat0(ti, ei), cat1(stacked(kti), stacked(bvi))),
                t_m, et, kt, bv)
    yield
    wt = each(lambda z: z[:c, :wd], both)
    ut = each(lambda z: z[:c, wd:], both)
    rhat = each(lambda ri, z: ri - z[c:, :wd], rt, both)
    oloc = each(lambda ci, z: ci - z[c:, wd:], cv, both)
    full_m = each(lambda bi, wi: _dot(bi.astype(BF16), wi.astype(BF16), TN), bb, wt)
    full_n = each(lambda ki, bi, vi, ui: _dot(cat0(ki, bi).astype(BF16), cat0(vi, -ui).astype(BF16), TN),
                  kb, bb, v, ut)
    m_c = each(lambda gti, fi: eye * jnp.exp(gti) - head_blocks(fi), gt, full_m)
    n_c = each(head_blocks, full_n)
    return m_c, n_c, rhat, oloc


def _mm_exact_rhs_left(a_exact, b):
    bh, bl = _split(b)
    return _dot(a_exact, bh) + _dot(a_exact, bl)


def _rwkv_scan(r, kk, v, lw, kd, bd, rows):
    c, heads = SCAN_CHUNK, SCAN_HEADS
    wd = heads * RWKV_HEAD
    nq = RWKV_WIDTH // wd
    rb = c * SCAN_SUBCHUNKS
    assert rows.ctx_len % rb == 0 and rows.seq % rb == 0
    nc_ctx = rows.ctx_len // rb
    nc_lat = rows.seq // rb
    lat_blocks = rows.n_lat // rb

    def chunk_idx(d, bi, st):
        in_ctx = st < nc_ctx
        t_ctx = jnp.where(d == 0, st, nc_ctx - 1 - st)
        sl = st - nc_ctx
        t_lat = jnp.where(d == 0, sl, nc_lat - 1 - sl)
        return jnp.where(in_ctx, lat_blocks + bi * nc_ctx + t_ctx, bi * nc_lat + t_lat)

    assert c == RWKV_HEAD
    groups = SCAN_GROUPS
    bw = wd * groups
    shared = pl.BlockSpec((rb, bw), lambda d, bi, q, st: (chunk_idx(d, bi, st), q))
    per_dir = pl.BlockSpec((None, rb, bw), lambda d, bi, q, st: (d, chunk_idx(d, bi, st), q))
    return pl.pallas_call(
        functools.partial(_scan_kernel, chunk=c, heads=heads, groups=groups),
        grid=(2, rows.batch, nq // groups, nc_ctx + nc_lat),
        in_specs=[shared, shared, shared, per_dir, per_dir, per_dir],
        out_specs=per_dir,
        out_shape=jax.ShapeDtypeStruct((2, rows.n_all, RWKV_WIDTH), F32),
        scratch_shapes=[
            pltpu.VMEM((groups, c, wd), F32),
            pltpu.VMEM((c, c), BF16),
            pltpu.VMEM((c, wd), F32),
            pltpu.VMEM((c, wd), F32),
            pltpu.VMEM((c, wd), F32),
            pltpu.VMEM((c, wd), F32),
            pltpu.VMEM((wd, wd), F32),
            pltpu.VMEM((wd, wd), BF16),
        ],
        compiler_params=_params(("parallel", "parallel", "parallel", "arbitrary")),
        name="rwkv_scan",
    )(r, kk, v, lw, kd, bd)


def _out_kernel(attn_ref, y_ref, bonus_ref, g_ref, lnw_ref, lnb_ref, x_ref, gate_ref,
                wa_ref, wr_ref, o_ref):
    out_a = _dot(attn_ref[...], wa_ref[...])
    ones = _head_ones(2 * LANES)
    y = y_ref[0] + y_ref[1]
    mean = _head_sum(y, ones) * (1.0 / RWKV_HEAD)
    yc = y - mean
    var = _head_sum(yc * yc, ones) * (1.0 / RWKV_HEAD)
    yn = yc * lax.rsqrt(var + GN_EPS) * lnw_ref[...] + lnb_ref[...]
    rw = ((yn + bonus_ref[...]) * g_ref[...]).astype(BF16)
    out = out_a + _dot(rw, wr_ref[...])
    o_ref[...] = x_ref[...] + gate_ref[...] * out


def _mix_output(attn, y, bonus, g, ln_w, ln_b, x, mod, w_out, layer, rows, n_rows):
    d = D_MODEL
    tm = min(256, rows.tm)
    tn = d
    w = RWKV_WIDTH
    return pl.pallas_call(
        _out_kernel,
        grid=(n_rows // tm, d // tn),
        in_specs=[
            pl.BlockSpec((tm, Q_COLS), lambda i, j: (i, 0)),
            pl.BlockSpec((2, tm, w), lambda i, j: (0, i, 0)),
            pl.BlockSpec((tm, w), lambda i, j: (i, 0)),
            pl.BlockSpec((tm, w), lambda i, j: (i, 0)),
            pl.BlockSpec((1, w), lambda i, j: (0, 0)),
            pl.BlockSpec((1, w), lambda i, j: (0, 0)),
            pl.BlockSpec((tm, tn), lambda i, j: (i, j)),
            pl.BlockSpec((None, 1, tn), lambda i, j: (rows.mod_row(i, tm), 0, 2 * (d // tn) + j)),
            pl.BlockSpec((None, Q_COLS, tn), lambda i, j: (layer, 0, j)),
            pl.BlockSpec((None, w, tn), lambda i, j: (layer, 1, j)),
        ],
        out_specs=pl.BlockSpec((tm, tn), lambda i, j: (i, j)),
        out_shape=jax.ShapeDtypeStruct((n_rows, d), F32),
        compiler_params=_params(("parallel", "arbitrary")),
        name="mix_output",
    )(attn, y, bonus, g, ln_w, ln_b, x, mod, w_out, w_out)


def _ffn_kernel(x_ref, g_ref, sc_ref, sh_ref, gate_ref, w1_ref, w3_ref, w2_ref, fg_ref,
                o_ref, h_ref, acc_ref, *, final):
    j = pl.program_id(1)

    @pl.when(j == 0)
    def _():
        h_ref[...] = _norm_mod(x_ref[...], g_ref[...], sc_ref[...], sh_ref[...]).astype(BF16)
        acc_ref[...] = jnp.zeros_like(acc_ref)

    h = h_ref[...]
    u = _dot(h, w1_ref[...])
    a = (u * _sigmoid(u)) * _dot(h, w3_ref[...])
    acc_ref[...] += _dot(a.astype(BF16), w2_ref[...])

    @pl.when(j == pl.num_programs(1) - 1)
    def _():
        y = x_ref[...] + gate_ref[...] * acc_ref[...]
        if final:
            ms = jnp.mean(y * y, axis=-1, keepdims=True)
            y = y * lax.rsqrt(ms + NORM_EPS) * fg_ref[...]
        o_ref[...] = y


def _ffn(x, g, mod, w1, w3, w2, layer, final_g, rows, n_rows, final):
    tm, d = rows.tm, D_MODEL
    tf = 512
    return pl.pallas_call(
        functools.partial(_ffn_kernel, final=final),
        grid=(n_rows // tm, D_FF // tf),
        in_specs=[
            pl.BlockSpec((tm, d), lambda i, j: (i, 0)),
            pl.BlockSpec((1, d), lambda i, j: (0, 0)),
            _mod_spec(rows, 4, 2),
            _mod_spec(rows, 3, 2),
            _mod_spec(rows, 5, 2),
            pl.BlockSpec((None, d, tf), lambda i, j: (layer, 0, j)),
            pl.BlockSpec((None, d, tf), lambda i, j: (layer, 0, j)),
            pl.BlockSpec((None, tf, d), lambda i, j: (layer, j, 0)),
            pl.BlockSpec((1, d), lambda i, j: (0, 0)),
        ],
        out_specs=pl.BlockSpec((tm, d), lambda i, j: (i, 0)),
        out_shape=jax.ShapeDtypeStruct((n_rows, d), F32),
        scratch_shapes=[pltpu.VMEM((tm, d), BF16), pltpu.VMEM((tm, d), F32)],
        compiler_params=_params(("parallel", "arbitrary")),
        name="ffn_final" if final else "ffn",
    )(x, g, mod, mod, mod, w1, w3, w2, final_g)


def _rope_tables(seq, tm):
    n_rows = seq // GRID_W
    row = jnp.broadcast_to(jnp.arange(n_rows)[:, None], (n_rows, GRID_W)).reshape(-1)
    col = jnp.broadcast_to(jnp.arange(GRID_W)[None, :], (n_rows, GRID_W)).reshape(-1)
    inv = ROPE_THETA ** (-jnp.arange(ROPE_FREQS, dtype=F32) / ROPE_FREQS)
    ang = jnp.concatenate([row[:, None].astype(F32) * inv, col[:, None].astype(F32) * inv], axis=-1)
    cos, sin = jnp.cos(ang), jnp.sin(ang)
    cos2 = jnp.concatenate([cos, cos], axis=-1)
    sin2 = jnp.concatenate([-sin, sin], axis=-1)
    cos2 = jnp.concatenate([cos2, jnp.ones((tm, HEAD_DIM), F32)], axis=0)
    sin2 = jnp.concatenate([sin2, jnp.zeros((tm, HEAD_DIM), F32)], axis=0)
    return cos2, sin2


def _pad_lora_cols(a):
    parts = [a[..., :LORA_GATE]]
    pad = [(0, 0)] * (a.ndim - 1) + [(0, LORA_PAD - LORA_DECAY)]
    for q in range(4):
        lo = LORA_GATE + q * LORA_DECAY
        parts.append(jnp.pad(a[..., lo:lo + LORA_DECAY], pad))
    return jnp.concatenate(parts, axis=-1)


def _mix_weights(w2, a2):
    mats = jnp.concatenate([w2, a2], axis=0)
    return jnp.pad(mats, ((0, 0), (0, LORA_PAD - LORA_DECAY), (0, 0)))


def kernel(x, c, ctx, c_ctx, norm1_g, norm2_g, ada_w, ada_b, w_in, q_gain, k_gain, mu_prev, mu_next,
           w0, w2, a0, a2, g2, k_k, k_a, r_k, ln_x_w, ln_x_b, w_out, ffn_w1, ffn_w3, ffn_w2, final_g):
    batch, seq, d = x.shape
    ctx_len = ctx.shape[1]
    depth = w_in.shape[0]
    rows = _Rows(batch, seq, ctx_len, min(512, batch * ctx_len))
    w = RWKV_WIDTH

    cond = jnp.concatenate([c, c_ctx[None, :], jnp.zeros((SUBLANES - batch - 1, d), F32)], axis=0)
    mod_all = _modulation(cond, ada_w, ada_b)
    mod_all = mod_all.reshape(depth, SUBLANES, 1, 6 * d)
    cos2, sin2 = _rope_tables(seq, rows.tm)

    w_attn_b = w_in[:, :, :ATTN_COLS].astype(BF16)
    split = ATTN_COLS + RKV_COLS
    w_rwkv_b = jnp.concatenate([w_in[:, :, ATTN_COLS:split], _pad_lora_cols(w_in[:, :, split:])],
                               axis=-1).astype(BF16)
    pad_mu = lambda mu: jnp.concatenate([mu[:, :RKV_COLS], _pad_lora_cols(mu[:, RKV_COLS:])], axis=-1)
    mu_prev_p = pad_mu(mu_prev)
    mu_next_p = pad_mu(mu_next)
    w_out_b = w_out.astype(BF16)
    w1_b = ffn_w1.astype(BF16)
    w3_b = ffn_w3.astype(BF16)
    w2_b = ffn_w2.astype(BF16)

    tok = jnp.concatenate([x.reshape(batch * seq, d), ctx.reshape(batch * ctx_len, d)], axis=0)
    row1 = lambda a: a.reshape(1, -1)
    for i in range(depth):
        last = i == depth - 1
        mod = mod_all[i]
        g1 = row1(norm1_g[i])
        q, k, vt = _attn_project(tok, g1, mod, w_attn_b, i, row1(q_gain[i]), row1(k_gain[i]),
                                 cos2, sin2, rows)
        attn = _attention(q, k, vt, rows, latent=True)
        if not last:
            attn = _attention(q, k, vt, rows, latent=False, out=attn)
        prep = {
            'mu_prev': row1(mu_prev_p[i]), 'mu_next': row1(mu_next_p[i]),
            'g2': g2[i].astype(BF16), 'wmix': _mix_weights(w2[i], a2[i]).astype(BF16),
            'w0': w0[i], 'a0': a0[i],
            'k_k': row1(k_k[i]), 'k_a': row1(k_a[i]), 'r_k': row1(r_k[i]),
        }
        r_s, kk_s, v_s, g_s, bonus, lw, kd, bd = _rwkv_in(tok, g1, mod, w_rwkv_b, i, prep, rows)
        y = _rwkv_scan(r_s, kk_s, v_s, lw, kd, bd, rows)
        n_rows = rows.n_lat if last else rows.n_all
        tok = _mix_output(attn, y, bonus, g_s, row1(ln_x_w[i]), row1(ln_x_b[i]), tok, mod,
                          w_out_b, i, rows, n_rows)
        tok = _ffn(tok, row1(norm2_g[i]), mod, w1_b, w3_b, w2_b, i, row1(final_g),
                   rows, n_rows, final=last)
    return tok.reshape(batch, seq, d)
```

```python
import functools

import jax
import jax.numpy as jnp
from jax import lax
from jax.experimental import pallas as pl
from jax.experimental.pallas import tpu as pltpu

F32 = jnp.float32
BF16 = jnp.bfloat16

D_MODEL = 2048
HEAD_DIM = 128
N_Q_HEADS = 8
N_KV_HEADS = 2
GQA_GROUP = N_Q_HEADS // N_KV_HEADS
Q_COLS = N_Q_HEADS * HEAD_DIM
KV_COLS = N_KV_HEADS * HEAD_DIM
ATTN_COLS = Q_COLS + 2 * KV_COLS
RWKV_WIDTH = 1024
RWKV_HEAD = 64
LORA_GATE = 256
LORA_DECAY = 96
LORA_ICLR = 96
LORA_COLS = LORA_GATE + 2 * LORA_DECAY + 2 * LORA_ICLR
LORA_PAD = 128
LORA_COLS_PADDED = LORA_GATE + 4 * LORA_PAD
RKV_COLS = 3 * RWKV_WIDTH
D_FF = 5632
GRID_W = 64
ROPE_THETA = 10000.0
ROPE_FREQS = HEAD_DIM // 4
ATTN_SCALE = HEAD_DIM ** -0.5
LOG2_E = 1.4426950408889634
NORM_EPS = 1e-6
GN_EPS = 64e-5
L2_EPS = 1e-12
DECAY_SCALE = 0.6065306597126334

LANES = 128
SUBLANES = 8
VMEM_LIMIT = 56 * 1024 * 1024

ATTN_Q_ROWS = 256
ATTN_KV_CHUNK = 1024
ATTN_LOOKAHEAD = 8
RWKV_IN_ROWS = 256
SCAN_CHUNK = 64
SCAN_HEADS = 4
INV_BLOCK = 16
SCAN_GROUPS = 4
SCAN_SUBCHUNKS = 4
SCAN_WAVE_SKEW = 1

NN = ((1,), (0,))
NT = ((1,), (1,))
TN = ((0,), (0,))


def _dot(a, b, dims=NN):
    return lax.dot_general(a, b, (dims, ((), ())), preferred_element_type=F32)


def _split(x):
    hi = x.astype(BF16)
    lo = (x - hi.astype(F32)).astype(BF16)
    return hi, lo


def _mm3(a, b, dims=NN):
    ah, al = a
    bh, bl = b
    return _dot(ah, bh, dims) + (_dot(ah, bl, dims) + _dot(al, bh, dims))


def _mm3f(a, b, dims=NN):
    return _mm3(_split(a), _split(b), dims)


def _mm_exact_rhs(a, b_exact, dims=NN):
    ah, al = _split(a)
    return _dot(ah, b_exact, dims) + _dot(al, b_exact, dims)


def _sigmoid(z):
    return 1.0 / (1.0 + jnp.exp(-z))


def _norm_mod(x, g, scale, shift):
    ms = jnp.mean(x * x, axis=-1, keepdims=True)
    y = x * lax.rsqrt(ms + NORM_EPS) * g
    return y * (1.0 + scale) + shift


def _head_ones(width):
    r = lax.broadcasted_iota(jnp.int32, (width, width), 0) >> 6
    c = lax.broadcasted_iota(jnp.int32, (width, width), 1) >> 6
    return jnp.where(r == c, 1.0, 0.0).astype(BF16)


def _head_sum(x, ones):
    w = ones.shape[0]
    xb = x.astype(BF16)
    parts = [_dot(xb[:, c:c + w], ones) for c in range(0, x.shape[1], w)]
    return jnp.concatenate(parts, axis=1)


def _params(sem):
    return pltpu.CompilerParams(dimension_semantics=sem, vmem_limit_bytes=VMEM_LIMIT)


def _mod_kernel(c_ref, w_ref, b_ref, o_ref):
    c = c_ref[...]
    s = c * _sigmoid(c)
    o_ref[...] = _mm3f(s, w_ref[...]) + b_ref[...]


def _modulation(cond, ada_w, ada_b):
    depth, d, n = ada_w.shape
    tn = 1024
    rows = cond.shape[0]
    return pl.pallas_call(
        _mod_kernel,
        grid=(depth, n // tn),
        in_specs=[
            pl.BlockSpec((rows, d), lambda l, j: (0, 0)),
            pl.BlockSpec((None, d, tn), lambda l, j: (l, 0, j)),
            pl.BlockSpec((None, 1, tn), lambda l, j: (l, 0, j)),
        ],
        out_specs=pl.BlockSpec((None, rows, tn), lambda l, j: (l, 0, j)),
        out_shape=jax.ShapeDtypeStruct((depth, rows, n), F32),
        compiler_params=_params(("parallel", "parallel")),
        name="modulation",
    )(cond, ada_w, ada_b.reshape(depth, 1, n))


class _Rows:
    def __init__(self, batch, seq, ctx_len, tm):
        self.batch, self.seq, self.ctx_len, self.tm = batch, seq, ctx_len, tm
        self.n_lat = batch * seq
        self.n_ctx = batch * ctx_len
        self.n_all = self.n_lat + self.n_ctx
        assert seq % tm == 0 and self.n_ctx % tm == 0
        self.blocks_per_batch = seq // tm

    def mod_row(self, i, tm=None):
        per_batch = self.blocks_per_batch if tm is None else self.seq // tm
        return jnp.minimum(i // per_batch, self.batch)


def _mod_spec(rows, which, ngrid):
    d = D_MODEL
    if ngrid == 1:
        return pl.BlockSpec((None, 1, d), lambda i: (rows.mod_row(i), 0, which))
    return pl.BlockSpec((None, 1, d), lambda i, j: (rows.mod_row(i), 0, which))


def _attn_proj_kernel(x_ref, g_ref, sc_ref, sh_ref, w_ref, qg_ref, kg_ref, cos_ref, sin_ref,
                      q_ref, k_ref, vt_ref):
    qg = qg_ref[...]
    kg = kg_ref[...]
    half = x_ref.shape[0] // 2
    halves = [slice(0, half), slice(half, 2 * half)]
    hs = [_norm_mod(x_ref[rs, :], g_ref[...], sc_ref[...], sh_ref[...]).astype(BF16) for rs in halves]
    fs = [_dot(h, w_ref[...]) for h in hs]
    for rs, f in zip(halves, fs):
        cos = cos_ref[rs, :]
        sin = sin_ref[rs, :]

        def head(xh, gain, scale):
            ms = jnp.mean(xh * xh, axis=-1, keepdims=True)
            y = xh * lax.rsqrt(ms + NORM_EPS) * gain
            y = y * cos + pltpu.roll(y, HEAD_DIM // 2, 1) * sin
            return (y * scale).astype(BF16)

        for hq in range(N_Q_HEADS):
            c = hq * HEAD_DIM
            q_ref[rs, c:c + HEAD_DIM] = head(f[:, c:c + HEAD_DIM], qg, ATTN_SCALE * LOG2_E)
        for hk in range(N_KV_HEADS):
            c = hk * HEAD_DIM
            k_ref[rs, c:c + HEAD_DIM] = head(f[:, Q_COLS + c:Q_COLS + c + HEAD_DIM], kg, 1.0)
        vt_ref[:, rs] = f[:, Q_COLS + KV_COLS:ATTN_COLS].T.astype(BF16)


def _attn_project(x, g, mod, w, layer, q_gain, k_gain, cos2, sin2, rows):
    tm, d = rows.tm, D_MODEL
    bpb = rows.blocks_per_batch

    def rope_idx(i):
        return (jnp.where(i < rows.n_lat // tm, i % bpb, bpb), 0)

    return pl.pallas_call(
        _attn_proj_kernel,
        grid=(rows.n_all // tm,),
        in_specs=[
            pl.BlockSpec((tm, d), lambda i: (i, 0)),
            pl.BlockSpec((1, d), lambda i: (0, 0)),
            _mod_spec(rows, 1, 1),
            _mod_spec(rows, 0, 1),
            pl.BlockSpec((None, d, ATTN_COLS), lambda i: (layer, 0, 0)),
            pl.BlockSpec((1, HEAD_DIM), lambda i: (0, 0)),
            pl.BlockSpec((1, HEAD_DIM), lambda i: (0, 0)),
            pl.BlockSpec((tm, HEAD_DIM), rope_idx),
            pl.BlockSpec((tm, HEAD_DIM), rope_idx),
        ],
        out_specs=[
            pl.BlockSpec((tm, Q_COLS), lambda i: (i, 0)),
            pl.BlockSpec((tm, KV_COLS), lambda i: (i, 0)),
            pl.BlockSpec((KV_COLS, tm), lambda i: (0, i)),
        ],
        out_shape=[
            jax.ShapeDtypeStruct((rows.n_all, Q_COLS), BF16),
            jax.ShapeDtypeStruct((rows.n_all, KV_COLS), BF16),
            jax.ShapeDtypeStruct((KV_COLS, rows.n_all), BF16),
        ],
        compiler_params=_params(("parallel",)),
        name="attn_proj",
    )(x, g, mod, mod, w, q_gain, k_gain, cos2, sin2)


def _attn_kernel(q_ref, *refs, n_kv, kv_chunk):
    k_refs = refs[:n_kv]
    vt_refs = refs[n_kv:2 * n_kv]
    o_ref = refs[-1]
    tq = q_ref.shape[0]
    heads = range(GQA_GROUP)
    q = [q_ref[:, h * HEAD_DIM:(h + 1) * HEAD_DIM] for h in heads]
    m = [jnp.full((1, tq), -jnp.inf, F32) for _ in heads]
    l = [jnp.zeros((1, tq), F32) for _ in heads]
    acc = [jnp.zeros((HEAD_DIM, tq), F32) for _ in heads]
    items = []
    for k_ref, vt_ref in zip(k_refs, vt_refs):
        n = k_ref.shape[0]
        ck = min(kv_chunk, n)
        items += [(k_ref, vt_ref, c0, ck, h) for c0 in range(0, n, ck) for h in heads]

    def scores(item):
        k_ref, _, c0, ck, h = item
        return _dot(k_ref[c0:c0 + ck, :], q[h], NT)

    ahead = min(ATTN_LOOKAHEAD, len(items))
    pending = [scores(it) for it in items[:ahead]]
    for i, (_, vt_ref, c0, ck, h) in enumerate(items):
        if i + ahead < len(items):
            pending.append(scores(items[i + ahead]))
        st = pending.pop(0)
        m_new = jnp.maximum(m[h], jnp.max(st, axis=0, keepdims=True))
        alpha = jnp.exp2(m[h] - m_new)
        p = jnp.exp2(st - m_new)
        l[h] = alpha * l[h] + jnp.sum(p, axis=0, keepdims=True)
        acc[h] = alpha * acc[h] + _dot(vt_ref[:, c0:c0 + ck], p.astype(BF16))
        m[h] = m_new
    for h in heads:
        o_ref[:, h * HEAD_DIM:(h + 1) * HEAD_DIM] = (acc[h] / l[h]).T.astype(o_ref.dtype)


def _attention(q, k, v, rows, latent, out=None):
    b, s, cl = rows.batch, rows.seq, rows.ctx_len
    gw = GQA_GROUP * HEAD_DIM
    ctx_blk0 = rows.n_lat // cl
    if latent:
        tq = min(ATTN_Q_ROWS, s)
        nq = s // tq
        q_spec = pl.BlockSpec((tq, gw), lambda bi, hi, qi: (bi * nq + qi, hi))
        k_specs = [pl.BlockSpec((s, HEAD_DIM), lambda bi, hi, qi: (bi, hi)),
                   pl.BlockSpec((cl, HEAD_DIM), lambda bi, hi, qi: (ctx_blk0 + bi, hi))]
        vt_specs = [pl.BlockSpec((HEAD_DIM, s), lambda bi, hi, qi: (hi, bi)),
                    pl.BlockSpec((HEAD_DIM, cl), lambda bi, hi, qi: (hi, ctx_blk0 + bi))]
        name = "attention_latent"
    else:
        tq = cl
        nq = 1
        q_spec = pl.BlockSpec((tq, gw), lambda bi, hi, qi: (ctx_blk0 + bi, hi))
        k_specs = [pl.BlockSpec((cl, HEAD_DIM), lambda bi, hi, qi: (ctx_blk0 + bi, hi))]
        vt_specs = [pl.BlockSpec((HEAD_DIM, cl), lambda bi, hi, qi: (hi, ctx_blk0 + bi))]
        name = "attention_context"
    n_kv = len(k_specs)
    in_specs = [q_spec] + k_specs + vt_specs
    args = [q] + [k] * n_kv + [v] * n_kv
    aliases = {}
    if out is not None:
        aliases = {len(args): 0}
        in_specs.append(pl.BlockSpec(memory_space=pl.ANY))
        args.append(out)
    return pl.pallas_call(
        functools.partial(_attn_kernel, n_kv=n_kv, kv_chunk=ATTN_KV_CHUNK),
        grid=(b, N_KV_HEADS, nq),
        in_specs=in_specs,
        out_specs=q_spec,
        out_shape=jax.ShapeDtypeStruct((rows.n_all, Q_COLS), BF16),
        input_output_aliases=aliases,
        compiler_params=_params(("parallel", "parallel", "arbitrary")),
        name=name,
    )(*args)


def _rwkv_in_kernel(x_ref, xp_ref, xn_ref, g_ref, sc_ref, sh_ref, w_ref, mup_ref, mun_ref,
                    g2_ref, wmix_ref, w0_ref, a0_ref, kk_scale_ref, ka_ref, rk_ref,
                    r_o, kk_o, v_o, g_o, bonus_o, lw_o, kd_o, bd_o,
                    *, n_lat_blocks, bps_lat, bps_ctx):
    i = pl.program_id(0)
    tm = x_ref.shape[0]
    s = SUBLANES
    is_lat = i < n_lat_blocks
    j = jnp.where(is_lat, i, i - n_lat_blocks)
    bps = jnp.where(is_lat, bps_lat, bps_ctx)
    pos = j % bps
    keep_prev = jnp.where(pos == 0, 0.0, 1.0)
    keep_next = jnp.where(pos == bps - 1, 0.0, 1.0)
    row = lax.broadcasted_iota(jnp.int32, (s, 1), 0)
    first_row = jnp.where(row == 0, keep_prev, 1.0)
    last_row = jnp.where(row == s - 1, keep_next, 1.0)

    norm = lambda ref: _norm_mod(ref[...], g_ref[...], sc_ref[...], sh_ref[...]).astype(BF16)
    h_ext = jnp.concatenate([norm(xp_ref), norm(x_ref), norm(xn_ref)], axis=0)
    n_ext = tm + 2 * s
    w = RWKV_WIDTH
    panels = [slice(0, w), slice(w, 2 * w), slice(2 * w, 3 * w), slice(3 * w, 3 * w + LORA_COLS_PADDED)]
    project = lambda c: _dot(h_ext, w_ref[:, c])

    def shifted(x, c):
        prev = pltpu.roll(x, 1, 0)[s:s + tm]
        nxt = pltpu.roll(x, n_ext - 1, 0)[s:s + tm]
        prev = jnp.concatenate([prev[:s] * first_row, prev[s:]], axis=0)
        nxt = jnp.concatenate([nxt[:tm - s], nxt[tm - s:] * last_row], axis=0)
        mu_prev = mup_ref[:, c]
        mu_next = mun_ref[:, c]
        return (1.0 - mu_prev - mu_next) * x[s:s + tm] + mu_prev * prev + mu_next * nxt

    f_next = project(panels[0])
    outs = []
    for pi, c in enumerate(panels):
        f_cur = f_next
        if pi + 1 < len(panels):
            f_next = project(panels[pi + 1])
        outs.append(shifted(f_cur, c))
    r, k, v, lora = outs

    gate_in = _sigmoid(lora[:, :LORA_GATE])
    g = _dot(gate_in.astype(BF16), g2_ref[...])
    group = lambda q: lora[:, LORA_GATE + q * LORA_PAD:LORA_GATE + (q + 1) * LORA_PAD]
    decay_lr = [_dot(jnp.tanh(group(q)).astype(BF16), wmix_ref[q]) for q in range(2)]
    iclr_lr = [_dot(group(2 + q).astype(BF16), wmix_ref[2 + q]) for q in range(2)]

    ones = _head_ones(2 * LANES)
    kk = k * kk_scale_ref[...]
    kk = kk * lax.rsqrt(_head_sum(kk * kk, ones) + L2_EPS)

    r_o[...] = r
    kk_o[...] = kk
    v_o[...] = v
    g_o[...] = g
    rk = rk_ref[...]
    ka = ka_ref[...]
    bonus = jnp.zeros_like(r)
    for d in range(2):
        z = w0_ref[d:d + 1, :] + decay_lr[d]
        lw_o[d] = -DECAY_SCALE * _sigmoid(z)
        a = _sigmoid(a0_ref[d:d + 1, :] + iclr_lr[d])
        kd = k * (1.0 + (a - 1.0) * ka)
        kd_o[d] = kd
        bd_o[d] = kk * a
        bonus = bonus + _head_sum(r * kd * rk, ones) * v
    bonus_o[...] = bonus


def _rwkv_in(x, g, mod, w_all, layer, p, rows):
    tm = min(RWKV_IN_ROWS, rows.ctx_len)
    d = D_MODEL
    n_all = rows.n_all
    nblk = n_all // tm
    sub_per_blk = tm // SUBLANES
    n_sub = n_all // SUBLANES
    w = RWKV_WIDTH
    cols = RKV_COLS + LORA_COLS_PADDED

    def prev_idx(i):
        return (jnp.maximum(i * sub_per_blk - 1, 0), 0)

    def next_idx(i):
        return (jnp.minimum((i + 1) * sub_per_blk, n_sub - 1), 0)

    once = pl.Buffered(1)
    full = lambda shape: pl.BlockSpec(shape, lambda i: (0,) * len(shape), pipeline_mode=once)
    mod_spec = lambda which: pl.BlockSpec((None, 1, d), lambda i: (rows.mod_row(i, tm), 0, which))
    row_spec = lambda width: pl.BlockSpec((tm, width), lambda i: (i, 0))
    dir_spec = pl.BlockSpec((2, tm, w), lambda i: (0, i, 0))
    kern = functools.partial(_rwkv_in_kernel, n_lat_blocks=rows.n_lat // tm,
                             bps_lat=rows.seq // tm, bps_ctx=rows.ctx_len // tm)
    sds = jax.ShapeDtypeStruct
    return pl.pallas_call(
        kern,
        grid=(nblk,),
        in_specs=[
            row_spec(d), pl.BlockSpec((SUBLANES, d), prev_idx), pl.BlockSpec((SUBLANES, d), next_idx),
            full((1, d)), mod_spec(1), mod_spec(0),
            pl.BlockSpec((None, d, cols), lambda i: (layer, 0, 0), pipeline_mode=once),
            full((1, cols)), full((1, cols)),
            full((LORA_GATE, w)), full((4, LORA_PAD, w)), full((2, w)), full((2, w)),
            full((1, w)), full((1, w)), full((1, w)),
        ],
        out_specs=[row_spec(w)] * 5 + [dir_spec] * 3,
        out_shape=[sds((n_all, w), F32)] * 5 + [sds((2, n_all, w), F32)] * 3,
        compiler_params=_params(("parallel",)),
        name="rwkv_in",
    )(x, x, x, g, mod, mod, w_all, p['mu_prev'], p['mu_next'],
      p['g2'], p['wmix'], p['w0'], p['a0'], p['k_k'], p['k_a'], p['r_k'])


def _scan_kernel(rf_ref, kkf_ref, vf_ref, rb_ref, kkb_ref, vb_ref,
                 lwf_ref, kdf_ref, bdf_ref, lwb_ref, kdb_ref, bdb_ref, yf_ref, yb_ref, h_ref,
                 lincl_ref, msd_ref, mso_ref, mincl_ref, eye_ref, hm_ref, hmb_ref,
                 *, chunk, heads, groups):
    c = chunk
    wd = heads * RWKV_HEAD
    step = pl.program_id(1)

    @pl.when(step == 0)
    def _():
        h_ref[...] = jnp.zeros_like(h_ref)
        iota = lambda shape, ax: lax.broadcasted_iota(jnp.int32, shape, ax)
        ti = iota((c, wd), 0)
        si = iota((c, wd), 1) & (c - 1)
        log2b = INV_BLOCK.bit_length() - 1
        same_blk = (ti >> log2b) == (si >> log2b)
        for d, sgn in enumerate((1, -1)):
            rel_c = (iota((c, c), 0) - iota((c, c), 1)) * sgn
            lincl_ref[d] = jnp.where(rel_c >= 0, 1.0, 0.0).astype(BF16)
            rel = (ti - si) * sgn
            strict = jnp.where(rel > 0, 1.0, 0.0)
            msd_ref[d] = jnp.where(same_blk, strict, 0.0)
            mso_ref[d] = jnp.where(same_blk, 0.0, strict)
            mincl_ref[d] = jnp.where(rel >= 0, 1.0, 0.0)
        eye_ref[...] = jnp.where(ti == si, 1.0, 0.0)
        hm = jnp.where((iota((wd, wd), 0) >> 6) == (iota((wd, wd), 1) >> 6), 1.0, 0.0)
        hm_ref[...] = hm
        hmb_ref[...] = hm.astype(BF16)

    n_sub = rf_ref.shape[0] // c
    lanes = [slice(q * wd, (q + 1) * wd) for q in range(groups)]
    ins = ((rf_ref, kkf_ref, vf_ref, lwf_ref, kdf_ref, bdf_ref),
           (rb_ref, kkb_ref, vb_ref, lwb_ref, kdb_ref, bdb_ref))
    outs = (yf_ref, yb_ref)
    order = ([slice(i * c, (i + 1) * c) for i in range(n_sub)],
             [slice((n_sub - 1 - i) * c, (n_sub - i) * c) for i in range(n_sub)])
    pending = []
    for i in range(n_sub):
        for d in range(2):
            stages = _chunk_operator_stages(
                ins[d], order[d][i], lanes, lincl_ref.at[d], msd_ref.at[d], mso_ref.at[d], mincl_ref.at[d],
                eye_ref, hm_ref, hmb_ref, heads)
            pending.append((d, order[d][i], stages, len(pending) * SCAN_WAVE_SKEW))
    h = [[h_ref[d * groups + q] for q in range(groups)] for d in range(2)]
    tick = 0
    while pending:
        for wave in list(pending):
            d, rw, stages, start = wave
            if tick < start:
                continue
            try:
                next(stages)
            except StopIteration as fin:
                m_c, n_c, rhat, oloc = fin.value
                h[d], y = _advance(m_c, n_c, rhat, oloc, h[d], hmb_ref, heads)
                for q in range(groups):
                    outs[d][rw, lanes[q]] = y[q]
                pending.remove(wave)
        tick += 1
    for d in range(2):
        for q in range(groups):
            h_ref[d * groups + q] = h[d][q]


def _advance(m_c, n_c, rhat, oloc, h, hmb_ref, heads):
    c = RWKV_HEAD
    hmb = hmb_ref[...]

    def stacked(zb):
        return jnp.concatenate([zb] * heads, axis=0) * hmb

    def step(mi, ri, hi):
        lhs = jnp.concatenate([mi, ri], axis=0).astype(BF16)
        return _dot(lhs, stacked(hi.astype(BF16)))

    res = [step(*a) for a in zip(m_c, rhat, h)]
    h_new = [z[:c] + ni for z, ni in zip(res, n_c)]
    y = [z[c:] + oi for z, oi in zip(res, oloc)]
    return h_new, y


def _chunk_operator_stages(stream_refs, rows, lanes, lincl_ref, msd_ref, mso_ref, mincl_ref, eye_ref,
                           hm_ref, hmb_ref, heads):
    c = RWKV_HEAD
    wd = heads * RWKV_HEAD
    hmb = hmb_ref[...]
    each = lambda f, *ls: [f(*a) for a in zip(*ls)]

    def stacked(z):
        zb = z.astype(BF16)
        return jnp.concatenate([zb] * heads, axis=0) * hmb

    def mm(x, y_stacked, dims=NN):
        return _dot(x.astype(BF16), y_stacked, dims)

    def head_blocks(full):
        z = full * hm_ref[...]
        out = z[0:c]
        for hh in range(1, heads):
            out = out + z[hh * c:(hh + 1) * c]
        return out

    cat0 = lambda *xs: jnp.concatenate(xs, axis=0)
    cat1 = lambda *xs: jnp.concatenate(xs, axis=1)
    mul = lambda x, y: x * y
    sub = lambda x, y: x - y
    add = lambda x, y: x + y
    left = lambda z: z[:, :wd]
    right = lambda z: z[:, wd:]
    top = lambda z: z[:c]
    bottom = lambda z: z[c:]
    eye = eye_ref[...]
    lincl = lincl_ref[...]
    r, kk, v, lw, k, b = [[ref[rows, ln] for ln in lanes] for ref in stream_refs]
    g = each(lambda x: _mm_exact_rhs_left(lincl, x), lw)
    yield
    gt = each(lambda x: jnp.sum(x, axis=0, keepdims=True), lw)
    e_x = each(lambda gi, lwi: jnp.exp(gi - lwi), g, lw)
    e_g = each(jnp.exp, g)
    e_n = each(lambda gi: jnp.exp(-gi), g)
    e_c = each(lambda gti, gi: jnp.exp(gti - gi), gt, g)
    kt = each(mul, kk, e_x)
    rt = each(mul, r, e_g)
    bh = each(mul, b, e_n)
    kh = each(mul, k, e_n)
    kb = each(mul, k, e_c)
    bb = each(mul, b, e_c)

    gm = each(lambda kti, rti, bhi, khi: mm(cat0(kti, rti), cat0(stacked(bhi), stacked(khi)).T),
              kt, rt, bh, kh)
    yield
    msd = msd_ref[...]
    mso = mso_ref[...]
    m_incl = mincl_ref[...]
    a_d = each(lambda z: z[:c, :wd] * msd, gm)
    a_o = each(lambda z: z[:c, :wd] * mso, gm)
    b_m = each(lambda z: z[:c, wd:] * (msd + mso), gm)
    e_m = each(lambda z: z[c:, :wd] * m_incl, gm)
    c_m = each(lambda z: z[c:, wd:] * m_incl, gm)

    mm_st = lambda x, y: mm(x, stacked(y))
    unzip = lambda pairs: [list(t) for t in zip(*pairs)]

    def fold(xi, di):
        z = mm(cat0(xi, di), stacked(xi))
        return z[:c], di + z[c:]

    x1 = each(lambda z: -z, a_d)
    dm = each(lambda z: eye - z, a_d)
    x2 = each(mm_st, x1, x1)
    bcv = each(lambda bi, ci, vi: mm(cat0(bi, ci), stacked(vi)), b_m, c_m, v)
    bv = each(top, bcv)
    cv = each(bottom, bcv)
    yield
    x4, dm = unzip(each(fold, x2, dm))
    yield
    x8, dm = unzip(each(fold, x4, dm))
    yield
    dm = each(lambda di, xi: di + mm_st(di, xi), dm, x8)
    yield
    n1 = each(mm_st, dm, a_o)
    yield
    n2 = each(mm_st, n1, n1)
    yield
    t1 = each(lambda z: eye - z, n1)
    t1 = each(lambda ti, ni: ti + mm_st(ti, ni), t1, n2)
    yield
    t_m = each(mm_st, t1, dm)
    yield
    et = each(mm_st, e_m, t_m)
    yield
    both = each(lambda ti, ei, kti, bvi: mm(cat0(ti, ei), cat1(stacked(kti), stacked(bvi))),
                t_m, et, kt, bv)
    yield
    wt = each(lambda z: z[:c, :wd], both)
    ut = each(lambda z: z[:c, wd:], both)
    rhat = each(lambda ri, z: ri - z[c:, :wd], rt, both)
    oloc = each(lambda ci, z: ci - z[c:, wd:], cv, both)
    full_m = each(lambda bi, wi: _dot(bi.astype(BF16), wi.astype(BF16), TN), bb, wt)
    full_n = each(lambda ki, bi, vi, ui: _dot(cat0(ki, bi).astype(BF16), cat0(vi, -ui).astype(BF16), TN),
                  kb, bb, v, ut)
    m_c = each(lambda gti, fi: eye * jnp.exp(gti) - head_blocks(fi), gt, full_m)
    n_c = each(head_blocks, full_n)
    return m_c, n_c, rhat, oloc


def _mm_exact_rhs_left(a_exact, b):
    bh, bl = _split(b)
    return _dot(a_exact, bh) + _dot(a_exact, bl)


def _rwkv_scan(r, kk, v, lw, kd, bd, rows):
    c, heads = SCAN_CHUNK, SCAN_HEADS
    wd = heads * RWKV_HEAD
    nq = RWKV_WIDTH // wd
    rb = c * SCAN_SUBCHUNKS
    assert rows.ctx_len % rb == 0 and rows.seq % rb == 0
    nc_ctx = rows.ctx_len // rb
    nc_lat = rows.seq // rb
    lat_blocks = rows.n_lat // rb

    def chunk_idx(d, bi, st):
        in_ctx = st < nc_ctx
        t_ctx = jnp.where(d == 0, st, nc_ctx - 1 - st)
        sl = st - nc_ctx
        t_lat = jnp.where(d == 0, sl, nc_lat - 1 - sl)
        return jnp.where(in_ctx, lat_blocks + bi * nc_ctx + t_ctx, bi * nc_lat + t_lat)

    assert c == RWKV_HEAD
    groups = SCAN_GROUPS
    bw = wd * groups
    assert groups * wd == RWKV_WIDTH
    shared = lambda d: pl.BlockSpec((rb, bw), lambda bi, st: (chunk_idx(d, bi, st), 0))
    per_dir = lambda d: pl.BlockSpec((None, rb, bw), lambda bi, st: (d, chunk_idx(d, bi, st), 0))
    y_shape = jax.ShapeDtypeStruct((rows.n_all, RWKV_WIDTH), F32)
    return pl.pallas_call(
        functools.partial(_scan_kernel, chunk=c, heads=heads, groups=groups),
        grid=(rows.batch, nc_ctx + nc_lat),
        in_specs=[shared(0)] * 3 + [shared(1)] * 3 + [per_dir(0)] * 3 + [per_dir(1)] * 3,
        out_specs=[shared(0), shared(1)],
        out_shape=[y_shape, y_shape],
        scratch_shapes=[
            pltpu.VMEM((2 * groups, c, wd), F32),
            pltpu.VMEM((2, c, c), BF16),
            pltpu.VMEM((2, c, wd), F32),
            pltpu.VMEM((2, c, wd), F32),
            pltpu.VMEM((2, c, wd), F32),
            pltpu.VMEM((c, wd), F32),
            pltpu.VMEM((wd, wd), F32),
            pltpu.VMEM((wd, wd), BF16),
        ],
        compiler_params=_params(("parallel", "arbitrary")),
        name="rwkv_scan",
    )(r, kk, v, r, kk, v, lw, kd, bd, lw, kd, bd)


def _out_kernel(attn_ref, yf_ref, yb_ref, bonus_ref, g_ref, lnw_ref, lnb_ref, x_ref, gate_ref,
                wa_ref, wr_ref, o_ref):
    out_a = _dot(attn_ref[...], wa_ref[...])
    ones = _head_ones(2 * LANES)
    y = yf_ref[...] + yb_ref[...]
    mean = _head_sum(y, ones) * (1.0 / RWKV_HEAD)
    yc = y - mean
    var = _head_sum(yc * yc, ones) * (1.0 / RWKV_HEAD)
    yn = yc * lax.rsqrt(var + GN_EPS) * lnw_ref[...] + lnb_ref[...]
    rw = ((yn + bonus_ref[...]) * g_ref[...]).astype(BF16)
    out = out_a + _dot(rw, wr_ref[...])
    o_ref[...] = x_ref[...] + gate_ref[...] * out


def _mix_output(attn, y, bonus, g, ln_w, ln_b, x, mod, w_out, layer, rows, n_rows):
    d = D_MODEL
    tm = min(256, rows.tm)
    tn = d
    w = RWKV_WIDTH
    return pl.pallas_call(
        _out_kernel,
        grid=(n_rows // tm, d // tn),
        in_specs=[
            pl.BlockSpec((tm, Q_COLS), lambda i, j: (i, 0)),
            pl.BlockSpec((tm, w), lambda i, j: (i, 0)),
            pl.BlockSpec((tm, w), lambda i, j: (i, 0)),
            pl.BlockSpec((tm, w), lambda i, j: (i, 0)),
            pl.BlockSpec((tm, w), lambda i, j: (i, 0)),
            pl.BlockSpec((1, w), lambda i, j: (0, 0)),
            pl.BlockSpec((1, w), lambda i, j: (0, 0)),
            pl.BlockSpec((tm, tn), lambda i, j: (i, j)),
            pl.BlockSpec((None, 1, tn), lambda i, j: (rows.mod_row(i, tm), 0, 2 * (d // tn) + j)),
            pl.BlockSpec((None, Q_COLS, tn), lambda i, j: (layer, 0, j)),
            pl.BlockSpec((None, w, tn), lambda i, j: (layer, 1, j)),
        ],
        out_specs=pl.BlockSpec((tm, tn), lambda i, j: (i, j)),
        out_shape=jax.ShapeDtypeStruct((n_rows, d), F32),
        compiler_params=_params(("parallel", "arbitrary")),
        name="mix_output",
    )(attn, y[0], y[1], bonus, g, ln_w, ln_b, x, mod, w_out, w_out)


def _ffn_kernel(x_ref, g_ref, sc_ref, sh_ref, gate_ref, w1_ref, w3_ref, w2_ref, fg_ref,
                o_ref, h_ref, acc_ref, *, final):
    j = pl.program_id(1)

    @pl.when(j == 0)
    def _():
        h_ref[...] = _norm_mod(x_ref[...], g_ref[...], sc_ref[...], sh_ref[...]).astype(BF16)
        acc_ref[...] = jnp.zeros_like(acc_ref)

    h = h_ref[...]
    u = _dot(h, w1_ref[...])
    a = (u * _sigmoid(u)) * _dot(h, w3_ref[...])
    acc_ref[...] += _dot(a.astype(BF16), w2_ref[...])

    @pl.when(j == pl.num_programs(1) - 1)
    def _():
        y = x_ref[...] + gate_ref[...] * acc_ref[...]
        if final:
            ms = jnp.mean(y * y, axis=-1, keepdims=True)
            y = y * lax.rsqrt(ms + NORM_EPS) * fg_ref[...]
        o_ref[...] = y


def _ffn(x, g, mod, w1, w3, w2, layer, final_g, rows, n_rows, final):
    tm, d = rows.tm, D_MODEL
    tf = 512
    return pl.pallas_call(
        functools.partial(_ffn_kernel, final=final),
        grid=(n_rows // tm, D_FF // tf),
        in_specs=[
            pl.BlockSpec((tm, d), lambda i, j: (i, 0)),
            pl.BlockSpec((1, d), lambda i, j: (0, 0)),
            _mod_spec(rows, 4, 2),
            _mod_spec(rows, 3, 2),
            _mod_spec(rows, 5, 2),
            pl.BlockSpec((None, d, tf), lambda i, j: (layer, 0, j)),
            pl.BlockSpec((None, d, tf), lambda i, j: (layer, 0, j)),
            pl.BlockSpec((None, tf, d), lambda i, j: (layer, j, 0)),
            pl.BlockSpec((1, d), lambda i, j: (0, 0)),
        ],
        out_specs=pl.BlockSpec((tm, d), lambda i, j: (i, 0)),
        out_shape=jax.ShapeDtypeStruct((n_rows, d), F32),
        scratch_shapes=[pltpu.VMEM((tm, d), BF16), pltpu.VMEM((tm, d), F32)],
        compiler_params=_params(("parallel", "arbitrary")),
        name="ffn_final" if final else "ffn",
    )(x, g, mod, mod, mod, w1, w3, w2, final_g)


def _rope_tables(seq, tm):
    n_rows = seq // GRID_W
    row = jnp.broadcast_to(jnp.arange(n_rows)[:, None], (n_rows, GRID_W)).reshape(-1)
    col = jnp.broadcast_to(jnp.arange(GRID_W)[None, :], (n_rows, GRID_W)).reshape(-1)
    inv = ROPE_THETA ** (-jnp.arange(ROPE_FREQS, dtype=F32) / ROPE_FREQS)
    ang = jnp.concatenate([row[:, None].astype(F32) * inv, col[:, None].astype(F32) * inv], axis=-1)
    cos, sin = jnp.cos(ang), jnp.sin(ang)
    cos2 = jnp.concatenate([cos, cos], axis=-1)
    sin2 = jnp.concatenate([-sin, sin], axis=-1)
    cos2 = jnp.concatenate([cos2, jnp.ones((tm, HEAD_DIM), F32)], axis=0)
    sin2 = jnp.concatenate([sin2, jnp.zeros((tm, HEAD_DIM), F32)], axis=0)
    return cos2, sin2


def _pad_lora_cols(a):
    parts = [a[..., :LORA_GATE]]
    pad = [(0, 0)] * (a.ndim - 1) + [(0, LORA_PAD - LORA_DECAY)]
    for q in range(4):
        lo = LORA_GATE + q * LORA_DECAY
        parts.append(jnp.pad(a[..., lo:lo + LORA_DECAY], pad))
    return jnp.concatenate(parts, axis=-1)


def _mix_weights(w2, a2):
    mats = jnp.concatenate([w2, a2], axis=0)
    return jnp.pad(mats, ((0, 0), (0, LORA_PAD - LORA_DECAY), (0, 0)))


def kernel(x, c, ctx, c_ctx, norm1_g, norm2_g, ada_w, ada_b, w_in, q_gain, k_gain, mu_prev, mu_next,
           w0, w2, a0, a2, g2, k_k, k_a, r_k, ln_x_w, ln_x_b, w_out, ffn_w1, ffn_w3, ffn_w2, final_g):
    batch, seq, d = x.shape
    ctx_len = ctx.shape[1]
    depth = w_in.shape[0]
    rows = _Rows(batch, seq, ctx_len, min(512, batch * ctx_len))
    w = RWKV_WIDTH

    cond = jnp.concatenate([c, c_ctx[None, :], jnp.zeros((SUBLANES - batch - 1, d), F32)], axis=0)
    mod_all = _modulation(cond, ada_w, ada_b)
    mod_all = mod_all.reshape(depth, SUBLANES, 1, 6 * d)
    cos2, sin2 = _rope_tables(seq, rows.tm)

    w_attn_b = w_in[:, :, :ATTN_COLS].astype(BF16)
    split = ATTN_COLS + RKV_COLS
    w_rwkv_b = jnp.concatenate([w_in[:, :, ATTN_COLS:split], _pad_lora_cols(w_in[:, :, split:])],
                               axis=-1).astype(BF16)
    pad_mu = lambda mu: jnp.concatenate([mu[:, :RKV_COLS], _pad_lora_cols(mu[:, RKV_COLS:])], axis=-1)
    mu_prev_p = pad_mu(mu_prev)
    mu_next_p = pad_mu(mu_next)
    w_out_b = w_out.astype(BF16)
    w1_b = ffn_w1.astype(BF16)
    w3_b = ffn_w3.astype(BF16)
    w2_b = ffn_w2.astype(BF16)

    tok = jnp.concatenate([x.reshape(batch * seq, d), ctx.reshape(batch * ctx_len, d)], axis=0)
    row1 = lambda a: a.reshape(1, -1)
    for i in range(depth):
        last = i == depth - 1
        mod = mod_all[i]
        g1 = row1(norm1_g[i])
        q, k, vt = _attn_project(tok, g1, mod, w_attn_b, i, row1(q_gain[i]), row1(k_gain[i]),
                                 cos2, sin2, rows)
        attn = _attention(q, k, vt, rows, latent=True)
        if not last:
            attn = _attention(q, k, vt, rows, latent=False, out=attn)
        prep = {
            'mu_prev': row1(mu_prev_p[i]), 'mu_next': row1(mu_next_p[i]),
            'g2': g2[i].astype(BF16), 'wmix': _mix_weights(w2[i], a2[i]).astype(BF16),
            'w0': w0[i], 'a0': a0[i],
            'k_k': row1(k_k[i]), 'k_a': row1(k_a[i]), 'r_k': row1(r_k[i]),
        }
        r_s, kk_s, v_s, g_s, bonus, lw, kd, bd = _rwkv_in(tok, g1, mod, w_rwkv_b, i, prep, rows)
        y = _rwkv_scan(r_s, kk_s, v_s, lw, kd, bd, rows)
        n_rows = rows.n_lat if last else rows.n_all
        tok = _mix_output(attn, y, bonus, g_s, row1(ln_x_w[i]), row1(ln_x_b[i]), tok, mod,
                          w_out_b, i, rows, n_rows)
        tok = _ffn(tok, row1(norm2_g[i]), mod, w1_b, w3_b, w2_b, i, row1(final_g),
                   rows, n_rows, final=last)
    return tok.reshape(batch, seq, d)
```

```python
import functools

import jax
import jax.numpy as jnp
from jax import lax
from jax.experimental import pallas as pl
from jax.experimental.pallas import tpu as pltpu

F32 = jnp.float32
BF16 = jnp.bfloat16

D_MODEL = 2048
HEAD_DIM = 128
N_Q_HEADS = 8
N_KV_HEADS = 2
GQA_GROUP = N_Q_HEADS // N_KV_HEADS
Q_COLS = N_Q_HEADS * HEAD_DIM
KV_COLS = N_KV_HEADS * HEAD_DIM
ATTN_COLS = Q_COLS + 2 * KV_COLS
RWKV_WIDTH = 1024
RWKV_HEAD = 64
LORA_GATE = 256
LORA_DECAY = 96
LORA_ICLR = 96
LORA_COLS = LORA_GATE + 2 * LORA_DECAY + 2 * LORA_ICLR
LORA_PAD = 128
LORA_COLS_PADDED = LORA_GATE + 4 * LORA_PAD
RKV_COLS = 3 * RWKV_WIDTH
D_FF = 5632
GRID_W = 64
ROPE_THETA = 10000.0
ROPE_FREQS = HEAD_DIM // 4
ATTN_SCALE = HEAD_DIM ** -0.5
LOG2_E = 1.4426950408889634
NORM_EPS = 1e-6
GN_EPS = 64e-5
L2_EPS = 1e-12
DECAY_SCALE = 0.6065306597126334

LANES = 128
SUBLANES = 8
VMEM_LIMIT = 56 * 1024 * 1024

ATTN_Q_ROWS = 256
ATTN_KV_CHUNK = 1024
ATTN_LOOKAHEAD = 8
RWKV_IN_ROWS = 256
SCAN_CHUNK = 64
SCAN_HEADS = 4
INV_BLOCK = 16
SCAN_GROUPS = 4
SCAN_SUBCHUNKS = 4
SCAN_WAVE_SKEW = 2

NN = ((1,), (0,))
NT = ((1,), (1,))
TN = ((0,), (0,))


def _dot(a, b, dims=NN):
    return lax.dot_general(a, b, (dims, ((), ())), preferred_element_type=F32)


def _split(x):
    hi = x.astype(BF16)
    lo = (x - hi.astype(F32)).astype(BF16)
    return hi, lo


def _mm3(a, b, dims=NN):
    ah, al = a
    bh, bl = b
    return _dot(ah, bh, dims) + (_dot(ah, bl, dims) + _dot(al, bh, dims))


def _mm3f(a, b, dims=NN):
    return _mm3(_split(a), _split(b), dims)


def _sigmoid(z):
    return 1.0 / (1.0 + jnp.exp(-z))


def _norm_mod(x, g, scale, shift):
    ms = jnp.mean(x * x, axis=-1, keepdims=True)
    y = x * lax.rsqrt(ms + NORM_EPS) * g
    return y * (1.0 + scale) + shift


def _head_ones(width):
    r = lax.broadcasted_iota(jnp.int32, (width, width), 0) >> 6
    c = lax.broadcasted_iota(jnp.int32, (width, width), 1) >> 6
    return jnp.where(r == c, 1.0, 0.0).astype(BF16)


def _head_sum(x, ones):
    w = ones.shape[0]
    xb = x.astype(BF16)
    parts = [_dot(xb[:, c:c + w], ones) for c in range(0, x.shape[1], w)]
    return jnp.concatenate(parts, axis=1)


def _params(sem):
    return pltpu.CompilerParams(dimension_semantics=sem, vmem_limit_bytes=VMEM_LIMIT)


def _mod_kernel(c_ref, w_ref, b_ref, o_ref):
    c = c_ref[...]
    s = c * _sigmoid(c)
    o_ref[...] = _mm3f(s, w_ref[...]) + b_ref[...]


def _modulation(cond, ada_w, ada_b):
    depth, d, n = ada_w.shape
    tn = 1024
    rows = cond.shape[0]
    return pl.pallas_call(
        _mod_kernel,
        grid=(depth, n // tn),
        in_specs=[
            pl.BlockSpec((rows, d), lambda l, j: (0, 0)),
            pl.BlockSpec((None, d, tn), lambda l, j: (l, 0, j)),
            pl.BlockSpec((None, 1, tn), lambda l, j: (l, 0, j)),
        ],
        out_specs=pl.BlockSpec((None, rows, tn), lambda l, j: (l, 0, j)),
        out_shape=jax.ShapeDtypeStruct((depth, rows, n), F32),
        compiler_params=_params(("parallel", "parallel")),
        name="modulation",
    )(cond, ada_w, ada_b.reshape(depth, 1, n))


class _Rows:
    def __init__(self, batch, seq, ctx_len, tm):
        self.batch, self.seq, self.ctx_len, self.tm = batch, seq, ctx_len, tm
        self.n_lat = batch * seq
        self.n_ctx = batch * ctx_len
        self.n_all = self.n_lat + self.n_ctx
        assert seq % tm == 0 and self.n_ctx % tm == 0
        self.blocks_per_batch = seq // tm

    def mod_row(self, i, tm=None):
        per_batch = self.blocks_per_batch if tm is None else self.seq // tm
        return jnp.minimum(i // per_batch, self.batch)


def _mod_spec(rows, which, ngrid):
    d = D_MODEL
    if ngrid == 1:
        return pl.BlockSpec((None, 1, d), lambda i: (rows.mod_row(i), 0, which))
    return pl.BlockSpec((None, 1, d), lambda i, j: (rows.mod_row(i), 0, which))


def _attn_proj_kernel(x_ref, g_ref, sc_ref, sh_ref, w_ref, qg_ref, kg_ref, cos_ref, sin_ref,
                      q_ref, k_ref, vt_ref):
    qg = qg_ref[...]
    kg = kg_ref[...]
    half = x_ref.shape[0] // 2
    halves = [slice(0, half), slice(half, 2 * half)]
    hs = [_norm_mod(x_ref[rs, :], g_ref[...], sc_ref[...], sh_ref[...]).astype(BF16) for rs in halves]
    fs = [_dot(h, w_ref[...]) for h in hs]
    for rs, f in zip(halves, fs):
        cos = cos_ref[rs, :]
        sin = sin_ref[rs, :]

        def head(xh, gain, scale):
            ms = jnp.mean(xh * xh, axis=-1, keepdims=True)
            y = xh * lax.rsqrt(ms + NORM_EPS) * gain
            y = y * cos + pltpu.roll(y, HEAD_DIM // 2, 1) * sin
            return (y * scale).astype(BF16)

        for hq in range(N_Q_HEADS):
            c = hq * HEAD_DIM
            q_ref[rs, c:c + HEAD_DIM] = head(f[:, c:c + HEAD_DIM], qg, ATTN_SCALE * LOG2_E)
        for hk in range(N_KV_HEADS):
            c = hk * HEAD_DIM
            k_ref[rs, c:c + HEAD_DIM] = head(f[:, Q_COLS + c:Q_COLS + c + HEAD_DIM], kg, 1.0)
        vt_ref[:, rs] = f[:, Q_COLS + KV_COLS:ATTN_COLS].T.astype(BF16)


def _attn_project(x, g, mod, w, layer, q_gain, k_gain, cos2, sin2, rows):
    tm, d = rows.tm, D_MODEL
    bpb = rows.blocks_per_batch

    def rope_idx(i):
        return (jnp.where(i < rows.n_lat // tm, i % bpb, bpb), 0)

    return pl.pallas_call(
        _attn_proj_kernel,
        grid=(rows.n_all // tm,),
        in_specs=[
            pl.BlockSpec((tm, d), lambda i: (i, 0)),
            pl.BlockSpec((1, d), lambda i: (0, 0)),
            _mod_spec(rows, 1, 1),
            _mod_spec(rows, 0, 1),
            pl.BlockSpec((None, d, ATTN_COLS), lambda i: (layer, 0, 0)),
            pl.BlockSpec((1, HEAD_DIM), lambda i: (0, 0)),
            pl.BlockSpec((1, HEAD_DIM), lambda i: (0, 0)),
            pl.BlockSpec((tm, HEAD_DIM), rope_idx),
            pl.BlockSpec((tm, HEAD_DIM), rope_idx),
        ],
        out_specs=[
            pl.BlockSpec((tm, Q_COLS), lambda i: (i, 0)),
            pl.BlockSpec((tm, KV_COLS), lambda i: (i, 0)),
            pl.BlockSpec((KV_COLS, tm), lambda i: (0, i)),
        ],
        out_shape=[
            jax.ShapeDtypeStruct((rows.n_all, Q_COLS), BF16),
            jax.ShapeDtypeStruct((rows.n_all, KV_COLS), BF16),
            jax.ShapeDtypeStruct((KV_COLS, rows.n_all), BF16),
        ],
        compiler_params=_params(("parallel",)),
        name="attn_proj",
    )(x, g, mod, mod, w, q_gain, k_gain, cos2, sin2)


def _attn_kernel(q_ref, *refs, n_kv, kv_chunk):
    k_refs = refs[:n_kv]
    vt_refs = refs[n_kv:2 * n_kv]
    o_ref = refs[-1]
    tq = q_ref.shape[0]
    heads = range(GQA_GROUP)
    q = [q_ref[:, h * HEAD_DIM:(h + 1) * HEAD_DIM] for h in heads]
    m = [jnp.full((1, tq), -jnp.inf, F32) for _ in heads]
    l = [jnp.zeros((1, tq), F32) for _ in heads]
    acc = [jnp.zeros((HEAD_DIM, tq), F32) for _ in heads]
    items = []
    for k_ref, vt_ref in zip(k_refs, vt_refs):
        n = k_ref.shape[0]
        ck = min(kv_chunk, n)
        items += [(k_ref, vt_ref, c0, ck, h) for c0 in range(0, n, ck) for h in heads]

    def scores(item):
        k_ref, _, c0, ck, h = item
        return _dot(k_ref[c0:c0 + ck, :], q[h], NT)

    ahead = min(ATTN_LOOKAHEAD, len(items))
    pending = [scores(it) for it in items[:ahead]]
    for i, (_, vt_ref, c0, ck, h) in enumerate(items):
        if i + ahead < len(items):
            pending.append(scores(items[i + ahead]))
        st = pending.pop(0)
        m_new = jnp.maximum(m[h], jnp.max(st, axis=0, keepdims=True))
        alpha = jnp.exp2(m[h] - m_new)
        p = jnp.exp2(st - m_new)
        l[h] = alpha * l[h] + jnp.sum(p, axis=0, keepdims=True)
        acc[h] = alpha * acc[h] + _dot(vt_ref[:, c0:c0 + ck], p.astype(BF16))
        m[h] = m_new
    for h in heads:
        o_ref[:, h * HEAD_DIM:(h + 1) * HEAD_DIM] = (acc[h] / l[h]).T.astype(o_ref.dtype)


def _attention(q, k, v, rows, latent, out=None):
    b, s, cl = rows.batch, rows.seq, rows.ctx_len
    gw = GQA_GROUP * HEAD_DIM
    ctx_blk0 = rows.n_lat // cl
    if latent:
        tq = min(ATTN_Q_ROWS, s)
        nq = s // tq
        q_spec = pl.BlockSpec((tq, gw), lambda bi, hi, qi: (bi * nq + qi, hi))
        k_specs = [pl.BlockSpec((s, HEAD_DIM), lambda bi, hi, qi: (bi, hi)),
                   pl.BlockSpec((cl, HEAD_DIM), lambda bi, hi, qi: (ctx_blk0 + bi, hi))]
        vt_specs = [pl.BlockSpec((HEAD_DIM, s), lambda bi, hi, qi: (hi, bi)),
                    pl.BlockSpec((HEAD_DIM, cl), lambda bi, hi, qi: (hi, ctx_blk0 + bi))]
        name = "attention_latent"
    else:
        tq = cl
        nq = 1
        q_spec = pl.BlockSpec((tq, gw), lambda bi, hi, qi: (ctx_blk0 + bi, hi))
        k_specs = [pl.BlockSpec((cl, HEAD_DIM), lambda bi, hi, qi: (ctx_blk0 + bi, hi))]
        vt_specs = [pl.BlockSpec((HEAD_DIM, cl), lambda bi, hi, qi: (hi, ctx_blk0 + bi))]
        name = "attention_context"
    n_kv = len(k_specs)
    in_specs = [q_spec] + k_specs + vt_specs
    args = [q] + [k] * n_kv + [v] * n_kv
    aliases = {}
    if out is not None:
        aliases = {len(args): 0}
        in_specs.append(pl.BlockSpec(memory_space=pl.ANY))
        args.append(out)
    return pl.pallas_call(
        functools.partial(_attn_kernel, n_kv=n_kv, kv_chunk=ATTN_KV_CHUNK),
        grid=(b, N_KV_HEADS, nq),
        in_specs=in_specs,
        out_specs=q_spec,
        out_shape=jax.ShapeDtypeStruct((rows.n_all, Q_COLS), BF16),
        input_output_aliases=aliases,
        compiler_params=_params(("parallel", "parallel", "arbitrary")),
        name=name,
    )(*args)


def _rwkv_in_kernel(x_ref, xp_ref, xn_ref, g_ref, sc_ref, sh_ref, w_ref, mup_ref, mun_ref,
                    g2_ref, wmix_ref, w0_ref, a0_ref, kk_scale_ref, ka_ref, rk_ref,
                    r_o, kk_o, v_o, g_o, bonus_o, lw_o, kd_o, bd_o,
                    *, n_lat_blocks, bps_lat, bps_ctx):
    i = pl.program_id(0)
    tm = x_ref.shape[0]
    s = SUBLANES
    is_lat = i < n_lat_blocks
    j = jnp.where(is_lat, i, i - n_lat_blocks)
    bps = jnp.where(is_lat, bps_lat, bps_ctx)
    pos = j % bps
    keep_prev = jnp.where(pos == 0, 0.0, 1.0)
    keep_next = jnp.where(pos == bps - 1, 0.0, 1.0)
    row = lax.broadcasted_iota(jnp.int32, (s, 1), 0)
    first_row = jnp.where(row == 0, keep_prev, 1.0)
    last_row = jnp.where(row == s - 1, keep_next, 1.0)

    norm = lambda ref: _norm_mod(ref[...], g_ref[...], sc_ref[...], sh_ref[...]).astype(BF16)
    h_ext = jnp.concatenate([norm(xp_ref), norm(x_ref), norm(xn_ref)], axis=0)
    n_ext = tm + 2 * s
    w = RWKV_WIDTH
    panels = [slice(0, w), slice(w, 2 * w), slice(2 * w, 3 * w), slice(3 * w, 3 * w + LORA_COLS_PADDED)]
    project = lambda c: _dot(h_ext, w_ref[:, c])

    def shifted(x, c):
        prev = pltpu.roll(x, 1, 0)[s:s + tm]
        nxt = pltpu.roll(x, n_ext - 1, 0)[s:s + tm]
        prev = jnp.concatenate([prev[:s] * first_row, prev[s:]], axis=0)
        nxt = jnp.concatenate([nxt[:tm - s], nxt[tm - s:] * last_row], axis=0)
        mu_prev = mup_ref[:, c]
        mu_next = mun_ref[:, c]
        return (1.0 - mu_prev - mu_next) * x[s:s + tm] + mu_prev * prev + mu_next * nxt

    f_next = project(panels[0])
    outs = []
    for pi, c in enumerate(panels):
        f_cur = f_next
        if pi + 1 < len(panels):
            f_next = project(panels[pi + 1])
        outs.append(shifted(f_cur, c))
    r, k, v, lora = outs

    gate_in = _sigmoid(lora[:, :LORA_GATE])
    g = _dot(gate_in.astype(BF16), g2_ref[...])
    group = lambda q: lora[:, LORA_GATE + q * LORA_PAD:LORA_GATE + (q + 1) * LORA_PAD]
    decay_lr = [_dot(jnp.tanh(group(q)).astype(BF16), wmix_ref[q]) for q in range(2)]
    iclr_lr = [_dot(group(2 + q).astype(BF16), wmix_ref[2 + q]) for q in range(2)]

    ones = _head_ones(2 * LANES)
    kk = k * kk_scale_ref[...]
    kk = kk * lax.rsqrt(_head_sum(kk * kk, ones) + L2_EPS)

    r_o[...] = r
    kk_o[...] = kk
    v_o[...] = v
    g_o[...] = g
    rk = rk_ref[...]
    ka = ka_ref[...]
    bonus = jnp.zeros_like(r)
    for d in range(2):
        z = w0_ref[d:d + 1, :] + decay_lr[d]
        lw_o[d] = -DECAY_SCALE * _sigmoid(z)
        a = _sigmoid(a0_ref[d:d + 1, :] + iclr_lr[d])
        kd = k * (1.0 + (a - 1.0) * ka)
        kd_o[d] = kd
        bd_o[d] = kk * a
        bonus = bonus + _head_sum(r * kd * rk, ones) * v
    bonus_o[...] = bonus


def _rwkv_in(x, g, mod, w_all, layer, p, rows):
    tm = min(RWKV_IN_ROWS, rows.ctx_len)
    d = D_MODEL
    n_all = rows.n_all
    nblk = n_all // tm
    sub_per_blk = tm // SUBLANES
    n_sub = n_all // SUBLANES
    w = RWKV_WIDTH
    cols = RKV_COLS + LORA_COLS_PADDED

    def prev_idx(i):
        return (jnp.maximum(i * sub_per_blk - 1, 0), 0)

    def next_idx(i):
        return (jnp.minimum((i + 1) * sub_per_blk, n_sub - 1), 0)

    once = pl.Buffered(1)
    full = lambda shape: pl.BlockSpec(shape, lambda i: (0,) * len(shape), pipeline_mode=once)
    mod_spec = lambda which: pl.BlockSpec((None, 1, d), lambda i: (rows.mod_row(i, tm), 0, which))
    row_spec = lambda width: pl.BlockSpec((tm, width), lambda i: (i, 0))
    dir_spec = pl.BlockSpec((2, tm, w), lambda i: (0, i, 0))
    kern = functools.partial(_rwkv_in_kernel, n_lat_blocks=rows.n_lat // tm,
                             bps_lat=rows.seq // tm, bps_ctx=rows.ctx_len // tm)
    sds = jax.ShapeDtypeStruct
    return pl.pallas_call(
        kern,
        grid=(nblk,),
        in_specs=[
            row_spec(d), pl.BlockSpec((SUBLANES, d), prev_idx), pl.BlockSpec((SUBLANES, d), next_idx),
            full((1, d)), mod_spec(1), mod_spec(0),
            pl.BlockSpec((None, d, cols), lambda i: (layer, 0, 0), pipeline_mode=once),
            full((1, cols)), full((1, cols)),
            full((LORA_GATE, w)), full((4, LORA_PAD, w)), full((2, w)), full((2, w)),
            full((1, w)), full((1, w)), full((1, w)),
        ],
        out_specs=[row_spec(w)] * 5 + [dir_spec] * 3,
        out_shape=[sds((n_all, w), F32)] * 5 + [sds((2, n_all, w), F32)] * 3,
        compiler_params=_params(("parallel",)),
        name="rwkv_in",
    )(x, x, x, g, mod, mod, w_all, p['mu_prev'], p['mu_next'],
      p['g2'], p['wmix'], p['w0'], p['a0'], p['k_k'], p['k_a'], p['r_k'])


def _scan_kernel(r_ref, kk_ref, v_ref, lw_ref, kd_ref, bd_ref, y_ref, h_ref,
                 lincl_ref, msd_ref, mso_ref, mincl_ref, eye_ref, hm_ref, hmb_ref,
                 *, chunk, heads, groups):
    c = chunk
    wd = heads * RWKV_HEAD
    step = pl.program_id(3)

    @pl.when(step == 0)
    def _():
        h_ref[...] = jnp.zeros_like(h_ref)
        sgn = 1 - 2 * pl.program_id(0)
        iota = lambda shape, ax: lax.broadcasted_iota(jnp.int32, shape, ax)
        rel_c = (iota((c, c), 0) - iota((c, c), 1)) * sgn
        lincl_ref[...] = jnp.where(rel_c >= 0, 1.0, 0.0).astype(BF16)
        ti = iota((c, wd), 0)
        si = iota((c, wd), 1) & (c - 1)
        rel = (ti - si) * sgn
        log2b = INV_BLOCK.bit_length() - 1
        same_blk = (ti >> log2b) == (si >> log2b)
        strict = jnp.where(rel > 0, 1.0, 0.0)
        msd_ref[...] = jnp.where(same_blk, strict, 0.0)
        mso_ref[...] = jnp.where(same_blk, 0.0, strict)
        mincl_ref[...] = jnp.where(rel >= 0, 1.0, 0.0)
        eye_ref[...] = jnp.where(ti == si, 1.0, 0.0)
        hm = jnp.where((iota((wd, wd), 0) >> 6) == (iota((wd, wd), 1) >> 6), 1.0, 0.0)
        hm_ref[...] = hm
        hmb_ref[...] = hm.astype(BF16)

    n_sub = r_ref.shape[0] // c
    d = pl.program_id(0)
    rows = [pl.ds(pl.multiple_of(jnp.where(d == 0, i, n_sub - 1 - i) * c, c), c) for i in range(n_sub)]
    lanes = [slice(q * wd, (q + 1) * wd) for q in range(groups)]
    refs = (r_ref, kk_ref, v_ref, lw_ref, kd_ref, bd_ref)
    waves = [_chunk_operator_stages(*[[ref[rw, ln] for ln in lanes] for ref in refs],
                                    lincl_ref, msd_ref, mso_ref, mincl_ref, eye_ref, hm_ref, hmb_ref, heads)
             for rw in rows]
    h = [h_ref[q] for q in range(groups)]
    done = 0
    tick = 0
    while done < n_sub:
        for i in range(done, n_sub):
            if tick < i * SCAN_WAVE_SKEW:
                break
            try:
                next(waves[i])
            except StopIteration as fin:
                m_c, n_c, rhat, oloc = fin.value
                h, y = _advance(m_c, n_c, rhat, oloc, h, hmb_ref, heads)
                for q in range(groups):
                    y_ref[rows[i], lanes[q]] = y[q]
                done += 1
        tick += 1
    for q in range(groups):
        h_ref[q] = h[q]


def _advance(m_c, n_c, rhat, oloc, h, hmb_ref, heads):
    c = RWKV_HEAD
    hmb = hmb_ref[...]

    def stacked(zb):
        return jnp.concatenate([zb] * heads, axis=0) * hmb

    def step(mi, ri, hi):
        lhs = jnp.concatenate([mi, ri], axis=0).astype(BF16)
        return _dot(lhs, stacked(hi.astype(BF16)))

    res = [step(*a) for a in zip(m_c, rhat, h)]
    h_new = [z[:c] + ni for z, ni in zip(res, n_c)]
    y = [z[c:] + oi for z, oi in zip(res, oloc)]
    return h_new, y


def _chunk_operator_stages(r, kk, v, lw, k, b, lincl_ref, msd_ref, mso_ref, mincl_ref, eye_ref,
                           hm_ref, hmb_ref, heads):
    c = RWKV_HEAD
    wd = heads * RWKV_HEAD
    hmb = hmb_ref[...]
    each = lambda f, *ls: [f(*a) for a in zip(*ls)]

    def stacked(z):
        zb = z.astype(BF16)
        return jnp.concatenate([zb] * heads, axis=0) * hmb

    def mm(x, y_stacked, dims=NN):
        return _dot(x.astype(BF16), y_stacked, dims)

    def head_blocks(full):
        z = full * hm_ref[...]
        out = z[0:c]
        for hh in range(1, heads):
            out = out + z[hh * c:(hh + 1) * c]
        return out

    cat0 = lambda *xs: jnp.concatenate(xs, axis=0)
    cat1 = lambda *xs: jnp.concatenate(xs, axis=1)
    mul = lambda x, y: x * y
    top = lambda z: z[:c]
    bottom = lambda z: z[c:]
    eye = eye_ref[...]
    lincl = lincl_ref[...]
    g = each(lambda x: _mm_exact_rhs_left(lincl, x), lw)
    yield
    gt = each(lambda x: jnp.sum(x, axis=0, keepdims=True), lw)
    e_x = each(lambda gi, lwi: jnp.exp(gi - lwi), g, lw)
    e_g = each(jnp.exp, g)
    e_n = each(lambda gi: jnp.exp(-gi), g)
    e_c = each(lambda gti, gi: jnp.exp(gti - gi), gt, g)
    kt = each(mul, kk, e_x)
    rt = each(mul, r, e_g)
    bh = each(mul, b, e_n)
    kh = each(mul, k, e_n)
    kb = each(mul, k, e_c)
    bb = each(mul, b, e_c)

    gm = each(lambda kti, rti, bhi, khi: mm(cat0(kti, rti), cat0(stacked(bhi), stacked(khi)).T),
              kt, rt, bh, kh)
    yield
    msd = msd_ref[...]
    mso = mso_ref[...]
    m_incl = mincl_ref[...]
    a_d = each(lambda z: z[:c, :wd] * msd, gm)
    a_o = each(lambda z: z[:c, :wd] * mso, gm)
    b_m = each(lambda z: z[:c, wd:] * (msd + mso), gm)
    e_m = each(lambda z: z[c:, :wd] * m_incl, gm)
    c_m = each(lambda z: z[c:, wd:] * m_incl, gm)

    mm_st = lambda x, y: mm(x, stacked(y))
    unzip = lambda pairs: [list(t) for t in zip(*pairs)]

    def fold(xi, di):
        z = mm(cat0(xi, di), stacked(xi))
        return z[:c], di + z[c:]

    x1 = each(lambda z: -z, a_d)
    dm = each(lambda z: eye - z, a_d)
    x2 = each(mm_st, x1, x1)
    bcv = each(lambda bi, ci, vi: mm(cat0(bi, ci), stacked(vi)), b_m, c_m, v)
    bv = each(top, bcv)
    cv = each(bottom, bcv)
    yield
    x4, dm = unzip(each(fold, x2, dm))
    yield
    x8, dm = unzip(each(fold, x4, dm))
    yield
    dm = each(lambda di, xi: di + mm_st(di, xi), dm, x8)
    yield
    n1 = each(mm_st, dm, a_o)
    yield
    n2 = each(mm_st, n1, n1)
    yield
    t1 = each(lambda z: eye - z, n1)
    t1 = each(lambda ti, ni: ti + mm_st(ti, ni), t1, n2)
    yield
    t_m = each(mm_st, t1, dm)
    yield
    et = each(mm_st, e_m, t_m)
    yield
    both = each(lambda ti, ei, kti, bvi: mm(cat0(ti, ei), cat1(stacked(kti), stacked(bvi))),
                t_m, et, kt, bv)
    yield
    wt = each(lambda z: z[:c, :wd], both)
    ut = each(lambda z: z[:c, wd:], both)
    rhat = each(lambda ri, z: ri - z[c:, :wd], rt, both)
    oloc = each(lambda ci, z: ci - z[c:, wd:], cv, both)
    full_m = each(lambda bi, wi: _dot(bi.astype(BF16), wi.astype(BF16), TN), bb, wt)
    full_n = each(lambda ki, bi, vi, ui: _dot(cat0(ki, bi).astype(BF16), cat0(vi, -ui).astype(BF16), TN),
                  kb, bb, v, ut)
    m_c = each(lambda gti, fi: eye * jnp.exp(gti) - head_blocks(fi), gt, full_m)
    n_c = each(head_blocks, full_n)
    return m_c, n_c, rhat, oloc


def _mm_exact_rhs_left(a_exact, b):
    bh, bl = _split(b)
    return _dot(a_exact, bh) + _dot(a_exact, bl)


def _rwkv_scan(r, kk, v, lw, kd, bd, rows):
    c, heads = SCAN_CHUNK, SCAN_HEADS
    wd = heads * RWKV_HEAD
    nq = RWKV_WIDTH // wd
    rb = c * SCAN_SUBCHUNKS
    assert rows.ctx_len % rb == 0 and rows.seq % rb == 0
    nc_ctx = rows.ctx_len // rb
    nc_lat = rows.seq // rb
    lat_blocks = rows.n_lat // rb

    def chunk_idx(d, bi, st):
        in_ctx = st < nc_ctx
        t_ctx = jnp.where(d == 0, st, nc_ctx - 1 - st)
        sl = st - nc_ctx
        t_lat = jnp.where(d == 0, sl, nc_lat - 1 - sl)
        return jnp.where(in_ctx, lat_blocks + bi * nc_ctx + t_ctx, bi * nc_lat + t_lat)

    assert c == RWKV_HEAD
    groups = SCAN_GROUPS
    bw = wd * groups
    shared = pl.BlockSpec((rb, bw), lambda d, bi, q, st: (chunk_idx(d, bi, st), q))
    per_dir = pl.BlockSpec((None, rb, bw), lambda d, bi, q, st: (d, chunk_idx(d, bi, st), q))
    return pl.pallas_call(
        functools.partial(_scan_kernel, chunk=c, heads=heads, groups=groups),
        grid=(2, rows.batch, nq // groups, nc_ctx + nc_lat),
        in_specs=[shared, shared, shared, per_dir, per_dir, per_dir],
        out_specs=per_dir,
        out_shape=jax.ShapeDtypeStruct((2, rows.n_all, RWKV_WIDTH), F32),
        scratch_shapes=[
            pltpu.VMEM((groups, c, wd), F32),
            pltpu.VMEM((c, c), BF16),
            pltpu.VMEM((c, wd), F32),
            pltpu.VMEM((c, wd), F32),
            pltpu.VMEM((c, wd), F32),
            pltpu.VMEM((c, wd), F32),
            pltpu.VMEM((wd, wd), F32),
            pltpu.VMEM((wd, wd), BF16),
        ],
        compiler_params=_params(("parallel", "parallel", "parallel", "arbitrary")),
        name="rwkv_scan",
    )(r, kk, v, lw, kd, bd)


def _out_kernel(attn_ref, y_ref, bonus_ref, g_ref, lnw_ref, lnb_ref, x_ref, gate_ref,
                wa_ref, wr_ref, o_ref):
    out_a = _dot(attn_ref[...], wa_ref[...])
    ones = _head_ones(2 * LANES)
    y = y_ref[0] + y_ref[1]
    mean = _head_sum(y, ones) * (1.0 / RWKV_HEAD)
    yc = y - mean
    var = _head_sum(yc * yc, ones) * (1.0 / RWKV_HEAD)
    yn = yc * lax.rsqrt(var + GN_EPS) * lnw_ref[...] + lnb_ref[...]
    rw = ((yn + bonus_ref[...]) * g_ref[...]).astype(BF16)
    out = out_a + _dot(rw, wr_ref[...])
    o_ref[...] = x_ref[...] + gate_ref[...] * out


def _mix_output(attn, y, bonus, g, ln_w, ln_b, x, mod, w_out, layer, rows, n_rows):
    d = D_MODEL
    tm = min(256, rows.tm)
    tn = d
    w = RWKV_WIDTH
    return pl.pallas_call(
        _out_kernel,
        grid=(n_rows // tm, d // tn),
        in_specs=[
            pl.BlockSpec((tm, Q_COLS), lambda i, j: (i, 0)),
            pl.BlockSpec((2, tm, w), lambda i, j: (0, i, 0)),
            pl.BlockSpec((tm, w), lambda i, j: (i, 0)),
            pl.BlockSpec((tm, w), lambda i, j: (i, 0)),
            pl.BlockSpec((1, w), lambda i, j: (0, 0)),
            pl.BlockSpec((1, w), lambda i, j: (0, 0)),
            pl.BlockSpec((tm, tn), lambda i, j: (i, j)),
            pl.BlockSpec((None, 1, tn), lambda i, j: (rows.mod_row(i, tm), 0, 2 * (d // tn) + j)),
            pl.BlockSpec((None, Q_COLS, tn), lambda i, j: (layer, 0, j)),
            pl.BlockSpec((None, w, tn), lambda i, j: (layer, 1, j)),
        ],
        out_specs=pl.BlockSpec((tm, tn), lambda i, j: (i, j)),
        out_shape=jax.ShapeDtypeStruct((n_rows, d), F32),
        compiler_params=_params(("parallel", "arbitrary")),
        name="mix_output",
    )(attn, y, bonus, g, ln_w, ln_b, x, mod, w_out, w_out)


def _ffn_kernel(x_ref, g_ref, sc_ref, sh_ref, gate_ref, w1_ref, w3_ref, w2_ref, fg_ref,
                o_ref, h_ref, acc_ref, *, final):
    j = pl.program_id(1)

    @pl.when(j == 0)
    def _():
        h_ref[...] = _norm_mod(x_ref[...], g_ref[...], sc_ref[...], sh_ref[...]).astype(BF16)
        acc_ref[...] = jnp.zeros_like(acc_ref)

    h = h_ref[...]
    u = _dot(h, w1_ref[...])
    a = (u * _sigmoid(u)) * _dot(h, w3_ref[...])
    acc_ref[...] += _dot(a.astype(BF16), w2_ref[...])

    @pl.when(j == pl.num_programs(1) - 1)
    def _():
        y = x_ref[...] + gate_ref[...] * acc_ref[...]
        if final:
            ms = jnp.mean(y * y, axis=-1, keepdims=True)
            y = y * lax.rsqrt(ms + NORM_EPS) * fg_ref[...]
        o_ref[...] = y


def _ffn(x, g, mod, w1, w3, w2, layer, final_g, rows, n_rows, final):
    tm, d = rows.tm, D_MODEL
    tf = 512
    return pl.pallas_call(
        functools.partial(_ffn_kernel, final=final),
        grid=(n_rows // tm, D_FF // tf),
        in_specs=[
            pl.BlockSpec((tm, d), lambda i, j: (i, 0)),
            pl.BlockSpec((1, d), lambda i, j: (0, 0)),
            _mod_spec(rows, 4, 2),
            _mod_spec(rows, 3, 2),
            _mod_spec(rows, 5, 2),
            pl.BlockSpec((None, d, tf), lambda i, j: (layer, 0, j)),
            pl.BlockSpec((None, d, tf), lambda i, j: (layer, 0, j)),
            pl.BlockSpec((None, tf, d), lambda i, j: (layer, j, 0)),
            pl.BlockSpec((1, d), lambda i, j: (0, 0)),
        ],
        out_specs=pl.BlockSpec((tm, d), lambda i, j: (i, 0)),
        out_shape=jax.ShapeDtypeStruct((n_rows, d), F32),
        scratch_shapes=[pltpu.VMEM((tm, d), BF16), pltpu.VMEM((tm, d), F32)],
        compiler_params=_params(("parallel", "arbitrary")),
        name="ffn_final" if final else "ffn",
    )(x, g, mod, mod, mod, w1, w3, w2, final_g)


def _rope_tables(seq, tm):
    n_rows = seq // GRID_W
    row = jnp.broadcast_to(jnp.arange(n_rows)[:, None], (n_rows, GRID_W)).reshape(-1)
    col = jnp.broadcast_to(jnp.arange(GRID_W)[None, :], (n_rows, GRID_W)).reshape(-1)
    inv = ROPE_THETA ** (-jnp.arange(ROPE_FREQS, dtype=F32) / ROPE_FREQS)
    ang = jnp.concatenate([row[:, None].astype(F32) * inv, col[:, None].astype(F32) * inv], axis=-1)
    cos, sin = jnp.cos(ang), jnp.sin(ang)
    cos2 = jnp.concatenate([cos, cos], axis=-1)
    sin2 = jnp.concatenate([-sin, sin], axis=-1)
    cos2 = jnp.concatenate([cos2, jnp.ones((tm, HEAD_DIM), F32)], axis=0)
    sin2 = jnp.concatenate([sin2, jnp.zeros((tm, HEAD_DIM), F32)], axis=0)
    return cos2, sin2


def _pad_lora_cols(a):
    parts = [a[..., :LORA_GATE]]
    pad = [(0, 0)] * (a.ndim - 1) + [(0, LORA_PAD - LORA_DECAY)]
    for q in range(4):
        lo = LORA_GATE + q * LORA_DECAY
        parts.append(jnp.pad(a[..., lo:lo + LORA_DECAY], pad))
    return jnp.concatenate(parts, axis=-1)


def _mix_weights(w2, a2):
    mats = jnp.concatenate([w2, a2], axis=0)
    return jnp.pad(mats, ((0, 0), (0, LORA_PAD - LORA_DECAY), (0, 0)))


def kernel(x, c, ctx, c_ctx, norm1_g, norm2_g, ada_w, ada_b, w_in, q_gain, k_gain, mu_prev, mu_next,
           w0, w2, a0, a2, g2, k_k, k_a, r_k, ln_x_w, ln_x_b, w_out, ffn_w1, ffn_w3, ffn_w2, final_g):
    batch, seq, d = x.shape
    ctx_len = ctx.shape[1]
    depth = w_in.shape[0]
    rows = _Rows(batch, seq, ctx_len, min(512, batch * ctx_len))

    cond = jnp.concatenate([c, c_ctx[None, :], jnp.zeros((SUBLANES - batch - 1, d), F32)], axis=0)
    mod_all = _modulation(cond, ada_w, ada_b)
    mod_all = mod_all.reshape(depth, SUBLANES, 1, 6 * d)
    cos2, sin2 = _rope_tables(seq, rows.tm)

    w_attn_b = w_in[:, :, :ATTN_COLS].astype(BF16)
    split = ATTN_COLS + RKV_COLS
    w_rwkv_b = jnp.concatenate([w_in[:, :, ATTN_COLS:split], _pad_lora_cols(w_in[:, :, split:])],
                               axis=-1).astype(BF16)
    pad_mu = lambda mu: jnp.concatenate([mu[:, :RKV_COLS], _pad_lora_cols(mu[:, RKV_COLS:])], axis=-1)
    mu_prev_p = pad_mu(mu_prev)
    mu_next_p = pad_mu(mu_next)
    w_out_b = w_out.astype(BF16)
    w1_b = ffn_w1.astype(BF16)
    w3_b = ffn_w3.astype(BF16)
    w2_b = ffn_w2.astype(BF16)

    tok = jnp.concatenate([x.reshape(batch * seq, d), ctx.reshape(batch * ctx_len, d)], axis=0)
    row1 = lambda a: a.reshape(1, -1)
    for i in range(depth):
        last = i == depth - 1
        mod = mod_all[i]
        g1 = row1(norm1_g[i])
        q, k, vt = _attn_project(tok, g1, mod, w_attn_b, i, row1(q_gain[i]), row1(k_gain[i]),
                                 cos2, sin2, rows)
        attn = _attention(q, k, vt, rows, latent=True)
        if not last:
            attn = _attention(q, k, vt, rows, latent=False, out=attn)
        prep = {
            'mu_prev': row1(mu_prev_p[i]), 'mu_next': row1(mu_next_p[i]),
            'g2': g2[i].astype(BF16), 'wmix': _mix_weights(w2[i], a2[i]).astype(BF16),
            'w0': w0[i], 'a0': a0[i],
            'k_k': row1(k_k[i]), 'k_a': row1(k_a[i]), 'r_k': row1(r_k[i]),
        }
        r_s, kk_s, v_s, g_s, bonus, lw, kd, bd = _rwkv_in(tok, g1, mod, w_rwkv_b, i, prep, rows)
        y = _rwkv_scan(r_s, kk_s, v_s, lw, kd, bd, rows)
        n_rows = rows.n_lat if last else rows.n_all
        tok = _mix_output(attn, y, bonus, g_s, row1(ln_x_w[i]), row1(ln_x_b[i]), tok, mod,
                          w_out_b, i, rows, n_rows)
        tok = _ffn(tok, row1(norm2_g[i]), mod, w1_b, w3_b, w2_b, i, row1(final_g),
                   rows, n_rows, final=last)
    return tok.reshape(batch, seq, d)
```

```python
import functools

import jax
import jax.numpy as jnp
from jax import lax
from jax.experimental import pallas as pl
from jax.experimental.pallas import tpu as pltpu

F32 = jnp.float32
BF16 = jnp.bfloat16

D_MODEL = 2048
HEAD_DIM = 128
N_Q_HEADS = 8
N_KV_HEADS = 2
GQA_GROUP = N_Q_HEADS // N_KV_HEADS
Q_COLS = N_Q_HEADS * HEAD_DIM
KV_COLS = N_KV_HEADS * HEAD_DIM
ATTN_COLS = Q_COLS + 2 * KV_COLS
RWKV_WIDTH = 1024
RWKV_HEAD = 64
LORA_GATE = 256
LORA_DECAY = 96
LORA_ICLR = 96
LORA_COLS = LORA_GATE + 2 * LORA_DECAY + 2 * LORA_ICLR
LORA_PAD = 128
LORA_COLS_PADDED = LORA_GATE + 4 * LORA_PAD
RKV_COLS = 3 * RWKV_WIDTH
D_FF = 5632
GRID_W = 64
ROPE_THETA = 10000.0
ROPE_FREQS = HEAD_DIM // 4
ATTN_SCALE = HEAD_DIM ** -0.5
LOG2_E = 1.4426950408889634
NORM_EPS = 1e-6
GN_EPS = 64e-5
L2_EPS = 1e-12
DECAY_SCALE = 0.6065306597126334

LANES = 128
SUBLANES = 8
VMEM_LIMIT = 56 * 1024 * 1024

RWKV_HEAD_SHIFT = RWKV_HEAD.bit_length() - 1

ROW_BLOCK = 512
MIX_ROWS = 256
MOD_COLS = 1024
FFN_COLS = 512
ATTN_Q_ROWS = 256
ATTN_KV_CHUNK = 1024
ATTN_LOOKAHEAD = 8
RWKV_IN_ROWS = 256
SCAN_CHUNK = 64
SCAN_HEADS = 4
INV_BLOCK = 16
SCAN_GROUPS = 4
SCAN_SUBCHUNKS = 4
SCAN_WAVE_SKEW = 2

NN = ((1,), (0,))
NT = ((1,), (1,))
TN = ((0,), (0,))


def _dot(a, b, dims=NN):
    return lax.dot_general(a, b, (dims, ((), ())), preferred_element_type=F32)


def _split(x):
    hi = x.astype(BF16)
    lo = (x - hi.astype(F32)).astype(BF16)
    return hi, lo


def _mm3(a, b, dims=NN):
    ah, al = a
    bh, bl = b
    return _dot(ah, bh, dims) + (_dot(ah, bl, dims) + _dot(al, bh, dims))


def _mm3f(a, b, dims=NN):
    return _mm3(_split(a), _split(b), dims)


def _sigmoid(z):
    return 1.0 / (1.0 + jnp.exp(-z))


def _norm_mod(x, g, scale, shift):
    ms = jnp.mean(x * x, axis=-1, keepdims=True)
    y = x * lax.rsqrt(ms + NORM_EPS) * g
    return y * (1.0 + scale) + shift


def _head_ones(width):
    r = lax.broadcasted_iota(jnp.int32, (width, width), 0) >> RWKV_HEAD_SHIFT
    c = lax.broadcasted_iota(jnp.int32, (width, width), 1) >> RWKV_HEAD_SHIFT
    return jnp.where(r == c, 1.0, 0.0).astype(BF16)


def _head_sum(x, ones):
    w = ones.shape[0]
    xb = x.astype(BF16)
    parts = [_dot(xb[:, c:c + w], ones) for c in range(0, x.shape[1], w)]
    return jnp.concatenate(parts, axis=1)


def _params(sem):
    return pltpu.CompilerParams(dimension_semantics=sem, vmem_limit_bytes=VMEM_LIMIT)


def _mod_kernel(c_ref, w_ref, b_ref, o_ref):
    c = c_ref[...]
    s = c * _sigmoid(c)
    o_ref[...] = _mm3f(s, w_ref[...]) + b_ref[...]


def _modulation(cond, ada_w, ada_b):
    depth, d, n = ada_w.shape
    tn = MOD_COLS
    rows = cond.shape[0]
    return pl.pallas_call(
        _mod_kernel,
        grid=(depth, n // tn),
        in_specs=[
            pl.BlockSpec((rows, d), lambda l, j: (0, 0)),
            pl.BlockSpec((None, d, tn), lambda l, j: (l, 0, j)),
            pl.BlockSpec((None, 1, tn), lambda l, j: (l, 0, j)),
        ],
        out_specs=pl.BlockSpec((None, rows, tn), lambda l, j: (l, 0, j)),
        out_shape=jax.ShapeDtypeStruct((depth, rows, n), F32),
        compiler_params=_params(("parallel", "parallel")),
        name="modulation",
    )(cond, ada_w, ada_b.reshape(depth, 1, n))


class _Rows:
    def __init__(self, batch, seq, ctx_len, tm):
        self.batch, self.seq, self.ctx_len, self.tm = batch, seq, ctx_len, tm
        self.n_lat = batch * seq
        self.n_ctx = batch * ctx_len
        self.n_all = self.n_lat + self.n_ctx
        assert seq % tm == 0 and self.n_ctx % tm == 0
        self.blocks_per_batch = seq // tm

    def mod_row(self, i, tm=None):
        per_batch = self.blocks_per_batch if tm is None else self.seq // tm
        return jnp.minimum(i // per_batch, self.batch)


def _mod_spec(rows, which, ngrid):
    d = D_MODEL
    if ngrid == 1:
        return pl.BlockSpec((None, 1, d), lambda i: (rows.mod_row(i), 0, which))
    return pl.BlockSpec((None, 1, d), lambda i, j: (rows.mod_row(i), 0, which))


def _attn_proj_kernel(x_ref, g_ref, sc_ref, sh_ref, w_ref, qg_ref, kg_ref, cos_ref, sin_ref,
                      q_ref, k_ref, vt_ref):
    qg = qg_ref[...]
    kg = kg_ref[...]
    half = x_ref.shape[0] // 2
    halves = [slice(0, half), slice(half, 2 * half)]
    hs = [_norm_mod(x_ref[rs, :], g_ref[...], sc_ref[...], sh_ref[...]).astype(BF16) for rs in halves]
    fs = [_dot(h, w_ref[...]) for h in hs]
    for rs, f in zip(halves, fs):
        cos = cos_ref[rs, :]
        sin = sin_ref[rs, :]

        def head(xh, gain, scale):
            ms = jnp.mean(xh * xh, axis=-1, keepdims=True)
            y = xh * lax.rsqrt(ms + NORM_EPS) * gain
            y = y * cos + pltpu.roll(y, HEAD_DIM // 2, 1) * sin
            return (y * scale).astype(BF16)

        for hq in range(N_Q_HEADS):
            c = hq * HEAD_DIM
            q_ref[rs, c:c + HEAD_DIM] = head(f[:, c:c + HEAD_DIM], qg, ATTN_SCALE * LOG2_E)
        for hk in range(N_KV_HEADS):
            c = hk * HEAD_DIM
            k_ref[rs, c:c + HEAD_DIM] = head(f[:, Q_COLS + c:Q_COLS + c + HEAD_DIM], kg, 1.0)
        vt_ref[:, rs] = f[:, Q_COLS + KV_COLS:ATTN_COLS].T.astype(BF16)


def _attn_project(x, g, mod, w, layer, q_gain, k_gain, cos2, sin2, rows):
    tm, d = rows.tm, D_MODEL
    bpb = rows.blocks_per_batch

    def rope_idx(i):
        return (jnp.where(i < rows.n_lat // tm, i % bpb, bpb), 0)

    return pl.pallas_call(
        _attn_proj_kernel,
        grid=(rows.n_all // tm,),
        in_specs=[
            pl.BlockSpec((tm, d), lambda i: (i, 0)),
            pl.BlockSpec((1, d), lambda i: (0, 0)),
            _mod_spec(rows, 1, 1),
            _mod_spec(rows, 0, 1),
            pl.BlockSpec((None, d, ATTN_COLS), lambda i: (layer, 0, 0)),
            pl.BlockSpec((1, HEAD_DIM), lambda i: (0, 0)),
            pl.BlockSpec((1, HEAD_DIM), lambda i: (0, 0)),
            pl.BlockSpec((tm, HEAD_DIM), rope_idx),
            pl.BlockSpec((tm, HEAD_DIM), rope_idx),
        ],
        out_specs=[
            pl.BlockSpec((tm, Q_COLS), lambda i: (i, 0)),
            pl.BlockSpec((tm, KV_COLS), lambda i: (i, 0)),
            pl.BlockSpec((KV_COLS, tm), lambda i: (0, i)),
        ],
        out_shape=[
            jax.ShapeDtypeStruct((rows.n_all, Q_COLS), BF16),
            jax.ShapeDtypeStruct((rows.n_all, KV_COLS), BF16),
            jax.ShapeDtypeStruct((KV_COLS, rows.n_all), BF16),
        ],
        compiler_params=_params(("parallel",)),
        name="attn_proj",
    )(x, g, mod, mod, w, q_gain, k_gain, cos2, sin2)


def _attn_kernel(q_ref, *refs, n_kv, kv_chunk):
    k_refs = refs[:n_kv]
    vt_refs = refs[n_kv:2 * n_kv]
    o_ref = refs[-1]
    tq = q_ref.shape[0]
    heads = range(GQA_GROUP)
    q = [q_ref[:, h * HEAD_DIM:(h + 1) * HEAD_DIM] for h in heads]
    m = [jnp.full((1, tq), -jnp.inf, F32) for _ in heads]
    l = [jnp.zeros((1, tq), F32) for _ in heads]
    acc = [jnp.zeros((HEAD_DIM, tq), F32) for _ in heads]
    items = []
    for k_ref, vt_ref in zip(k_refs, vt_refs):
        n = k_ref.shape[0]
        ck = min(kv_chunk, n)
        items += [(k_ref, vt_ref, c0, ck, h) for c0 in range(0, n, ck) for h in heads]

    def scores(item):
        k_ref, _, c0, ck, h = item
        return _dot(k_ref[c0:c0 + ck, :], q[h], NT)

    ahead = min(ATTN_LOOKAHEAD, len(items))
    pending = [scores(it) for it in items[:ahead]]
    for i, (_, vt_ref, c0, ck, h) in enumerate(items):
        if i + ahead < len(items):
            pending.append(scores(items[i + ahead]))
        st = pending.pop(0)
        m_new = jnp.maximum(m[h], jnp.max(st, axis=0, keepdims=True))
        alpha = jnp.exp2(m[h] - m_new)
        p = jnp.exp2(st - m_new)
        l[h] = alpha * l[h] + jnp.sum(p, axis=0, keepdims=True)
        acc[h] = alpha * acc[h] + _dot(vt_ref[:, c0:c0 + ck], p.astype(BF16))
        m[h] = m_new
    for h in heads:
        o_ref[:, h * HEAD_DIM:(h + 1) * HEAD_DIM] = (acc[h] / l[h]).T.astype(o_ref.dtype)


def _attention(q, k, v, rows, latent, out=None):
    b, s, cl = rows.batch, rows.seq, rows.ctx_len
    gw = GQA_GROUP * HEAD_DIM
    ctx_blk0 = rows.n_lat // cl
    if latent:
        tq = min(ATTN_Q_ROWS, s)
        nq = s // tq
        q_spec = pl.BlockSpec((tq, gw), lambda bi, hi, qi: (bi * nq + qi, hi))
        k_specs = [pl.BlockSpec((s, HEAD_DIM), lambda bi, hi, qi: (bi, hi)),
                   pl.BlockSpec((cl, HEAD_DIM), lambda bi, hi, qi: (ctx_blk0 + bi, hi))]
        vt_specs = [pl.BlockSpec((HEAD_DIM, s), lambda bi, hi, qi: (hi, bi)),
                    pl.BlockSpec((HEAD_DIM, cl), lambda bi, hi, qi: (hi, ctx_blk0 + bi))]
        name = "attention_latent"
    else:
        tq = cl
        nq = 1
        q_spec = pl.BlockSpec((tq, gw), lambda bi, hi, qi: (ctx_blk0 + bi, hi))
        k_specs = [pl.BlockSpec((cl, HEAD_DIM), lambda bi, hi, qi: (ctx_blk0 + bi, hi))]
        vt_specs = [pl.BlockSpec((HEAD_DIM, cl), lambda bi, hi, qi: (hi, ctx_blk0 + bi))]
        name = "attention_context"
    n_kv = len(k_specs)
    in_specs = [q_spec] + k_specs + vt_specs
    args = [q] + [k] * n_kv + [v] * n_kv
    aliases = {}
    if out is not None:
        aliases = {len(args): 0}
        in_specs.append(pl.BlockSpec(memory_space=pl.ANY))
        args.append(out)
    return pl.pallas_call(
        functools.partial(_attn_kernel, n_kv=n_kv, kv_chunk=ATTN_KV_CHUNK),
        grid=(b, N_KV_HEADS, nq),
        in_specs=in_specs,
        out_specs=q_spec,
        out_shape=jax.ShapeDtypeStruct((rows.n_all, Q_COLS), BF16),
        input_output_aliases=aliases,
        compiler_params=_params(("parallel", "parallel", "arbitrary")),
        name=name,
    )(*args)


def _rwkv_in_kernel(x_ref, xp_ref, xn_ref, g_ref, sc_ref, sh_ref, w_ref, mup_ref, mun_ref,
                    g2_ref, wmix_ref, w0_ref, a0_ref, kk_scale_ref, ka_ref, rk_ref,
                    r_o, kk_o, v_o, g_o, bonus_o, lw_o, kd_o, bd_o,
                    *, n_lat_blocks, bps_lat, bps_ctx):
    i = pl.program_id(0)
    tm = x_ref.shape[0]
    s = SUBLANES
    is_lat = i < n_lat_blocks
    j = jnp.where(is_lat, i, i - n_lat_blocks)
    bps = jnp.where(is_lat, bps_lat, bps_ctx)
    pos = j % bps
    keep_prev = jnp.where(pos == 0, 0.0, 1.0)
    keep_next = jnp.where(pos == bps - 1, 0.0, 1.0)
    row = lax.broadcasted_iota(jnp.int32, (s, 1), 0)
    first_row = jnp.where(row == 0, keep_prev, 1.0)
    last_row = jnp.where(row == s - 1, keep_next, 1.0)

    norm = lambda ref: _norm_mod(ref[...], g_ref[...], sc_ref[...], sh_ref[...]).astype(BF16)
    h_ext = jnp.concatenate([norm(xp_ref), norm(x_ref), norm(xn_ref)], axis=0)
    n_ext = tm + 2 * s
    w = RWKV_WIDTH
    panels = [slice(0, w), slice(w, 2 * w), slice(2 * w, 3 * w), slice(3 * w, 3 * w + LORA_COLS_PADDED)]
    project = lambda c: _dot(h_ext, w_ref[:, c])

    def shifted(x, c):
        prev = pltpu.roll(x, 1, 0)[s:s + tm]
        nxt = pltpu.roll(x, n_ext - 1, 0)[s:s + tm]
        prev = jnp.concatenate([prev[:s] * first_row, prev[s:]], axis=0)
        nxt = jnp.concatenate([nxt[:tm - s], nxt[tm - s:] * last_row], axis=0)
        mu_prev = mup_ref[:, c]
        mu_next = mun_ref[:, c]
        return (1.0 - mu_prev - mu_next) * x[s:s + tm] + mu_prev * prev + mu_next * nxt

    f_next = project(panels[0])
    outs = []
    for pi, c in enumerate(panels):
        f_cur = f_next
        if pi + 1 < len(panels):
            f_next = project(panels[pi + 1])
        outs.append(shifted(f_cur, c))
    r, k, v, lora = outs

    gate_in = _sigmoid(lora[:, :LORA_GATE])
    g = _dot(gate_in.astype(BF16), g2_ref[...])
    group = lambda q: lora[:, LORA_GATE + q * LORA_PAD:LORA_GATE + (q + 1) * LORA_PAD]
    decay_lr = [_dot(jnp.tanh(group(q)).astype(BF16), wmix_ref[q]) for q in range(2)]
    iclr_lr = [_dot(group(2 + q).astype(BF16), wmix_ref[2 + q]) for q in range(2)]

    ones = _head_ones(2 * LANES)
    kk = k * kk_scale_ref[...]
    kk = kk * lax.rsqrt(_head_sum(kk * kk, ones) + L2_EPS)

    r_o[...] = r
    kk_o[...] = kk
    v_o[...] = v
    g_o[...] = g
    rk = rk_ref[...]
    ka = ka_ref[...]
    bonus = jnp.zeros_like(r)
    for d in range(2):
        z = w0_ref[d:d + 1, :] + decay_lr[d]
        lw_o[d] = -DECAY_SCALE * _sigmoid(z)
        a = _sigmoid(a0_ref[d:d + 1, :] + iclr_lr[d])
        kd = k * (1.0 + (a - 1.0) * ka)
        kd_o[d] = kd
        bd_o[d] = kk * a
        bonus = bonus + _head_sum(r * kd * rk, ones) * v
    bonus_o[...] = bonus


def _rwkv_in(x, g, mod, w_all, layer, p, rows):
    tm = min(RWKV_IN_ROWS, rows.ctx_len)
    d = D_MODEL
    n_all = rows.n_all
    nblk = n_all // tm
    sub_per_blk = tm // SUBLANES
    n_sub = n_all // SUBLANES
    w = RWKV_WIDTH
    cols = RKV_COLS + LORA_COLS_PADDED

    def prev_idx(i):
        return (jnp.maximum(i * sub_per_blk - 1, 0), 0)

    def next_idx(i):
        return (jnp.minimum((i + 1) * sub_per_blk, n_sub - 1), 0)

    once = pl.Buffered(1)
    full = lambda shape: pl.BlockSpec(shape, lambda i: (0,) * len(shape), pipeline_mode=once)
    mod_spec = lambda which: pl.BlockSpec((None, 1, d), lambda i: (rows.mod_row(i, tm), 0, which))
    row_spec = lambda width: pl.BlockSpec((tm, width), lambda i: (i, 0))
    dir_spec = pl.BlockSpec((2, tm, w), lambda i: (0, i, 0))
    kern = functools.partial(_rwkv_in_kernel, n_lat_blocks=rows.n_lat // tm,
                             bps_lat=rows.seq // tm, bps_ctx=rows.ctx_len // tm)
    sds = jax.ShapeDtypeStruct
    return pl.pallas_call(
        kern,
        grid=(nblk,),
        in_specs=[
            row_spec(d), pl.BlockSpec((SUBLANES, d), prev_idx), pl.BlockSpec((SUBLANES, d), next_idx),
            full((1, d)), mod_spec(1), mod_spec(0),
            pl.BlockSpec((None, d, cols), lambda i: (layer, 0, 0), pipeline_mode=once),
            full((1, cols)), full((1, cols)),
            full((LORA_GATE, w)), full((4, LORA_PAD, w)), full((2, w)), full((2, w)),
            full((1, w)), full((1, w)), full((1, w)),
        ],
        out_specs=[row_spec(w)] * 5 + [dir_spec] * 3,
        out_shape=[sds((n_all, w), F32)] * 5 + [sds((2, n_all, w), F32)] * 3,
        compiler_params=_params(("parallel",)),
        name="rwkv_in",
    )(x, x, x, g, mod, mod, w_all, p['mu_prev'], p['mu_next'],
      p['g2'], p['wmix'], p['w0'], p['a0'], p['k_k'], p['k_a'], p['r_k'])


def _scan_kernel(r_ref, kk_ref, v_ref, lw_ref, kd_ref, bd_ref, y_ref, h_ref,
                 lincl_ref, msd_ref, mso_ref, mincl_ref, eye_ref, hm_ref, hmb_ref,
                 *, chunk, heads, groups):
    c = chunk
    wd = heads * RWKV_HEAD
    step = pl.program_id(3)

    @pl.when(step == 0)
    def _():
        h_ref[...] = jnp.zeros_like(h_ref)
        sgn = 1 - 2 * pl.program_id(0)
        iota = lambda shape, ax: lax.broadcasted_iota(jnp.int32, shape, ax)
        rel_c = (iota((c, c), 0) - iota((c, c), 1)) * sgn
        lincl_ref[...] = jnp.where(rel_c >= 0, 1.0, 0.0).astype(BF16)
        ti = iota((c, wd), 0)
        si = iota((c, wd), 1) & (c - 1)
        rel = (ti - si) * sgn
        log2b = INV_BLOCK.bit_length() - 1
        same_blk = (ti >> log2b) == (si >> log2b)
        strict = jnp.where(rel > 0, 1.0, 0.0)
        msd_ref[...] = jnp.where(same_blk, strict, 0.0)
        mso_ref[...] = jnp.where(same_blk, 0.0, strict)
        mincl_ref[...] = jnp.where(rel >= 0, 1.0, 0.0)
        eye_ref[...] = jnp.where(ti == si, 1.0, 0.0)
        hm = jnp.where((iota((wd, wd), 0) >> RWKV_HEAD_SHIFT) == (iota((wd, wd), 1) >> RWKV_HEAD_SHIFT),
                       1.0, 0.0)
        hm_ref[...] = hm
        hmb_ref[...] = hm.astype(BF16)

    n_sub = r_ref.shape[0] // c
    d = pl.program_id(0)
    rows = [pl.ds(pl.multiple_of(jnp.where(d == 0, i, n_sub - 1 - i) * c, c), c) for i in range(n_sub)]
    lanes = [slice(q * wd, (q + 1) * wd) for q in range(groups)]
    refs = (r_ref, kk_ref, v_ref, lw_ref, kd_ref, bd_ref)
    waves = [_chunk_operator_stages(*[[ref[rw, ln] for ln in lanes] for ref in refs],
                                    lincl_ref, msd_ref, mso_ref, mincl_ref, eye_ref, hm_ref, hmb_ref, heads)
             for rw in rows]
    h = [h_ref[q] for q in range(groups)]
    done = 0
    tick = 0
    while done < n_sub:
        for i in range(done, n_sub):
            if tick < i * SCAN_WAVE_SKEW:
                break
            try:
                next(waves[i])
            except StopIteration as fin:
                m_c, n_c, rhat, oloc = fin.value
                h, y = _advance(m_c, n_c, rhat, oloc, h, hmb_ref, heads)
                for q in range(groups):
                    y_ref[rows[i], lanes[q]] = y[q]
                done += 1
        tick += 1
    for q in range(groups):
        h_ref[q] = h[q]


def _advance(m_c, n_c, rhat, oloc, h, hmb_ref, heads):
    c = RWKV_HEAD
    hmb = hmb_ref[...]

    def stacked(zb):
        return jnp.concatenate([zb] * heads, axis=0) * hmb

    def step(mi, ri, hi):
        lhs = jnp.concatenate([mi, ri], axis=0).astype(BF16)
        return _dot(lhs, stacked(hi.astype(BF16)))

    res = [step(*a) for a in zip(m_c, rhat, h)]
    h_new = [z[:c] + ni for z, ni in zip(res, n_c)]
    y = [z[c:] + oi for z, oi in zip(res, oloc)]
    return h_new, y


def _chunk_operator_stages(r, kk, v, lw, k, b, lincl_ref, msd_ref, mso_ref, mincl_ref, eye_ref,
                           hm_ref, hmb_ref, heads):
    c = RWKV_HEAD
    wd = heads * RWKV_HEAD
    hmb = hmb_ref[...]
    each = lambda f, *ls: [f(*a) for a in zip(*ls)]

    def stacked(z):
        zb = z.astype(BF16)
        return jnp.concatenate([zb] * heads, axis=0) * hmb

    def mm(x, y_stacked, dims=NN):
        return _dot(x.astype(BF16), y_stacked, dims)

    def head_blocks(full):
        z = full * hm_ref[...]
        out = z[0:c]
        for hh in range(1, heads):
            out = out + z[hh * c:(hh + 1) * c]
        return out

    cat0 = lambda *xs: jnp.concatenate(xs, axis=0)
    cat1 = lambda *xs: jnp.concatenate(xs, axis=1)
    mul = lambda x, y: x * y
    top = lambda z: z[:c]
    bottom = lambda z: z[c:]
    eye = eye_ref[...]
    lincl = lincl_ref[...]
    g = each(lambda x: _mm_exact_rhs_left(lincl, x), lw)
    yield
    gt = each(lambda x: jnp.sum(x, axis=0, keepdims=True), lw)
    e_x = each(lambda gi, lwi: jnp.exp(gi - lwi), g, lw)
    e_g = each(jnp.exp, g)
    e_n = each(lambda gi: jnp.exp(-gi), g)
    e_c = each(lambda gti, gi: jnp.exp(gti - gi), gt, g)
    kt = each(mul, kk, e_x)
    rt = each(mul, r, e_g)
    bh = each(mul, b, e_n)
    kh = each(mul, k, e_n)
    kb = each(mul, k, e_c)
    bb = each(mul, b, e_c)

    gm = each(lambda kti, rti, bhi, khi: mm(cat0(kti, rti), cat0(stacked(bhi), stacked(khi)).T),
              kt, rt, bh, kh)
    yield
    msd = msd_ref[...]
    mso = mso_ref[...]
    m_incl = mincl_ref[...]
    a_d = each(lambda z: z[:c, :wd] * msd, gm)
    a_o = each(lambda z: z[:c, :wd] * mso, gm)
    b_m = each(lambda z: z[:c, wd:] * (msd + mso), gm)
    e_m = each(lambda z: z[c:, :wd] * m_incl, gm)
    c_m = each(lambda z: z[c:, wd:] * m_incl, gm)

    mm_st = lambda x, y: mm(x, stacked(y))
    unzip = lambda pairs: [list(t) for t in zip(*pairs)]

    def fold(xi, di):
        z = mm(cat0(xi, di), stacked(xi))
        return z[:c], di + z[c:]

    x1 = each(lambda z: -z, a_d)
    dm = each(lambda z: eye - z, a_d)
    x2 = each(mm_st, x1, x1)
    bcv = each(lambda bi, ci, vi: mm(cat0(bi, ci), stacked(vi)), b_m, c_m, v)
    bv = each(top, bcv)
    cv = each(bottom, bcv)
    yield
    x4, dm = unzip(each(fold, x2, dm))
    yield
    x8, dm = unzip(each(fold, x4, dm))
    yield
    dm = each(lambda di, xi: di + mm_st(di, xi), dm, x8)
    yield
    n1 = each(mm_st, dm, a_o)
    yield
    n2 = each(mm_st, n1, n1)
    yield
    t1 = each(lambda z: eye - z, n1)
    t1 = each(lambda ti, ni: ti + mm_st(ti, ni), t1, n2)
    yield
    t_m = each(mm_st, t1, dm)
    yield
    et = each(mm_st, e_m, t_m)
    yield
    both = each(lambda ti, ei, kti, bvi: mm(cat0(ti, ei), cat1(stacked(kti), stacked(bvi))),
                t_m, et, kt, bv)
    yield
    wt = each(lambda z: z[:c, :wd], both)
    ut = each(lambda z: z[:c, wd:], both)
    rhat = each(lambda ri, z: ri - z[c:, :wd], rt, both)
    oloc = each(lambda ci, z: ci - z[c:, wd:], cv, both)
    full_m = each(lambda bi, wi: _dot(bi.astype(BF16), wi.astype(BF16), TN), bb, wt)
    full_n = each(lambda ki, bi, vi, ui: _dot(cat0(ki, bi).astype(BF16), cat0(vi, -ui).astype(BF16), TN),
                  kb, bb, v, ut)
    m_c = each(lambda gti, fi: eye * jnp.exp(gti) - head_blocks(fi), gt, full_m)
    n_c = each(head_blocks, full_n)
    return m_c, n_c, rhat, oloc


def _mm_exact_rhs_left(a_exact, b):
    bh, bl = _split(b)
    return _dot(a_exact, bh) + _dot(a_exact, bl)


def _rwkv_scan(r, kk, v, lw, kd, bd, rows):
    c, heads = SCAN_CHUNK, SCAN_HEADS
    wd = heads * RWKV_HEAD
    nq = RWKV_WIDTH // wd
    rb = c * SCAN_SUBCHUNKS
    assert rows.ctx_len % rb == 0 and rows.seq % rb == 0
    nc_ctx = rows.ctx_len // rb
    nc_lat = rows.seq // rb
    lat_blocks = rows.n_lat // rb

    def chunk_idx(d, bi, st):
        in_ctx = st < nc_ctx
        t_ctx = jnp.where(d == 0, st, nc_ctx - 1 - st)
        sl = st - nc_ctx
        t_lat = jnp.where(d == 0, sl, nc_lat - 1 - sl)
        return jnp.where(in_ctx, lat_blocks + bi * nc_ctx + t_ctx, bi * nc_lat + t_lat)

    assert c == RWKV_HEAD
    groups = SCAN_GROUPS
    bw = wd * groups
    shared = pl.BlockSpec((rb, bw), lambda d, bi, q, st: (chunk_idx(d, bi, st), q))
    per_dir = pl.BlockSpec((None, rb, bw), lambda d, bi, q, st: (d, chunk_idx(d, bi, st), q))
    return pl.pallas_call(
        functools.partial(_scan_kernel, chunk=c, heads=heads, groups=groups),
        grid=(2, rows.batch, nq // groups, nc_ctx + nc_lat),
        in_specs=[shared, shared, shared, per_dir, per_dir, per_dir],
        out_specs=per_dir,
        out_shape=jax.ShapeDtypeStruct((2, rows.n_all, RWKV_WIDTH), F32),
        scratch_shapes=[
            pltpu.VMEM((groups, c, wd), F32),
            pltpu.VMEM((c, c), BF16),
            pltpu.VMEM((c, wd), F32),
            pltpu.VMEM((c, wd), F32),
            pltpu.VMEM((c, wd), F32),
            pltpu.VMEM((c, wd), F32),
            pltpu.VMEM((wd, wd), F32),
            pltpu.VMEM((wd, wd), BF16),
        ],
        compiler_params=_params(("parallel", "parallel", "parallel", "arbitrary")),
        name="rwkv_scan",
    )(r, kk, v, lw, kd, bd)


def _out_kernel(attn_ref, y_ref, bonus_ref, g_ref, lnw_ref, lnb_ref, x_ref, gate_ref,
                wa_ref, wr_ref, o_ref):
    out_a = _dot(attn_ref[...], wa_ref[...])
    ones = _head_ones(2 * LANES)
    y = y_ref[0] + y_ref[1]
    mean = _head_sum(y, ones) * (1.0 / RWKV_HEAD)
    yc = y - mean
    var = _head_sum(yc * yc, ones) * (1.0 / RWKV_HEAD)
    yn = yc * lax.rsqrt(var + GN_EPS) * lnw_ref[...] + lnb_ref[...]
    rw = ((yn + bonus_ref[...]) * g_ref[...]).astype(BF16)
    out = out_a + _dot(rw, wr_ref[...])
    o_ref[...] = x_ref[...] + gate_ref[...] * out


def _mix_output(attn, y, bonus, g, ln_w, ln_b, x, mod, w_out, layer, rows, n_rows):
    d = D_MODEL
    tm = min(MIX_ROWS, rows.tm)
    tn = d
    w = RWKV_WIDTH
    return pl.pallas_call(
        _out_kernel,
        grid=(n_rows // tm, d // tn),
        in_specs=[
            pl.BlockSpec((tm, Q_COLS), lambda i, j: (i, 0)),
            pl.BlockSpec((2, tm, w), lambda i, j: (0, i, 0)),
            pl.BlockSpec((tm, w), lambda i, j: (i, 0)),
            pl.BlockSpec((tm, w), lambda i, j: (i, 0)),
            pl.BlockSpec((1, w), lambda i, j: (0, 0)),
            pl.BlockSpec((1, w), lambda i, j: (0, 0)),
            pl.BlockSpec((tm, tn), lambda i, j: (i, j)),
            pl.BlockSpec((None, 1, tn), lambda i, j: (rows.mod_row(i, tm), 0, 2 * (d // tn) + j)),
            pl.BlockSpec((None, Q_COLS, tn), lambda i, j: (layer, 0, j)),
            pl.BlockSpec((None, w, tn), lambda i, j: (layer, 1, j)),
        ],
        out_specs=pl.BlockSpec((tm, tn), lambda i, j: (i, j)),
        out_shape=jax.ShapeDtypeStruct((n_rows, d), F32),
        compiler_params=_params(("parallel", "arbitrary")),
        name="mix_output",
    )(attn, y, bonus, g, ln_w, ln_b, x, mod, w_out, w_out)


def _ffn_kernel(x_ref, g_ref, sc_ref, sh_ref, gate_ref, w1_ref, w3_ref, w2_ref, fg_ref,
                o_ref, h_ref, acc_ref, *, final):
    j = pl.program_id(1)

    @pl.when(j == 0)
    def _():
        h_ref[...] = _norm_mod(x_ref[...], g_ref[...], sc_ref[...], sh_ref[...]).astype(BF16)
        acc_ref[...] = jnp.zeros_like(acc_ref)

    h = h_ref[...]
    u = _dot(h, w1_ref[...])
    a = (u * _sigmoid(u)) * _dot(h, w3_ref[...])
    acc_ref[...] += _dot(a.astype(BF16), w2_ref[...])

    @pl.when(j == pl.num_programs(1) - 1)
    def _():
        y = x_ref[...] + gate_ref[...] * acc_ref[...]
        if final:
            ms = jnp.mean(y * y, axis=-1, keepdims=True)
            y = y * lax.rsqrt(ms + NORM_EPS) * fg_ref[...]
        o_ref[...] = y


def _ffn(x, g, mod, w1, w3, w2, layer, final_g, rows, n_rows, final):
    tm, d = rows.tm, D_MODEL
    tf = FFN_COLS
    return pl.pallas_call(
        functools.partial(_ffn_kernel, final=final),
        grid=(n_rows // tm, D_FF // tf),
        in_specs=[
            pl.BlockSpec((tm, d), lambda i, j: (i, 0)),
            pl.BlockSpec((1, d), lambda i, j: (0, 0)),
            _mod_spec(rows, 4, 2),
            _mod_spec(rows, 3, 2),
            _mod_spec(rows, 5, 2),
            pl.BlockSpec((None, d, tf), lambda i, j: (layer, 0, j)),
            pl.BlockSpec((None, d, tf), lambda i, j: (layer, 0, j)),
            pl.BlockSpec((None, tf, d), lambda i, j: (layer, j, 0)),
            pl.BlockSpec((1, d), lambda i, j: (0, 0)),
        ],
        out_specs=pl.BlockSpec((tm, d), lambda i, j: (i, 0)),
        out_shape=jax.ShapeDtypeStruct((n_rows, d), F32),
        scratch_shapes=[pltpu.VMEM((tm, d), BF16), pltpu.VMEM((tm, d), F32)],
        compiler_params=_params(("parallel", "arbitrary")),
        name="ffn_final" if final else "ffn",
    )(x, g, mod, mod, mod, w1, w3, w2, final_g)


def _rope_tables(seq, tm):
    n_rows = seq // GRID_W
    row = jnp.broadcast_to(jnp.arange(n_rows)[:, None], (n_rows, GRID_W)).reshape(-1)
    col = jnp.broadcast_to(jnp.arange(GRID_W)[None, :], (n_rows, GRID_W)).reshape(-1)
    inv = ROPE_THETA ** (-jnp.arange(ROPE_FREQS, dtype=F32) / ROPE_FREQS)
    ang = jnp.concatenate([row[:, None].astype(F32) * inv, col[:, None].astype(F32) * inv], axis=-1)
    cos, sin = jnp.cos(ang), jnp.sin(ang)
    cos2 = jnp.concatenate([cos, cos], axis=-1)
    sin2 = jnp.concatenate([-sin, sin], axis=-1)
    cos2 = jnp.concatenate([cos2, jnp.ones((tm, HEAD_DIM), F32)], axis=0)
    sin2 = jnp.concatenate([sin2, jnp.zeros((tm, HEAD_DIM), F32)], axis=0)
    return cos2, sin2


def _pad_lora_cols(a):
    parts = [a[..., :LORA_GATE]]
    pad = [(0, 0)] * (a.ndim - 1) + [(0, LORA_PAD - LORA_DECAY)]
    for q in range(4):
        lo = LORA_GATE + q * LORA_DECAY
        parts.append(jnp.pad(a[..., lo:lo + LORA_DECAY], pad))
    return jnp.concatenate(parts, axis=-1)


def _mix_weights(w2, a2):
    mats = jnp.concatenate([w2, a2], axis=0)
    return jnp.pad(mats, ((0, 0), (0, LORA_PAD - LORA_DECAY), (0, 0)))


def kernel(x, c, ctx, c_ctx, norm1_g, norm2_g, ada_w, ada_b, w_in, q_gain, k_gain, mu_prev, mu_next,
           w0, w2, a0, a2, g2, k_k, k_a, r_k, ln_x_w, ln_x_b, w_out, ffn_w1, ffn_w3, ffn_w2, final_g):
    batch, seq, d = x.shape
    ctx_len = ctx.shape[1]
    depth = w_in.shape[0]
    rows = _Rows(batch, seq, ctx_len, min(ROW_BLOCK, batch * ctx_len))

    cond = jnp.concatenate([c, c_ctx[None, :], jnp.zeros((SUBLANES - batch - 1, d), F32)], axis=0)
    mod_all = _modulation(cond, ada_w, ada_b)
    mod_all = mod_all.reshape(depth, SUBLANES, 1, 6 * d)
    cos2, sin2 = _rope_tables(seq, rows.tm)

    w_attn_b = w_in[:, :, :ATTN_COLS].astype(BF16)
    split = ATTN_COLS + RKV_COLS
    w_rwkv_b = jnp.concatenate([w_in[:, :, ATTN_COLS:split], _pad_lora_cols(w_in[:, :, split:])],
                               axis=-1).astype(BF16)
    pad_mu = lambda mu: jnp.concatenate([mu[:, :RKV_COLS], _pad_lora_cols(mu[:, RKV_COLS:])], axis=-1)
    mu_prev_p = pad_mu(mu_prev)
    mu_next_p = pad_mu(mu_next)
    w_out_b = w_out.astype(BF16)
    w1_b = ffn_w1.astype(BF16)
    w3_b = ffn_w3.astype(BF16)
    w2_b = ffn_w2.astype(BF16)

    tok = jnp.concatenate([x.reshape(batch * seq, d), ctx.reshape(batch * ctx_len, d)], axis=0)
    row1 = lambda a: a.reshape(1, -1)
    for i in range(depth):
        last = i == depth - 1
        mod = mod_all[i]
        g1 = row1(norm1_g[i])
        q, k, vt = _attn_project(tok, g1, mod, w_attn_b, i, row1(q_gain[i]), row1(k_gain[i]),
                                 cos2, sin2, rows)
        attn = _attention(q, k, vt, rows, latent=True)
        if not last:
            attn = _attention(q, k, vt, rows, latent=False, out=attn)
        prep = {
            'mu_prev': row1(mu_prev_p[i]), 'mu_next': row1(mu_next_p[i]),
            'g2': g2[i].astype(BF16), 'wmix': _mix_weights(w2[i], a2[i]).astype(BF16),
            'w0': w0[i], 'a0': a0[i],
            'k_k': row1(k_k[i]), 'k_a': row1(k_a[i]), 'r_k': row1(r_k[i]),
        }
        r_s, kk_s, v_s, g_s, bonus, lw, kd, bd = _rwkv_in(tok, g1, mod, w_rwkv_b, i, prep, rows)
        y = _rwkv_scan(r_s, kk_s, v_s, lw, kd, bd, rows)
        n_rows = rows.n_lat if last else rows.n_all
        tok = _mix_output(attn, y, bonus, g_s, row1(ln_x_w[i]), row1(ln_x_b[i]), tok, mod,
                          w_out_b, i, rows, n_rows)
        tok = _ffn(tok, row1(norm2_g[i]), mod, w1_b, w3_b, w2_b, i, row1(final_g),
                   rows, n_rows, final=last)
    return tok.reshape(batch, seq, d)
```

```python
import functools

import jax
import jax.numpy as jnp
from jax import lax
from jax.experimental import pallas as pl
from jax.experimental.pallas import tpu as pltpu

F32 = jnp.float32
BF16 = jnp.bfloat16

D_MODEL = 2048
HEAD_DIM = 128
N_Q_HEADS = 8
N_KV_HEADS = 2
GQA_GROUP = N_Q_HEADS // N_KV_HEADS
Q_COLS = N_Q_HEADS * HEAD_DIM
KV_COLS = N_KV_HEADS * HEAD_DIM
ATTN_COLS = Q_COLS + 2 * KV_COLS
RWKV_WIDTH = 1024
RWKV_HEAD = 64
LORA_GATE = 256
LORA_DECAY = 96
LORA_ICLR = 96
LORA_COLS = LORA_GATE + 2 * LORA_DECAY + 2 * LORA_ICLR
LORA_PAD = 128
LORA_COLS_PADDED = LORA_GATE + 4 * LORA_PAD
RKV_COLS = 3 * RWKV_WIDTH
D_FF = 5632
GRID_W = 64
ROPE_THETA = 10000.0
ROPE_FREQS = HEAD_DIM // 4
ATTN_SCALE = HEAD_DIM ** -0.5
LOG2_E = 1.4426950408889634
NORM_EPS = 1e-6
GN_EPS = 64e-5
L2_EPS = 1e-12
DECAY_SCALE = 0.6065306597126334

LANES = 128
SUBLANES = 8
VMEM_LIMIT = 56 * 1024 * 1024

RWKV_HEAD_SHIFT = RWKV_HEAD.bit_length() - 1

ROW_BLOCK = 512
MIX_ROWS = 256
MOD_COLS = 1024
FFN_COLS = 512
ATTN_Q_ROWS = 256
ATTN_KV_CHUNK = 1024
ATTN_LOOKAHEAD = 8
RWKV_IN_ROWS = 256
SCAN_CHUNK = 64
SCAN_HEADS = 4
INV_BLOCK = 16
SCAN_GROUPS = 4
SCAN_SUBCHUNKS = 4
SCAN_WAVE_SKEW = 2

NN = ((1,), (0,))
NT = ((1,), (1,))
TN = ((0,), (0,))


def _dot(a, b, dims=NN):
    return lax.dot_general(a, b, (dims, ((), ())), preferred_element_type=F32)


def _split(x):
    hi = x.astype(BF16)
    lo = (x - hi.astype(F32)).astype(BF16)
    return hi, lo


def _mm3(a, b, dims=NN):
    ah, al = a
    bh, bl = b
    return _dot(ah, bh, dims) + (_dot(ah, bl, dims) + _dot(al, bh, dims))


def _mm3f(a, b, dims=NN):
    return _mm3(_split(a), _split(b), dims)


def _sigmoid(z):
    return 1.0 / (1.0 + jnp.exp(-z))


def _norm_mod(x, g, scale, shift):
    ms = jnp.mean(x * x, axis=-1, keepdims=True)
    y = x * lax.rsqrt(ms + NORM_EPS) * g
    return y * (1.0 + scale) + shift


def _head_ones(width):
    r = lax.broadcasted_iota(jnp.int32, (width, width), 0) >> RWKV_HEAD_SHIFT
    c = lax.broadcasted_iota(jnp.int32, (width, width), 1) >> RWKV_HEAD_SHIFT
    return jnp.where(r == c, 1.0, 0.0).astype(BF16)


def _head_sum(x, ones):
    w = ones.shape[0]
    xb = x.astype(BF16)
    parts = [_dot(xb[:, c:c + w], ones) for c in range(0, x.shape[1], w)]
    return jnp.concatenate(parts, axis=1)


def _params(sem):
    return pltpu.CompilerParams(dimension_semantics=sem, vmem_limit_bytes=VMEM_LIMIT)


def _mod_kernel(c_ref, w_ref, b_ref, o_ref):
    c = c_ref[...]
    s = c * _sigmoid(c)
    o_ref[...] = _mm3f(s, w_ref[...]) + b_ref[...]


def _modulation(cond, ada_w, ada_b):
    depth, d, n = ada_w.shape
    tn = MOD_COLS
    rows = cond.shape[0]
    return pl.pallas_call(
        _mod_kernel,
        grid=(depth, n // tn),
        in_specs=[
            pl.BlockSpec((rows, d), lambda l, j: (0, 0)),
            pl.BlockSpec((None, d, tn), lambda l, j: (l, 0, j)),
            pl.BlockSpec((None, 1, tn), lambda l, j: (l, 0, j)),
        ],
        out_specs=pl.BlockSpec((None, rows, tn), lambda l, j: (l, 0, j)),
        out_shape=jax.ShapeDtypeStruct((depth, rows, n), F32),
        compiler_params=_params(("parallel", "parallel")),
        name="modulation",
    )(cond, ada_w, ada_b.reshape(depth, 1, n))


class _Rows:
    def __init__(self, batch, seq, ctx_len, tm):
        self.batch, self.seq, self.ctx_len, self.tm = batch, seq, ctx_len, tm
        self.n_lat = batch * seq
        self.n_ctx = batch * ctx_len
        self.n_all = self.n_lat + self.n_ctx
        assert seq % tm == 0 and self.n_ctx % tm == 0
        self.blocks_per_batch = seq // tm

    def mod_row(self, i, tm=None):
        per_batch = self.blocks_per_batch if tm is None else self.seq // tm
        return jnp.minimum(i // per_batch, self.batch)


def _mod_spec(rows, which, ngrid):
    d = D_MODEL
    if ngrid == 1:
        return pl.BlockSpec((None, 1, d), lambda i: (rows.mod_row(i), 0, which))
    return pl.BlockSpec((None, 1, d), lambda i, j: (rows.mod_row(i), 0, which))


def _attn_proj_kernel(x_ref, g_ref, sc_ref, sh_ref, w_ref, qg_ref, kg_ref, cos_ref, sin_ref,
                      q_ref, k_ref, vt_ref):
    qg = qg_ref[...]
    kg = kg_ref[...]
    half = x_ref.shape[0] // 2
    halves = [slice(0, half), slice(half, 2 * half)]
    hs = [_norm_mod(x_ref[rs, :], g_ref[...], sc_ref[...], sh_ref[...]).astype(BF16) for rs in halves]
    fs = [_dot(h, w_ref[...]) for h in hs]
    for rs, f in zip(halves, fs):
        cos = cos_ref[rs, :]
        sin = sin_ref[rs, :]

        def head(xh, gain, scale):
            ms = jnp.mean(xh * xh, axis=-1, keepdims=True)
            y = xh * lax.rsqrt(ms + NORM_EPS) * gain
            y = y * cos + pltpu.roll(y, HEAD_DIM // 2, 1) * sin
            return (y * scale).astype(BF16)

        for hq in range(N_Q_HEADS):
            c = hq * HEAD_DIM
            q_ref[rs, c:c + HEAD_DIM] = head(f[:, c:c + HEAD_DIM], qg, ATTN_SCALE * LOG2_E)
        for hk in range(N_KV_HEADS):
            c = hk * HEAD_DIM
            k_ref[rs, c:c + HEAD_DIM] = head(f[:, Q_COLS + c:Q_COLS + c + HEAD_DIM], kg, 1.0)
        vt_ref[:, rs] = f[:, Q_COLS + KV_COLS:ATTN_COLS].T.astype(BF16)


def _attn_project(x, g, mod, w, layer, q_gain, k_gain, cos2, sin2, rows):
    tm, d = rows.tm, D_MODEL
    bpb = rows.blocks_per_batch

    def rope_idx(i):
        return (jnp.where(i < rows.n_lat // tm, i % bpb, bpb), 0)

    return pl.pallas_call(
        _attn_proj_kernel,
        grid=(rows.n_all // tm,),
        in_specs=[
            pl.BlockSpec((tm, d), lambda i: (i, 0)),
            pl.BlockSpec((1, d), lambda i: (0, 0)),
            _mod_spec(rows, 1, 1),
            _mod_spec(rows, 0, 1),
            pl.BlockSpec((None, d, ATTN_COLS), lambda i: (layer, 0, 0)),
            pl.BlockSpec((1, HEAD_DIM), lambda i: (0, 0)),
            pl.BlockSpec((1, HEAD_DIM), lambda i: (0, 0)),
            pl.BlockSpec((tm, HEAD_DIM), rope_idx),
            pl.BlockSpec((tm, HEAD_DIM), rope_idx),
        ],
        out_specs=[
            pl.BlockSpec((tm, Q_COLS), lambda i: (i, 0)),
            pl.BlockSpec((tm, KV_COLS), lambda i: (i, 0)),
            pl.BlockSpec((KV_COLS, tm), lambda i: (0, i)),
        ],
        out_shape=[
            jax.ShapeDtypeStruct((rows.n_all, Q_COLS), BF16),
            jax.ShapeDtypeStruct((rows.n_all, KV_COLS), BF16),
            jax.ShapeDtypeStruct((KV_COLS, rows.n_all), BF16),
        ],
        compiler_params=_params(("parallel",)),
        name="attn_proj",
    )(x, g, mod, mod, w, q_gain, k_gain, cos2, sin2)


def _attn_kernel(q_ref, *refs, n_kv, kv_chunk):
    k_refs = refs[:n_kv]
    vt_refs = refs[n_kv:2 * n_kv]
    o_ref = refs[-1]
    tq = q_ref.shape[0]
    heads = range(GQA_GROUP)
    q = [q_ref[:, h * HEAD_DIM:(h + 1) * HEAD_DIM] for h in heads]
    m = [jnp.full((1, tq), -jnp.inf, F32) for _ in heads]
    l = [jnp.zeros((1, tq), F32) for _ in heads]
    acc = [jnp.zeros((HEAD_DIM, tq), F32) for _ in heads]
    items = []
    for k_ref, vt_ref in zip(k_refs, vt_refs):
        n = k_ref.shape[0]
        ck = min(kv_chunk, n)
        items += [(k_ref, vt_ref, c0, ck, h) for c0 in range(0, n, ck) for h in heads]

    def scores(item):
        k_ref, _, c0, ck, h = item
        return _dot(k_ref[c0:c0 + ck, :], q[h], NT)

    ahead = min(ATTN_LOOKAHEAD, len(items))
    pending = [scores(it) for it in items[:ahead]]
    for i, (_, vt_ref, c0, ck, h) in enumerate(items):
        if i + ahead < len(items):
            pending.append(scores(items[i + ahead]))
        st = pending.pop(0)
        m_new = jnp.maximum(m[h], jnp.max(st, axis=0, keepdims=True))
        alpha = jnp.exp2(m[h] - m_new)
        p = jnp.exp2(st - m_new)
        l[h] = alpha * l[h] + jnp.sum(p, axis=0, keepdims=True)
        acc[h] = alpha * acc[h] + _dot(vt_ref[:, c0:c0 + ck], p.astype(BF16))
        m[h] = m_new
    for h in heads:
        o_ref[:, h * HEAD_DIM:(h + 1) * HEAD_DIM] = (acc[h] / l[h]).T.astype(o_ref.dtype)


def _attention(q, k, v, rows, latent, out=None):
    b, s, cl = rows.batch, rows.seq, rows.ctx_len
    gw = GQA_GROUP * HEAD_DIM
    ctx_blk0 = rows.n_lat // cl
    if latent:
        tq = min(ATTN_Q_ROWS, s)
        nq = s // tq
        q_spec = pl.BlockSpec((tq, gw), lambda bi, hi, qi: (bi * nq + qi, hi))
        k_specs = [pl.BlockSpec((s, HEAD_DIM), lambda bi, hi, qi: (bi, hi)),
                   pl.BlockSpec((cl, HEAD_DIM), lambda bi, hi, qi: (ctx_blk0 + bi, hi))]
        vt_specs = [pl.BlockSpec((HEAD_DIM, s), lambda bi, hi, qi: (hi, bi)),
                    pl.BlockSpec((HEAD_DIM, cl), lambda bi, hi, qi: (hi, ctx_blk0 + bi))]
        name = "attention_latent"
    else:
        tq = cl
        nq = 1
        q_spec = pl.BlockSpec((tq, gw), lambda bi, hi, qi: (ctx_blk0 + bi, hi))
        k_specs = [pl.BlockSpec((cl, HEAD_DIM), lambda bi, hi, qi: (ctx_blk0 + bi, hi))]
        vt_specs = [pl.BlockSpec((HEAD_DIM, cl), lambda bi, hi, qi: (hi, ctx_blk0 + bi))]
        name = "attention_context"
    n_kv = len(k_specs)
    in_specs = [q_spec] + k_specs + vt_specs
    args = [q] + [k] * n_kv + [v] * n_kv
    aliases = {}
    if out is not None:
        aliases = {len(args): 0}
        in_specs.append(pl.BlockSpec(memory_space=pl.ANY))
        args.append(out)
    return pl.pallas_call(
        functools.partial(_attn_kernel, n_kv=n_kv, kv_chunk=ATTN_KV_CHUNK),
        grid=(b, N_KV_HEADS, nq),
        in_specs=in_specs,
        out_specs=q_spec,
        out_shape=jax.ShapeDtypeStruct((rows.n_all, Q_COLS), BF16),
        input_output_aliases=aliases,
        compiler_params=_params(("parallel", "parallel", "arbitrary")),
        name=name,
    )(*args)


def _rwkv_in_kernel(x_ref, xp_ref, xn_ref, g_ref, sc_ref, sh_ref, w_ref, mup_ref, mun_ref,
                    g2_ref, wmix_ref, w0_ref, a0_ref, kk_scale_ref, ka_ref, rk_ref,
                    r_o, kk_o, v_o, g_o, bonus_o, lw_o, kd_o, bd_o,
                    *, n_lat_blocks, bps_lat, bps_ctx):
    i = pl.program_id(0)
    tm = x_ref.shape[0]
    s = SUBLANES
    is_lat = i < n_lat_blocks
    j = jnp.where(is_lat, i, i - n_lat_blocks)
    bps = jnp.where(is_lat, bps_lat, bps_ctx)
    pos = j % bps
    keep_prev = jnp.where(pos == 0, 0.0, 1.0)
    keep_next = jnp.where(pos == bps - 1, 0.0, 1.0)
    row = lax.broadcasted_iota(jnp.int32, (s, 1), 0)
    first_row = jnp.where(row == 0, keep_prev, 1.0)
    last_row = jnp.where(row == s - 1, keep_next, 1.0)

    norm = lambda ref: _norm_mod(ref[...], g_ref[...], sc_ref[...], sh_ref[...]).astype(BF16)
    h_ext = jnp.concatenate([norm(xp_ref), norm(x_ref), norm(xn_ref)], axis=0)
    n_ext = tm + 2 * s
    w = RWKV_WIDTH
    panels = [slice(0, w), slice(w, 2 * w), slice(2 * w, 3 * w), slice(3 * w, 3 * w + LORA_COLS_PADDED)]
    project = lambda c: _dot(h_ext, w_ref[:, c])

    def shifted(x, c):
        prev = pltpu.roll(x, 1, 0)[s:s + tm]
        nxt = pltpu.roll(x, n_ext - 1, 0)[s:s + tm]
        prev = jnp.concatenate([prev[:s] * first_row, prev[s:]], axis=0)
        nxt = jnp.concatenate([nxt[:tm - s], nxt[tm - s:] * last_row], axis=0)
        mu_prev = mup_ref[:, c]
        mu_next = mun_ref[:, c]
        return (1.0 - mu_prev - mu_next) * x[s:s + tm] + mu_prev * prev + mu_next * nxt

    f_next = project(panels[0])
    outs = []
    for pi, c in enumerate(panels):
        f_cur = f_next
        if pi + 1 < len(panels):
            f_next = project(panels[pi + 1])
        outs.append(shifted(f_cur, c))
    r, k, v, lora = outs

    gate_in = _sigmoid(lora[:, :LORA_GATE])
    g = _dot(gate_in.astype(BF16), g2_ref[...])
    group = lambda q: lora[:, LORA_GATE + q * LORA_PAD:LORA_GATE + (q + 1) * LORA_PAD]
    decay_lr = [_dot(jnp.tanh(group(q)).astype(BF16), wmix_ref[q]) for q in range(2)]
    iclr_lr = [_dot(group(2 + q).astype(BF16), wmix_ref[2 + q]) for q in range(2)]

    ones = _head_ones(2 * LANES)
    kk = k * kk_scale_ref[...]
    kk = kk * lax.rsqrt(_head_sum(kk * kk, ones) + L2_EPS)

    r_o[...] = r.astype(r_o.dtype)
    kk_o[...] = kk.astype(kk_o.dtype)
    v_o[...] = v.astype(v_o.dtype)
    g_o[...] = g
    rk = rk_ref[...]
    ka = ka_ref[...]
    bonus = jnp.zeros_like(r)
    for d in range(2):
        z = w0_ref[d:d + 1, :] + decay_lr[d]
        lw_o[d] = -DECAY_SCALE * _sigmoid(z)
        a = _sigmoid(a0_ref[d:d + 1, :] + iclr_lr[d])
        kd = k * (1.0 + (a - 1.0) * ka)
        kd_o[d] = kd.astype(kd_o.dtype)
        bd_o[d] = (kk * a).astype(bd_o.dtype)
        bonus = bonus + _head_sum(r * kd * rk, ones) * v
    bonus_o[...] = bonus


def _rwkv_in(x, g, mod, w_all, layer, p, rows):
    tm = min(RWKV_IN_ROWS, rows.ctx_len)
    d = D_MODEL
    n_all = rows.n_all
    nblk = n_all // tm
    sub_per_blk = tm // SUBLANES
    n_sub = n_all // SUBLANES
    w = RWKV_WIDTH
    cols = RKV_COLS + LORA_COLS_PADDED

    def prev_idx(i):
        return (jnp.maximum(i * sub_per_blk - 1, 0), 0)

    def next_idx(i):
        return (jnp.minimum((i + 1) * sub_per_blk, n_sub - 1), 0)

    once = pl.Buffered(1)
    full = lambda shape: pl.BlockSpec(shape, lambda i: (0,) * len(shape), pipeline_mode=once)
    mod_spec = lambda which: pl.BlockSpec((None, 1, d), lambda i: (rows.mod_row(i, tm), 0, which))
    row_spec = lambda width: pl.BlockSpec((tm, width), lambda i: (i, 0))
    dir_spec = pl.BlockSpec((2, tm, w), lambda i: (0, i, 0))
    kern = functools.partial(_rwkv_in_kernel, n_lat_blocks=rows.n_lat // tm,
                             bps_lat=rows.seq // tm, bps_ctx=rows.ctx_len // tm)
    sds = jax.ShapeDtypeStruct
    return pl.pallas_call(
        kern,
        grid=(nblk,),
        in_specs=[
            row_spec(d), pl.BlockSpec((SUBLANES, d), prev_idx), pl.BlockSpec((SUBLANES, d), next_idx),
            full((1, d)), mod_spec(1), mod_spec(0),
            pl.BlockSpec((None, d, cols), lambda i: (layer, 0, 0), pipeline_mode=once),
            full((1, cols)), full((1, cols)),
            full((LORA_GATE, w)), full((4, LORA_PAD, w)), full((2, w)), full((2, w)),
            full((1, w)), full((1, w)), full((1, w)),
        ],
        out_specs=[row_spec(w)] * 5 + [dir_spec] * 3,
        out_shape=[sds((n_all, w), BF16)] * 3 + [sds((n_all, w), F32)] * 2
        + [sds((2, n_all, w), F32)] + [sds((2, n_all, w), BF16)] * 2,
        compiler_params=_params(("parallel",)),
        name="rwkv_in",
    )(x, x, x, g, mod, mod, w_all, p['mu_prev'], p['mu_next'],
      p['g2'], p['wmix'], p['w0'], p['a0'], p['k_k'], p['k_a'], p['r_k'])


def _scan_kernel(r_ref, kk_ref, v_ref, lw_ref, kd_ref, bd_ref, y_ref, h_ref,
                 lincl_ref, msd_ref, mso_ref, mincl_ref, eye_ref, hm_ref, hmb_ref,
                 *, chunk, heads, groups):
    c = chunk
    wd = heads * RWKV_HEAD
    step = pl.program_id(3)

    @pl.when(step == 0)
    def _():
        h_ref[...] = jnp.zeros_like(h_ref)
        sgn = 1 - 2 * pl.program_id(0)
        iota = lambda shape, ax: lax.broadcasted_iota(jnp.int32, shape, ax)
        rel_c = (iota((c, c), 0) - iota((c, c), 1)) * sgn
        lincl_ref[...] = jnp.where(rel_c >= 0, 1.0, 0.0).astype(BF16)
        ti = iota((c, wd), 0)
        si = iota((c, wd), 1) & (c - 1)
        rel = (ti - si) * sgn
        log2b = INV_BLOCK.bit_length() - 1
        same_blk = (ti >> log2b) == (si >> log2b)
        strict = jnp.where(rel > 0, 1.0, 0.0)
        msd_ref[...] = jnp.where(same_blk, strict, 0.0)
        mso_ref[...] = jnp.where(same_blk, 0.0, strict)
        mincl_ref[...] = jnp.where(rel >= 0, 1.0, 0.0)
        eye_ref[...] = jnp.where(ti == si, 1.0, 0.0)
        hm = jnp.where((iota((wd, wd), 0) >> RWKV_HEAD_SHIFT) == (iota((wd, wd), 1) >> RWKV_HEAD_SHIFT),
                       1.0, 0.0)
        hm_ref[...] = hm
        hmb_ref[...] = hm.astype(BF16)

    n_sub = r_ref.shape[0] // c
    d = pl.program_id(0)
    rows = [pl.ds(pl.multiple_of(jnp.where(d == 0, i, n_sub - 1 - i) * c, c), c) for i in range(n_sub)]
    lanes = [slice(q * wd, (q + 1) * wd) for q in range(groups)]
    refs = (r_ref, kk_ref, v_ref, lw_ref, kd_ref, bd_ref)
    waves = [_chunk_operator_stages(*[[ref[rw, ln].astype(F32) for ln in lanes] for ref in refs],
                                    lincl_ref, msd_ref, mso_ref, mincl_ref, eye_ref, hm_ref, hmb_ref, heads)
             for rw in rows]
    h = [h_ref[q] for q in range(groups)]
    done = 0
    tick = 0
    while done < n_sub:
        for i in range(done, n_sub):
            if tick < i * SCAN_WAVE_SKEW:
                break
            try:
                next(waves[i])
            except StopIteration as fin:
                m_c, n_c, rhat, oloc = fin.value
                h, y = _advance(m_c, n_c, rhat, oloc, h, hmb_ref, heads)
                for q in range(groups):
                    y_ref[rows[i], lanes[q]] = y[q]
                done += 1
        tick += 1
    for q in range(groups):
        h_ref[q] = h[q]


def _advance(m_c, n_c, rhat, oloc, h, hmb_ref, heads):
    c = RWKV_HEAD
    hmb = hmb_ref[...]

    def stacked(zb):
        return jnp.concatenate([zb] * heads, axis=0) * hmb

    def step(mi, ri, hi):
        lhs = jnp.concatenate([mi, ri], axis=0).astype(BF16)
        return _dot(lhs, stacked(hi.astype(BF16)))

    res = [step(*a) for a in zip(m_c, rhat, h)]
    h_new = [z[:c] + ni for z, ni in zip(res, n_c)]
    y = [z[c:] + oi for z, oi in zip(res, oloc)]
    return h_new, y


def _chunk_operator_stages(r, kk, v, lw, k, b, lincl_ref, msd_ref, mso_ref, mincl_ref, eye_ref,
                           hm_ref, hmb_ref, heads):
    c = RWKV_HEAD
    wd = heads * RWKV_HEAD
    hmb = hmb_ref[...]
    each = lambda f, *ls: [f(*a) for a in zip(*ls)]

    def stacked(z):
        zb = z.astype(BF16)
        return jnp.concatenate([zb] * heads, axis=0) * hmb

    def mm(x, y_stacked, dims=NN):
        return _dot(x.astype(BF16), y_stacked, dims)

    def head_blocks(full):
        z = full * hm_ref[...]
        out = z[0:c]
        for hh in range(1, heads):
            out = out + z[hh * c:(hh + 1) * c]
        return out

    cat0 = lambda *xs: jnp.concatenate(xs, axis=0)
    cat1 = lambda *xs: jnp.concatenate(xs, axis=1)
    mul = lambda x, y: x * y
    top = lambda z: z[:c]
    bottom = lambda z: z[c:]
    eye = eye_ref[...]
    lincl = lincl_ref[...]
    g = each(lambda x: _mm_exact_rhs_left(lincl, x), lw)
    yield
    gt = each(lambda x: jnp.sum(x, axis=0, keepdims=True), lw)
    e_x = each(lambda gi, lwi: jnp.exp(gi - lwi), g, lw)
    e_g = each(jnp.exp, g)
    e_n = each(lambda gi: jnp.exp(-gi), g)
    e_c = each(lambda gti, gi: jnp.exp(gti - gi), gt, g)
    kt = each(mul, kk, e_x)
    rt = each(mul, r, e_g)
    bh = each(mul, b, e_n)
    kh = each(mul, k, e_n)
    kb = each(mul, k, e_c)
    bb = each(mul, b, e_c)

    gm = each(lambda kti, rti, bhi, khi: mm(cat0(kti, rti), cat0(stacked(bhi), stacked(khi)).T),
              kt, rt, bh, kh)
    yield
    msd = msd_ref[...]
    mso = mso_ref[...]
    m_incl = mincl_ref[...]
    a_d = each(lambda z: z[:c, :wd] * msd, gm)
    a_o = each(lambda z: z[:c, :wd] * mso, gm)
    b_m = each(lambda z: z[:c, wd:] * (msd + mso), gm)
    e_m = each(lambda z: z[c:, :wd] * m_incl, gm)
    c_m = each(lambda z: z[c:, wd:] * m_incl, gm)

    mm_st = lambda x, y: mm(x, stacked(y))
    unzip = lambda pairs: [list(t) for t in zip(*pairs)]

    def fold(xi, di):
        z = mm(cat0(xi, di), stacked(xi))
        return z[:c], di + z[c:]

    x1 = each(lambda z: -z, a_d)
    dm = each(lambda z: eye - z, a_d)
    x2 = each(mm_st, x1, x1)
    bcv = each(lambda bi, ci, vi: mm(cat0(bi, ci), stacked(vi)), b_m, c_m, v)
    bv = each(top, bcv)
    cv = each(bottom, bcv)
    yield
    x4, dm = unzip(each(fold, x2, dm))
    yield
    x8, dm = unzip(each(fold, x4, dm))
    yield
    dm = each(lambda di, xi: di + mm_st(di, xi), dm, x8)
    yield
    n1 = each(mm_st, dm, a_o)
    yield
    n2 = each(mm_st, n1, n1)
    yield
    t1 = each(lambda z: eye - z, n1)
    t1 = each(lambda ti, ni: ti + mm_st(ti, ni), t1, n2)
    yield
    t_m = each(mm_st, t1, dm)
    yield
    et = each(mm_st, e_m, t_m)
    yield
    both = each(lambda ti, ei, kti, bvi: mm(cat0(ti, ei), cat1(stacked(kti), stacked(bvi))),
                t_m, et, kt, bv)
    yield
    wt = each(lambda z: z[:c, :wd], both)
    ut = each(lambda z: z[:c, wd:], both)
    rhat = each(lambda ri, z: ri - z[c:, :wd], rt, both)
    oloc = each(lambda ci, z: ci - z[c:, wd:], cv, both)
    full_m = each(lambda bi, wi: _dot(bi.astype(BF16), wi.astype(BF16), TN), bb, wt)
    full_n = each(lambda ki, bi, vi, ui: _dot(cat0(ki, bi).astype(BF16), cat0(vi, -ui).astype(BF16), TN),
                  kb, bb, v, ut)
    m_c = each(lambda gti, fi: eye * jnp.exp(gti) - head_blocks(fi), gt, full_m)
    n_c = each(head_blocks, full_n)
    return m_c, n_c, rhat, oloc


def _mm_exact_rhs_left(a_exact, b):
    bh, bl = _split(b)
    return _dot(a_exact, bh) + _dot(a_exact, bl)


def _rwkv_scan(r, kk, v, lw, kd, bd, rows):
    c, heads = SCAN_CHUNK, SCAN_HEADS
    wd = heads * RWKV_HEAD
    nq = RWKV_WIDTH // wd
    rb = c * SCAN_SUBCHUNKS
    assert rows.ctx_len % rb == 0 and rows.seq % rb == 0
    nc_ctx = rows.ctx_len // rb
    nc_lat = rows.seq // rb
    lat_blocks = rows.n_lat // rb

    def chunk_idx(d, bi, st):
        in_ctx = st < nc_ctx
        t_ctx = jnp.where(d == 0, st, nc_ctx - 1 - st)
        sl = st - nc_ctx
        t_lat = jnp.where(d == 0, sl, nc_lat - 1 - sl)
        return jnp.where(in_ctx, lat_blocks + bi * nc_ctx + t_ctx, bi * nc_lat + t_lat)

    assert c == RWKV_HEAD
    groups = SCAN_GROUPS
    bw = wd * groups
    shared = pl.BlockSpec((rb, bw), lambda d, bi, q, st: (chunk_idx(d, bi, st), q))
    per_dir = pl.BlockSpec((None, rb, bw), lambda d, bi, q, st: (d, chunk_idx(d, bi, st), q))
    return pl.pallas_call(
        functools.partial(_scan_kernel, chunk=c, heads=heads, groups=groups),
        grid=(2, rows.batch, nq // groups, nc_ctx + nc_lat),
        in_specs=[shared, shared, shared, per_dir, per_dir, per_dir],
        out_specs=per_dir,
        out_shape=jax.ShapeDtypeStruct((2, rows.n_all, RWKV_WIDTH), F32),
        scratch_shapes=[
            pltpu.VMEM((groups, c, wd), F32),
            pltpu.VMEM((c, c), BF16),
            pltpu.VMEM((c, wd), F32),
            pltpu.VMEM((c, wd), F32),
            pltpu.VMEM((c, wd), F32),
            pltpu.VMEM((c, wd), F32),
            pltpu.VMEM((wd, wd), F32),
            pltpu.VMEM((wd, wd), BF16),
        ],
        compiler_params=_params(("parallel", "parallel", "parallel", "arbitrary")),
        name="rwkv_scan",
    )(r, kk, v, lw, kd, bd)


def _out_kernel(attn_ref, y_ref, bonus_ref, g_ref, lnw_ref, lnb_ref, x_ref, gate_ref,
                wa_ref, wr_ref, o_ref):
    out_a = _dot(attn_ref[...], wa_ref[...])
    ones = _head_ones(2 * LANES)
    y = y_ref[0] + y_ref[1]
    mean = _head_sum(y, ones) * (1.0 / RWKV_HEAD)
    yc = y - mean
    var = _head_sum(yc * yc, ones) * (1.0 / RWKV_HEAD)
    yn = yc * lax.rsqrt(var + GN_EPS) * lnw_ref[...] + lnb_ref[...]
    rw = ((yn + bonus_ref[...]) * g_ref[...]).astype(BF16)
    out = out_a + _dot(rw, wr_ref[...])
    o_ref[...] = x_ref[...] + gate_ref[...] * out


def _mix_output(attn, y, bonus, g, ln_w, ln_b, x, mod, w_out, layer, rows, n_rows):
    d = D_MODEL
    tm = min(MIX_ROWS, rows.tm)
    tn = d
    w = RWKV_WIDTH
    return pl.pallas_call(
        _out_kernel,
        grid=(n_rows // tm, d // tn),
        in_specs=[
            pl.BlockSpec((tm, Q_COLS), lambda i, j: (i, 0)),
            pl.BlockSpec((2, tm, w), lambda i, j: (0, i, 0)),
            pl.BlockSpec((tm, w), lambda i, j: (i, 0)),
            pl.BlockSpec((tm, w), lambda i, j: (i, 0)),
            pl.BlockSpec((1, w), lambda i, j: (0, 0)),
            pl.BlockSpec((1, w), lambda i, j: (0, 0)),
            pl.BlockSpec((tm, tn), lambda i, j: (i, j)),
            pl.BlockSpec((None, 1, tn), lambda i, j: (rows.mod_row(i, tm), 0, 2 * (d // tn) + j)),
            pl.BlockSpec((None, Q_COLS, tn), lambda i, j: (layer, 0, j)),
            pl.BlockSpec((None, w, tn), lambda i, j: (layer, 1, j)),
        ],
        out_specs=pl.BlockSpec((tm, tn), lambda i, j: (i, j)),
        out_shape=jax.ShapeDtypeStruct((n_rows, d), F32),
        compiler_params=_params(("parallel", "arbitrary")),
        name="mix_output",
    )(attn, y, bonus, g, ln_w, ln_b, x, mod, w_out, w_out)


def _ffn_kernel(x_ref, g_ref, sc_ref, sh_ref, gate_ref, w1_ref, w3_ref, w2_ref, fg_ref,
                o_ref, h_ref, acc_ref, *, final):
    j = pl.program_id(1)

    @pl.when(j == 0)
    def _():
        h_ref[...] = _norm_mod(x_ref[...], g_ref[...], sc_ref[...], sh_ref[...]).astype(BF16)
        acc_ref[...] = jnp.zeros_like(acc_ref)

    h = h_ref[...]
    u = _dot(h, w1_ref[...])
    a = (u * _sigmoid(u)) * _dot(h, w3_ref[...])
    acc_ref[...] += _dot(a.astype(BF16), w2_ref[...])

    @pl.when(j == pl.num_programs(1) - 1)
    def _():
        y = x_ref[...] + gate_ref[...] * acc_ref[...]
        if final:
            ms = jnp.mean(y * y, axis=-1, keepdims=True)
            y = y * lax.rsqrt(ms + NORM_EPS) * fg_ref[...]
        o_ref[...] = y


def _ffn(x, g, mod, w1, w3, w2, layer, final_g, rows, n_rows, final):
    tm, d = rows.tm, D_MODEL
    tf = FFN_COLS
    return pl.pallas_call(
        functools.partial(_ffn_kernel, final=final),
        grid=(n_rows // tm, D_FF // tf),
        in_specs=[
            pl.BlockSpec((tm, d), lambda i, j: (i, 0)),
            pl.BlockSpec((1, d), lambda i, j: (0, 0)),
            _mod_spec(rows, 4, 2),
            _mod_spec(rows, 3, 2),
            _mod_spec(rows, 5, 2),
            pl.BlockSpec((None, d, tf), lambda i, j: (layer, 0, j)),
            pl.BlockSpec((None, d, tf), lambda i, j: (layer, 0, j)),
            pl.BlockSpec((None, tf, d), lambda i, j: (layer, j, 0)),
            pl.BlockSpec((1, d), lambda i, j: (0, 0)),
        ],
        out_specs=pl.BlockSpec((tm, d), lambda i, j: (i, 0)),
        out_shape=jax.ShapeDtypeStruct((n_rows, d), F32),
        scratch_shapes=[pltpu.VMEM((tm, d), BF16), pltpu.VMEM((tm, d), F32)],
        compiler_params=_params(("parallel", "arbitrary")),
        name="ffn_final" if final else "ffn",
    )(x, g, mod, mod, mod, w1, w3, w2, final_g)


def _rope_tables(seq, tm):
    n_rows = seq // GRID_W
    row = jnp.broadcast_to(jnp.arange(n_rows)[:, None], (n_rows, GRID_W)).reshape(-1)
    col = jnp.broadcast_to(jnp.arange(GRID_W)[None, :], (n_rows, GRID_W)).reshape(-1)
    inv = ROPE_THETA ** (-jnp.arange(ROPE_FREQS, dtype=F32) / ROPE_FREQS)
    ang = jnp.concatenate([row[:, None].astype(F32) * inv, col[:, None].astype(F32) * inv], axis=-1)
    cos, sin = jnp.cos(ang), jnp.sin(ang)
    cos2 = jnp.concatenate([cos, cos], axis=-1)
    sin2 = jnp.concatenate([-sin, sin], axis=-1)
    cos2 = jnp.concatenate([cos2, jnp.ones((tm, HEAD_DIM), F32)], axis=0)
    sin2 = jnp.concatenate([sin2, jnp.zeros((tm, HEAD_DIM), F32)], axis=0)
    return cos2, sin2


def _pad_lora_cols(a):
    parts = [a[..., :LORA_GATE]]
    pad = [(0, 0)] * (a.ndim - 1) + [(0, LORA_PAD - LORA_DECAY)]
    for q in range(4):
        lo = LORA_GATE + q * LORA_DECAY
        parts.append(jnp.pad(a[..., lo:lo + LORA_DECAY], pad))
    return jnp.concatenate(parts, axis=-1)


def _mix_weights(w2, a2):
    mats = jnp.concatenate([w2, a2], axis=0)
    return jnp.pad(mats, ((0, 0), (0, LORA_PAD - LORA_DECAY), (0, 0)))


def kernel(x, c, ctx, c_ctx, norm1_g, norm2_g, ada_w, ada_b, w_in, q_gain, k_gain, mu_prev, mu_next,
           w0, w2, a0, a2, g2, k_k, k_a, r_k, ln_x_w, ln_x_b, w_out, ffn_w1, ffn_w3, ffn_w2, final_g):
    batch, seq, d = x.shape
    ctx_len = ctx.shape[1]
    depth = w_in.shape[0]
    rows = _Rows(batch, seq, ctx_len, min(ROW_BLOCK, batch * ctx_len))

    cond = jnp.concatenate([c, c_ctx[None, :], jnp.zeros((SUBLANES - batch - 1, d), F32)], axis=0)
    mod_all = _modulation(cond, ada_w, ada_b)
    mod_all = mod_all.reshape(depth, SUBLANES, 1, 6 * d)
    cos2, sin2 = _rope_tables(seq, rows.tm)

    w_attn_b = w_in[:, :, :ATTN_COLS].astype(BF16)
    split = ATTN_COLS + RKV_COLS
    w_rwkv_b = jnp.concatenate([w_in[:, :, ATTN_COLS:split], _pad_lora_cols(w_in[:, :, split:])],
                               axis=-1).astype(BF16)
    pad_mu = lambda mu: jnp.concatenate([mu[:, :RKV_COLS], _pad_lora_cols(mu[:, RKV_COLS:])], axis=-1)
    mu_prev_p = pad_mu(mu_prev)
    mu_next_p = pad_mu(mu_next)
    w_out_b = w_out.astype(BF16)
    w1_b = ffn_w1.astype(BF16)
    w3_b = ffn_w3.astype(BF16)
    w2_b = ffn_w2.astype(BF16)

    tok = jnp.concatenate([x.reshape(batch * seq, d), ctx.reshape(batch * ctx_len, d)], axis=0)
    row1 = lambda a: a.reshape(1, -1)
    for i in range(depth):
        last = i == depth - 1
        mod = mod_all[i]
        g1 = row1(norm1_g[i])
        q, k, vt = _attn_project(tok, g1, mod, w_attn_b, i, row1(q_gain[i]), row1(k_gain[i]),
                                 cos2, sin2, rows)
        attn = _attention(q, k, vt, rows, latent=True)
        if not last:
            attn = _attention(q, k, vt, rows, latent=False, out=attn)
        prep = {
            'mu_prev': row1(mu_prev_p[i]), 'mu_next': row1(mu_next_p[i]),
            'g2': g2[i].astype(BF16), 'wmix': _mix_weights(w2[i], a2[i]).astype(BF16),
            'w0': w0[i], 'a0': a0[i],
            'k_k': row1(k_k[i]), 'k_a': row1(k_a[i]), 'r_k': row1(r_k[i]),
        }
        r_s, kk_s, v_s, g_s, bonus, lw, kd, bd = _rwkv_in(tok, g1, mod, w_rwkv_b, i, prep, rows)
        y = _rwkv_scan(r_s, kk_s, v_s, lw, kd, bd, rows)
        n_rows = rows.n_lat if last else rows.n_all
        tok = _mix_output(attn, y, bonus, g_s, row1(ln_x_w[i]), row1(ln_x_b[i]), tok, mod,
                          w_out_b, i, rows, n_rows)
        tok = _ffn(tok, row1(norm2_g[i]), mod, w1_b, w3_b, w2_b, i, row1(final_g),
                   rows, n_rows, final=last)
    return tok.reshape(batch, seq, d)
```
